```python
import jax, jax.numpy as jnp
from jax import lax
import numpy as np

D_MODEL = 1024
BATCH = 4
SEQ = 4096
DEPTH = 2

N_BRANCH = 3
POOL_WINDOWS = (2, 4, 8, 16)
POOL_WIDTH = D_MODEL // 2
POOL_GROUP = POOL_WIDTH // len(POOL_WINDOWS)
CONV_WIDTH = D_MODEL // 2
CONV_K = 3
HEAD_DIM = 64
ATTN_GROUPS = ((128, 1), (512, 4), (2048, 16))
HEADS_PER_GROUP = 4
N_HEADS = HEADS_PER_GROUP * len(ATTN_GROUPS)
ATTN_WIDTH = N_HEADS * HEAD_DIM
ATTN_OUT = HEADS_PER_GROUP * HEAD_DIM
ATTN_BLOCK = 128
D_FF = 4 * D_MODEL
EPS = 1e-6
MASK_VALUE = -1e30

OFF_POOL = 0
OFF_CONV_B = OFF_POOL + POOL_WIDTH
OFF_CONV_C = OFF_CONV_B + CONV_WIDTH
OFF_CONV_X = OFF_CONV_C + CONV_WIDTH
OFF_Q = OFF_CONV_X + CONV_WIDTH
OFF_K = OFF_Q + ATTN_WIDTH
OFF_V = OFF_K + ATTN_WIDTH
OFF_GATE = OFF_V + ATTN_WIDTH
IN_COLS = OFF_GATE + N_BRANCH * D_MODEL

kernel_name = "hybrid_pool_conv_dilated_attn_block"


def rms_norm(x, gain):
    xf = x.astype(jnp.float32)
    y = xf * lax.rsqrt(jnp.mean(xf * xf, axis=-1, keepdims=True) + EPS)
    return (y * gain.astype(jnp.float32)).astype(x.dtype)


def multiscale_pool(u, w_mix, scale):
    B, S, _ = u.shape
    cs = jnp.cumsum(u.astype(jnp.float32), axis=1)
    pos = jnp.arange(S)
    diffs = []
    for g, w in enumerate(POOL_WINDOWS):
        c = cs[..., g * POOL_GROUP:(g + 1) * POOL_GROUP]
        lag = jnp.pad(c, ((0, 0), (w, 0), (0, 0)))[:, :S]
        count = jnp.minimum(pos + 1, w).astype(jnp.float32)[None, :, None]
        diffs.append((c - lag) / count - u[..., g * POOL_GROUP:(g + 1) * POOL_GROUP].astype(jnp.float32))
    d = jnp.stack(diffs, axis=2).astype(u.dtype)
    y = jnp.einsum('bsgc,gcd->bsgd', d, w_mix).reshape(B, S, POOL_WIDTH)
    return y * scale


def short_gated_conv(b_gate, c_gate, xin, conv_w):
    S = xin.shape[1]
    u = c_gate * xin
    up = jnp.pad(u, ((0, 0), (CONV_K - 1, 0), (0, 0)))
    y = conv_w[CONV_K - 1] * up[:, CONV_K - 1:CONV_K - 1 + S]
    for j in range(CONV_K - 1):
        y = y + conv_w[j] * up[:, j:j + S]
    return b_gate * y


def dilated_window_attention(q, k, v, window, dilation):
    B, S, H, Dh = q.shape
    span = window // dilation
    assert span <= ATTN_BLOCK
    L = S // dilation
    nb = -(-L // ATTN_BLOCK)
    Lp = nb * ATTN_BLOCK

    def fold(t):
        t = t.reshape(B, L, dilation, H, Dh).transpose(0, 2, 3, 1, 4)
        t = jnp.pad(t, ((0, 0), (0, 0), (0, 0), (0, Lp - L), (0, 0)))
        return t.reshape(B, dilation, H, nb, ATTN_BLOCK, Dh)

    def with_prev(t):
        prev = jnp.pad(t, ((0, 0), (0, 0), (0, 0), (1, 0), (0, 0), (0, 0)))[:, :, :, :nb]
        return jnp.concatenate([prev, t], axis=4)

    qb = fold(q).astype(jnp.float32)
    kc = with_prev(fold(k)).astype(jnp.float32)
    vc = with_prev(fold(v)).astype(jnp.float32)
    s = jnp.einsum('bdhnqc,bdhnkc->bdhnqk', qb, kc) * (Dh ** -0.5)
    qi = jnp.arange(ATTN_BLOCK)[:, None]
    ki = jnp.arange(2 * ATTN_BLOCK)[None, :] - ATTN_BLOCK
    rel = qi - ki
    band = (rel >= 0) & (rel <= span)
    has_prev = (jnp.arange(nb) > 0)[:, None, None] | (ki >= 0)[None]
    mask = band[None] & has_prev
    s = jnp.where(mask, s, MASK_VALUE)
    m = jnp.max(s, axis=-1, keepdims=True)
    p = jnp.exp(s - m)
    den = jnp.sum(p, axis=-1, keepdims=True)
    o = jnp.einsum('bdhnqk,bdhnkc->bdhnqc', p, vc) / den
    lse = (m + jnp.log(den))[..., 0]
    o = o.reshape(B, dilation, H, Lp, Dh)[:, :, :, :L].transpose(0, 3, 1, 2, 4).reshape(B, S, H, Dh)
    lse = lse.reshape(B, dilation, H, Lp)[..., :L].transpose(0, 3, 1, 2).reshape(B, S, H)
    return o.astype(q.dtype), lse


def dilated_mixture_attention(q, k, v):
    B, S = q.shape[:2]
    outs, lses = [], []
    for g, (window, dilation) in enumerate(ATTN_GROUPS):
        hs = slice(g * HEADS_PER_GROUP, (g + 1) * HEADS_PER_GROUP)
        o, lse = dilated_window_attention(q[:, :, hs], k[:, :, hs], v[:, :, hs], window, dilation)
        outs.append(o)
        lses.append(lse)
    wts = jax.nn.softmax(jnp.stack(lses, axis=0), axis=0)
    o = jnp.sum(wts[..., None] * jnp.stack(outs, axis=0).astype(jnp.float32), axis=0)
    return o.reshape(B, S, ATTN_OUT).astype(q.dtype)


def hybrid_layer(x, norm_mix, w_in, b_gate, pool_mix, pool_scale, conv_w, q_gain, k_gain,
                 w_pool_up, w_conv_out, w_attn_up, w_o, norm_mlp, w_ff1, w_ff2):
    B, S, D = x.shape
    h = rms_norm(x, norm_mix)
    z = jnp.einsum('bsd,dc->bsc', h, w_in)
    y_pool = multiscale_pool(z[..., OFF_POOL:OFF_CONV_B], pool_mix, pool_scale)
    y_conv = short_gated_conv(z[..., OFF_CONV_B:OFF_CONV_C], z[..., OFF_CONV_C:OFF_CONV_X],
                              z[..., OFF_CONV_X:OFF_Q], conv_w)
    q = rms_norm(z[..., OFF_Q:OFF_K].reshape(B, S, N_HEADS, HEAD_DIM), q_gain)
    k = rms_norm(z[..., OFF_K:OFF_V].reshape(B, S, N_HEADS, HEAD_DIM), k_gain)
    v = z[..., OFF_V:OFF_GATE].reshape(B, S, N_HEADS, HEAD_DIM)
    y_attn = dilated_mixture_attention(q, k, v)
    gates = jax.nn.sigmoid((z[..., OFF_GATE:] + b_gate).astype(jnp.float32)).astype(x.dtype)
    gates = gates.reshape(B, S, N_BRANCH, D)
    merged = (gates[:, :, 0] * (y_pool @ w_pool_up)
              + gates[:, :, 1] * (y_conv @ w_conv_out)
              + gates[:, :, 2] * (y_attn @ w_attn_up))
    x = x + merged @ w_o
    h2 = rms_norm(x, norm_mlp)
    x = x + jnp.square(jax.nn.relu(h2 @ w_ff1)) @ w_ff2
    return x


def setup_inputs(seed: int = 0) -> dict:
    key = jax.random.key(seed)
    ks = jax.random.split(key, 17)
    L, D = DEPTH, D_MODEL

    def nrm(k, shape, fan_in):
        return jax.random.normal(k, shape, jnp.float32) * (fan_in ** -0.5)

    def gain(k, shape):
        return 1.0 + 0.02 * jax.random.normal(k, shape, jnp.float32)

    return {
        "x": jax.random.normal(ks[0], (BATCH, SEQ, D), jnp.float32),
        "norm_mix": gain(ks[1], (L, D)),
        "w_in": nrm(ks[2], (L, D, IN_COLS), D),
        "b_gate": 0.01 * jax.random.normal(ks[3], (L, N_BRANCH * D), jnp.float32),
        "pool_mix": nrm(ks[4], (L, len(POOL_WINDOWS), POOL_GROUP, POOL_GROUP), POOL_GROUP),
        "pool_scale": gain(ks[5], (L, POOL_WIDTH)),
        "conv_w": nrm(ks[6], (L, CONV_K, CONV_WIDTH), CONV_K),
        "q_gain": gain(ks[7], (L, HEAD_DIM)),
        "k_gain": gain(ks[8], (L, HEAD_DIM)),
        "w_pool_up": nrm(ks[9], (L, POOL_WIDTH, D), POOL_WIDTH),
        "w_conv_out": nrm(ks[10], (L, CONV_WIDTH, D), CONV_WIDTH),
        "w_attn_up": nrm(ks[11], (L, ATTN_OUT, D), ATTN_OUT),
        "w_o": nrm(ks[12], (L, D, D), D),
        "norm_mlp": gain(ks[13], (L, D)),
        "w_ff1": nrm(ks[14], (L, D, D_FF), D),
        "w_ff2": nrm(ks[15], (L, D_FF, D), D_FF),
    }


def reference(x, norm_mix, w_in, b_gate, pool_mix, pool_scale, conv_w, q_gain, k_gain,
              w_pool_up, w_conv_out, w_attn_up, w_o, norm_mlp, w_ff1, w_ff2):
    for l in range(DEPTH):
        x = hybrid_layer(x, norm_mix[l], w_in[l], b_gate[l], pool_mix[l], pool_scale[l], conv_w[l],
                         q_gain[l], k_gain[l], w_pool_up[l], w_conv_out[l], w_attn_up[l], w_o[l],
                         norm_mlp[l], w_ff1[l], w_ff2[l])
    return x
```

```python
import functools

import jax
import jax.numpy as jnp
from jax import lax
from jax.experimental import pallas as pl
from jax.experimental.pallas import tpu as pltpu

F32 = jnp.float32
BF16 = jnp.bfloat16

POOL_WINDOWS = (2, 4, 8, 16)
POOL_GROUP = 128
POOL_WIDTH = POOL_GROUP * len(POOL_WINDOWS)
CONV_WIDTH = 512
CONV_K = 3
HEAD_DIM = 64
ATTN_GROUPS = ((128, 1), (512, 4), (2048, 16))
HEADS_PER_GROUP = 4
GROUP_WIDTH = HEADS_PER_GROUP * HEAD_DIM
N_GROUPS = len(ATTN_GROUPS)
ATTN_WIDTH = N_GROUPS * GROUP_WIDTH
ATTN_BLOCK = 128
N_BRANCH = 3
EPS = 1e-6
MASK_VALUE = -1e30

OFF_POOL = 0
OFF_CONV = OFF_POOL + POOL_WIDTH
OFF_Q = OFF_CONV + 3 * CONV_WIDTH
OFF_K = OFF_Q + ATTN_WIDTH
OFF_V = OFF_K + ATTN_WIDTH
OFF_GATE = OFF_V + ATTN_WIDTH

POOL_HALO = 16
CONV_HALO = 8

VMEM_LIMIT_BYTES = 56 * 1024 * 1024
TM_IN = 512
TM_MERGE = 512
TM_FFN = 512
FF_CHUNK = 1024
ATTN_RESIDUES_PER_STEP = (1, 4, 4)


def _resident(shape):
    return pl.BlockSpec(shape, lambda *_: (0,) * len(shape), pipeline_mode=pl.Buffered(1))


def _rms_norm(x, gain):
    ms = jnp.mean(x * x, axis=-1, keepdims=True)
    return x * lax.rsqrt(ms + EPS) * gain


def _dot(a, b):
    return jnp.dot(a, b, preferred_element_type=F32)


def _in_proj_kernel(x_ref, gain_ref, w_ref, qgain_ref, kgain_ref, convw_ref, headmean_ref,
                    d_ref, yc_ref, q0_ref, q1_ref, q2_ref, k0_ref, k1_ref, k2_ref,
                    v0_ref, v1_ref, v2_ref, ubuf, cbuf, *, tm, tiles_per_seq):
    i = pl.program_id(0)
    tile_in_seq = i % tiles_per_seq
    h = _rms_norm(x_ref[...], gain_ref[...]).astype(BF16)

    u = _dot(h, w_ref[:, OFF_POOL:OFF_POOL + POOL_WIDTH])

    @pl.when(tile_in_seq == 0)
    def _():
        ubuf[0:POOL_HALO, :] = jnp.zeros((POOL_HALO, POOL_WIDTH), F32)

    @pl.when(tile_in_seq != 0)
    def _():
        ubuf[0:POOL_HALO, :] = ubuf[tm:tm + POOL_HALO, :]

    ubuf[POOL_HALO:POOL_HALO + tm, :] = u
    pos = tile_in_seq * tm + lax.broadcasted_iota(jnp.int32, (tm, 1), 0)
    for g, w in enumerate(POOL_WINDOWS):
        cols = slice(g * POOL_GROUP, (g + 1) * POOL_GROUP)
        u_g = ubuf[POOL_HALO:POOL_HALO + tm, cols]
        acc = u_g
        for j in range(1, w):
            acc = acc + ubuf[POOL_HALO - j:POOL_HALO - j + tm, cols]
        count = jnp.minimum(pos + 1, w).astype(F32)
        d_ref[:, cols] = (acc / count - u_g).astype(BF16)

    zc = _dot(h, w_ref[:, OFF_CONV:OFF_CONV + 3 * CONV_WIDTH])
    b_gate = zc[:, 0:CONV_WIDTH]
    uc = zc[:, CONV_WIDTH:2 * CONV_WIDTH] * zc[:, 2 * CONV_WIDTH:3 * CONV_WIDTH]

    @pl.when(tile_in_seq == 0)
    def _():
        cbuf[0:CONV_HALO, :] = jnp.zeros((CONV_HALO, CONV_WIDTH), F32)

    @pl.when(tile_in_seq != 0)
    def _():
        cbuf[0:CONV_HALO, :] = cbuf[tm:tm + CONV_HALO, :]

    cbuf[CONV_HALO:CONV_HALO + tm, :] = uc
    y = convw_ref[CONV_K - 1:CONV_K, :] * uc
    for j in range(CONV_K - 1):
        lag = CONV_K - 1 - j
        y = y + convw_ref[j:j + 1, :] * cbuf[CONV_HALO - lag:CONV_HALO - lag + tm, :]
    yc_ref[...] = (b_gate * y).astype(BF16)

    headmean = headmean_ref[...]
    for g, (q_ref, k_ref, v_ref) in enumerate(((q0_ref, k0_ref, v0_ref),
                                               (q1_ref, k1_ref, v1_ref),
                                               (q2_ref, k2_ref, v2_ref))):
        for off, gain, o_ref in ((OFF_Q, qgain_ref, q_ref), (OFF_K, kgain_ref, k_ref)):
            z = _dot(h, w_ref[:, off + g * GROUP_WIDTH:off + (g + 1) * GROUP_WIDTH])
            ms = _dot((z * z).astype(BF16), headmean)
            o_ref[...] = (z * lax.rsqrt(ms + EPS) * gain[...]).astype(BF16)
        v_ref[...] = _dot(h, w_ref[:, OFF_V + g * GROUP_WIDTH:OFF_V + (g + 1) * GROUP_WIDTH]).astype(BF16)


def _in_proj(x2d, gain, w_a, qgain, kgain, convw, headmean, *, seq):
    t, dm = x2d.shape
    tm = TM_IN
    assert seq % tm == 0 and t % tm == 0
    row = lambda width: pl.BlockSpec((tm, width), lambda i: (i, 0))
    out_shapes = ([jax.ShapeDtypeStruct((t, POOL_WIDTH), BF16), jax.ShapeDtypeStruct((t, CONV_WIDTH), BF16)]
                  + [jax.ShapeDtypeStruct((t, GROUP_WIDTH), BF16)] * (3 * N_GROUPS))
    out_specs = [row(POOL_WIDTH), row(CONV_WIDTH)] + [row(GROUP_WIDTH)] * (3 * N_GROUPS)
    return pl.pallas_call(
        functools.partial(_in_proj_kernel, tm=tm, tiles_per_seq=seq // tm),
        grid=(t // tm,),
        in_specs=[row(dm), _resident((1, dm)), _resident(w_a.shape), _resident((1, GROUP_WIDTH)),
                  _resident((1, GROUP_WIDTH)), _resident((CONV_K, CONV_WIDTH)),
                  _resident((GROUP_WIDTH, GROUP_WIDTH))],
        out_specs=out_specs,
        out_shape=out_shapes,
        scratch_shapes=[pltpu.VMEM((tm + POOL_HALO, POOL_WIDTH), F32),
                        pltpu.VMEM((tm + CONV_HALO, CONV_WIDTH), F32)],
        compiler_params=pltpu.CompilerParams(dimension_semantics=("arbitrary",),
                                             vmem_limit_bytes=VMEM_LIMIT_BYTES),
        name="in_proj",
    )(x2d, gain, w_a, qgain, kgain, convw, headmean)


def _attn_block(qb, kb, vb, valid):
    lane_head = lax.broadcasted_iota(jnp.int32, (1, GROUP_WIDTH), 1) // HEAD_DIM
    out = jnp.zeros((ATTN_BLOCK, GROUP_WIDTH), F32)
    lse = jnp.zeros((ATTN_BLOCK, GROUP_WIDTH), F32)
    for hd in range(HEADS_PER_GROUP):
        in_head = lane_head == hd
        qh = jnp.where(in_head, qb, jnp.zeros_like(qb))
        s = lax.dot_general(qh, kb, (((1,), (1,)), ((), ())), preferred_element_type=F32)
        s = jnp.where(valid, s, MASK_VALUE)
        m = jnp.max(s, axis=-1, keepdims=True)
        p = jnp.exp(s - m)
        den = jnp.sum(p, axis=-1, keepdims=True)
        pv = _dot(p.astype(BF16), vb)
        out = jnp.where(in_head, pv / den, out)
        lse = jnp.where(in_head, m + jnp.log(den), lse)
    return out, lse


def _attn_kernel(q_ref, k_ref, v_ref, o_ref, lse_ref, *, n_blocks, residues):
    blk = ATTN_BLOCK
    qi = lax.broadcasted_iota(jnp.int32, (blk, 2 * blk), 0)
    ki = lax.broadcasted_iota(jnp.int32, (blk, 2 * blk), 1)
    valid_first = (lax.broadcasted_iota(jnp.int32, (blk, blk), 1)
                   <= lax.broadcasted_iota(jnp.int32, (blk, blk), 0))
    for r in range(residues):
        cols = slice(r * GROUP_WIDTH, (r + 1) * GROUP_WIDTH)
        out, lse = _attn_block(q_ref[0, 0:blk, cols], k_ref[0, 0:blk, cols], v_ref[0, 0:blk, cols],
                               valid_first)
        o_ref[0, 0:blk, cols] = out
        lse_ref[0, 0:blk, cols] = lse

        def body(n, carry, cols=cols):
            cur = pl.multiple_of(n * blk, blk)
            prev = pl.multiple_of((n - 1) * blk, blk)
            valid = (ki >= qi) & (ki <= qi + blk)
            out, lse = _attn_block(q_ref[0, pl.ds(cur, blk), cols], k_ref[0, pl.ds(prev, 2 * blk), cols],
                                   v_ref[0, pl.ds(prev, 2 * blk), cols], valid)
            o_ref[0, pl.ds(cur, blk), cols] = out
            lse_ref[0, pl.ds(cur, blk), cols] = lse
            return carry

        lax.fori_loop(1, n_blocks, body, 0)


def _attention_group(q, k, v, *, batch, seq, dilation, residues):
    sub_len = seq // dilation
    assert sub_len % ATTN_BLOCK == 0 and dilation % residues == 0
    view = lambda a: a.reshape(batch, sub_len, dilation * GROUP_WIDTH)
    spec = pl.BlockSpec((1, sub_len, residues * GROUP_WIDTH), lambda b, r: (b, 0, r))
    o, lse = pl.pallas_call(
        functools.partial(_attn_kernel, n_blocks=sub_len // ATTN_BLOCK, residues=residues),
        grid=(batch, dilation // residues),
        in_specs=[spec, spec, spec],
        out_specs=[spec, spec],
        out_shape=[jax.ShapeDtypeStruct((batch, sub_len, dilation * GROUP_WIDTH), F32)] * 2,
        compiler_params=pltpu.CompilerParams(dimension_semantics=("arbitrary", "arbitrary"),
                                             vmem_limit_bytes=VMEM_LIMIT_BYTES),
        name=f"attn_d{dilation}",
    )(view(q), view(k), view(v))
    return o.reshape(batch * seq, GROUP_WIDTH), lse.reshape(batch * seq, GROUP_WIDTH)


def _merge_kernel(x_ref, gain_ref, wg_ref, bg_ref, d_ref, wmix_ref, pscale_ref, yc_ref,
                  o0_ref, o1_ref, o2_ref, l0_ref, l1_ref, l2_ref,
                  wp_ref, wc_ref, wa_ref, wo_ref, gain2_ref, xo_ref, h2_ref):
    dm = x_ref.shape[1]
    x = x_ref[...]
    h = _rms_norm(x, gain_ref[...]).astype(BF16)

    l0, l1, l2 = l0_ref[...], l1_ref[...], l2_ref[...]
    lmax = jnp.maximum(jnp.maximum(l0, l1), l2)
    e0, e1, e2 = jnp.exp(l0 - lmax), jnp.exp(l1 - lmax), jnp.exp(l2 - lmax)
    den = e0 + e1 + e2
    y_attn = ((e0 / den) * o0_ref[...] + (e1 / den) * o1_ref[...] + (e2 / den) * o2_ref[...]).astype(BF16)

    y_pool = jnp.concatenate(
        [_dot(d_ref[:, g * POOL_GROUP:(g + 1) * POOL_GROUP], wmix_ref[g]) for g in range(len(POOL_WINDOWS))],
        axis=-1)
    y_pool = (y_pool * pscale_ref[...]).astype(BF16)

    merged = None
    for j, (y, w_ref) in enumerate(((y_pool, wp_ref), (yc_ref[...], wc_ref), (y_attn, wa_ref))):
        gate = jax.nn.sigmoid(_dot(h, wg_ref[:, j * dm:(j + 1) * dm]) + bg_ref[:, j * dm:(j + 1) * dm])
        term = gate * _dot(y, w_ref[...])
        merged = term if merged is None else merged + term

    x_new = x + _dot(merged.astype(BF16), wo_ref[...])
    xo_ref[...] = x_new
    h2_ref[...] = _rms_norm(x_new, gain2_ref[...]).astype(BF16)


def _merge(x2d, gain, w_gate, b_gate, d, w_mix, pool_scale, yc, outs, lses, w_pool_up, w_conv_out,
           w_attn_up, w_o, gain2):
    t, dm = x2d.shape
    tm = TM_MERGE
    assert t % tm == 0
    row = lambda width: pl.BlockSpec((tm, width), lambda i: (i, 0))
    return pl.pallas_call(
        _merge_kernel,
        grid=(t // tm,),
        in_specs=[row(dm), _resident((1, dm)), _resident(w_gate.shape), _resident(b_gate.shape),
                  row(POOL_WIDTH), _resident(w_mix.shape), _resident((1, POOL_WIDTH)), row(CONV_WIDTH)]
                 + [row(GROUP_WIDTH)] * (2 * N_GROUPS)
                 + [_resident(w_pool_up.shape), _resident(w_conv_out.shape), _resident(w_attn_up.shape),
                    _resident(w_o.shape), _resident((1, dm))],
        out_specs=[row(dm), row(dm)],
        out_shape=[jax.ShapeDtypeStruct((t, dm), F32), jax.ShapeDtypeStruct((t, dm), BF16)],
        compiler_params=pltpu.CompilerParams(dimension_semantics=("arbitrary",),
                                             vmem_limit_bytes=VMEM_LIMIT_BYTES),
        name="merge",
    )(x2d, gain, w_gate, b_gate, d, w_mix, pool_scale, yc, *outs, *lses,
      w_pool_up, w_conv_out, w_attn_up, w_o, gain2)


def _ffn_kernel(x_ref, h2_ref, w1_ref, w2_ref, o_ref):
    d_ff = w1_ref.shape[1]
    h2 = h2_ref[...]
    acc = None
    for c in range(d_ff // FF_CHUNK):
        cols = slice(c * FF_CHUNK, (c + 1) * FF_CHUNK)
        a = jnp.square(jnp.maximum(_dot(h2, w1_ref[:, cols]), 0.0)).astype(BF16)
        part = _dot(a, w2_ref[cols, :])
        acc = part if acc is None else acc + part
    o_ref[...] = x_ref[...] + acc


def _ffn(x2d, h2, w1, w2):
    t, dm = x2d.shape
    tm = TM_FFN
    assert t % tm == 0 and w1.shape[1] % FF_CHUNK == 0
    row = pl.BlockSpec((tm, dm), lambda i: (i, 0))
    return pl.pallas_call(
        _ffn_kernel,
        grid=(t // tm,),
        in_specs=[row, row, _resident(w1.shape), _resident(w2.shape)],
        out_specs=row,
        out_shape=jax.ShapeDtypeStruct((t, dm), F32),
        compiler_params=pltpu.CompilerParams(dimension_semantics=("arbitrary",),
                                             vmem_limit_bytes=VMEM_LIMIT_BYTES),
        name="ffn",
    )(x2d, h2, w1, w2)


def kernel(x, norm_mix, w_in, b_gate, pool_mix, pool_scale, conv_w, q_gain, k_gain, w_pool_up, w_conv_out,
           w_attn_up, w_o, norm_mlp, w_ff1, w_ff2):
    batch, seq, dm = x.shape
    depth = norm_mix.shape[0]
    assert w_in.shape[2] == OFF_GATE + N_BRANCH * dm
    head_id = jnp.arange(GROUP_WIDTH) // HEAD_DIM
    headmean = jnp.where(head_id[:, None] == head_id[None, :], 1.0 / HEAD_DIM, 0.0).astype(BF16)
    x2d = x.reshape(batch * seq, dm)
    for l in range(depth):
        w_in_l = w_in[l].astype(BF16)
        qgain = (jnp.tile(q_gain[l], HEADS_PER_GROUP) * (HEAD_DIM ** -0.5)).reshape(1, GROUP_WIDTH)
        kgain = jnp.tile(k_gain[l], HEADS_PER_GROUP).reshape(1, GROUP_WIDTH)
        res = _in_proj(x2d, norm_mix[l].reshape(1, dm), w_in_l[:, :OFF_GATE], qgain, kgain, conv_w[l],
                       headmean, seq=seq)
        d, yc = res[0], res[1]
        qs, ks, vs = res[2:5], res[5:8], res[8:11]
        outs, lses = [], []
        for g, (_, dilation) in enumerate(ATTN_GROUPS):
            o, lse = _attention_group(qs[g], ks[g], vs[g], batch=batch, seq=seq, dilation=dilation,
                                      residues=ATTN_RESIDUES_PER_STEP[g])
            outs.append(o)
            lses.append(lse)
        x2d, h2 = _merge(x2d, norm_mix[l].reshape(1, dm), w_in_l[:, OFF_GATE:], b_gate[l].reshape(1, N_BRANCH * dm),
                         d, pool_mix[l].astype(BF16), pool_scale[l].reshape(1, POOL_WIDTH), yc, outs, lses,
                         w_pool_up[l].astype(BF16), w_conv_out[l].astype(BF16), w_attn_up[l].astype(BF16),
                         w_o[l].astype(BF16), norm_mlp[l].reshape(1, dm))
        x2d = _ffn(x2d, h2, w_ff1[l].astype(BF16), w_ff2[l].astype(BF16))
    return x2d.reshape(batch, seq, dm)
```

```python
import functools

import jax
import jax.numpy as jnp
from jax import lax
from jax.experimental import pallas as pl
from jax.experimental.pallas import tpu as pltpu

F32 = jnp.float32
BF16 = jnp.bfloat16

POOL_WINDOWS = (2, 4, 8, 16)
POOL_GROUP = 128
POOL_WIDTH = POOL_GROUP * len(POOL_WINDOWS)
CONV_WIDTH = 512
CONV_K = 3
HEAD_DIM = 64
ATTN_GROUPS = ((128, 1), (512, 4), (2048, 16))
HEADS_PER_GROUP = 4
GROUP_WIDTH = HEADS_PER_GROUP * HEAD_DIM
N_GROUPS = len(ATTN_GROUPS)
ATTN_WIDTH = N_GROUPS * GROUP_WIDTH
ATTN_BLOCK = 128
LANES = 128
N_BRANCH = 3
EPS = 1e-6
MASK_VALUE = -1e30

OFF_POOL = 0
OFF_CONV = OFF_POOL + POOL_WIDTH
OFF_Q = OFF_CONV + 3 * CONV_WIDTH
OFF_K = OFF_Q + ATTN_WIDTH
OFF_V = OFF_K + ATTN_WIDTH
OFF_GATE = OFF_V + ATTN_WIDTH

POOL_HALO = 16
CONV_HALO = 8

VMEM_LIMIT_BYTES = 56 * 1024 * 1024
TM_IN = 512
TM_MERGE = 512
TM_FFN = 512
FF_CHUNK = 1024
OUT_CHUNK = 512


def _resident(shape):
    return pl.BlockSpec(shape, lambda *_: (0,) * len(shape), pipeline_mode=pl.Buffered(1))


def _rms_norm(x, gain):
    ms = jnp.mean(x * x, axis=-1, keepdims=True)
    return x * lax.rsqrt(ms + EPS) * gain


def _dot(a, b):
    return jnp.dot(a, b, preferred_element_type=F32)


def _halves_scratch(rows):
    return pltpu.VMEM((GROUP_WIDTH // LANES, rows, LANES), F32)


def _put_halves(ref, rows, val):
    for j in range(GROUP_WIDTH // LANES):
        ref[j, rows, :] = val[:, j * LANES:(j + 1) * LANES]


def _get_halves(ref, rows):
    return jnp.concatenate([ref[j, rows, :] for j in range(GROUP_WIDTH // LANES)], axis=-1)


def _in_proj_kernel(x_ref, gain_ref, w_ref, qgain_ref, kgain_ref, convw_ref, headmean_ref,
                    d_ref, yc_ref, q_ref, k_ref, v_ref, uhalo, chalo, zbuf, *, tm, tiles_per_seq):
    i = pl.program_id(0)

    @pl.when(i == 0)
    def _():
        uhalo[...] = jnp.zeros_like(uhalo)
        chalo[...] = jnp.zeros_like(chalo)

    tile_in_seq = i % tiles_per_seq
    carry = tile_in_seq != 0
    h = _rms_norm(x_ref[...], gain_ref[...]).astype(BF16)

    u = _dot(h, w_ref[:, OFF_POOL:OFF_POOL + POOL_WIDTH])
    ext = jnp.concatenate([jnp.where(carry, uhalo[...], 0.0), u], axis=0)
    uhalo[...] = u[tm - POOL_HALO:tm, :]
    pos = tile_in_seq * tm + lax.broadcasted_iota(jnp.int32, (tm, 1), 0)
    run = ext
    for g, w in enumerate(POOL_WINDOWS):
        run = run + pltpu.roll(run, w // 2, 0)
        cols = slice(g * POOL_GROUP, (g + 1) * POOL_GROUP)
        inv_count = 1.0 / jnp.minimum(pos + 1, w).astype(F32)
        d_ref[:, cols] = (run[POOL_HALO:, 0:POOL_GROUP] * inv_count - u[:, cols]).astype(BF16)
        run = run[:, POOL_GROUP:]

    zc = _dot(h, w_ref[:, OFF_CONV:OFF_CONV + 3 * CONV_WIDTH])
    uc = zc[:, CONV_WIDTH:2 * CONV_WIDTH] * zc[:, 2 * CONV_WIDTH:3 * CONV_WIDTH]
    ext = jnp.concatenate([jnp.where(carry, chalo[...], 0.0), uc], axis=0)
    chalo[...] = uc[tm - CONV_HALO:tm, :]
    y = convw_ref[CONV_K - 1:CONV_K, :] * uc
    for j in range(CONV_K - 1):
        y = y + convw_ref[j:j + 1, :] * pltpu.roll(ext, CONV_K - 1 - j, 0)[CONV_HALO:, :]
    yc_ref[...] = (zc[:, 0:CONV_WIDTH] * y).astype(BF16)

    headmean = headmean_ref[...]
    for g, (_, dilation) in enumerate(ATTN_GROUPS):
        cols = slice(g * GROUP_WIDTH, (g + 1) * GROUP_WIDTH)
        for slot, (off, gain, o_ref) in enumerate(((OFF_Q, qgain_ref, q_ref), (OFF_K, kgain_ref, k_ref),
                                                   (OFF_V, None, v_ref))):
            z = _dot(h, w_ref[:, off + g * GROUP_WIDTH:off + (g + 1) * GROUP_WIDTH])
            if gain is not None:
                ms = _dot((z * z).astype(BF16), headmean)
                z = z * lax.rsqrt(ms + EPS) * gain[...]
            if dilation == 1:
                o_ref[g] = z.astype(BF16)
            else:
                buf = zbuf.at[(g - 1) * 3 + slot]
                _put_halves(buf, slice(None), z)
                chunk = tm // dilation
                for r in range(dilation):
                    rows = pl.ds(r, chunk, stride=dilation)
                    o_ref[g, r * chunk:(r + 1) * chunk, :] = _get_halves(buf, rows).astype(BF16)


def _in_proj(x2d, gain, w_in, qgain, kgain, convw, headmean, *, batch, seq):
    t, dm = x2d.shape
    tm = TM_IN
    tiles_per_seq = seq // tm
    assert seq % tm == 0 and all(tm % (16 * d) == 0 for _, d in ATTN_GROUPS)
    row = lambda width: pl.BlockSpec((tm, width), lambda i: (i, 0))
    qkv_spec = pl.BlockSpec((None, N_GROUPS, tm, GROUP_WIDTH),
                            lambda i: (i // tiles_per_seq, 0, i % tiles_per_seq, 0))
    qkv_shape = jax.ShapeDtypeStruct((batch, N_GROUPS, seq, GROUP_WIDTH), BF16)
    n_regrouped = 3 * sum(1 for _, d in ATTN_GROUPS if d > 1)
    return pl.pallas_call(
        functools.partial(_in_proj_kernel, tm=tm, tiles_per_seq=tiles_per_seq),
        grid=(t // tm,),
        in_specs=[row(dm), _resident((1, dm)), _resident(w_in.shape), _resident((1, GROUP_WIDTH)),
                  _resident((1, GROUP_WIDTH)), _resident((CONV_K, CONV_WIDTH)),
                  _resident((GROUP_WIDTH, GROUP_WIDTH))],
        out_specs=[row(POOL_WIDTH), row(CONV_WIDTH), qkv_spec, qkv_spec, qkv_spec],
        out_shape=[jax.ShapeDtypeStruct((t, POOL_WIDTH), BF16), jax.ShapeDtypeStruct((t, CONV_WIDTH), BF16),
                   qkv_shape, qkv_shape, qkv_shape],
        scratch_shapes=[pltpu.VMEM((POOL_HALO, POOL_WIDTH), F32), pltpu.VMEM((CONV_HALO, CONV_WIDTH), F32),
                        pltpu.VMEM((n_regrouped, GROUP_WIDTH // LANES, tm, LANES), F32)],
        compiler_params=pltpu.CompilerParams(dimension_semantics=("arbitrary",),
                                             vmem_limit_bytes=VMEM_LIMIT_BYTES),
        name="in_proj",
    )(x2d, gain, w_in, qgain, kgain, convw, headmean)


def _attn_block(qb, kb, vb, first):
    blk, nk = ATTN_BLOCK, kb.shape[0]
    qi = lax.broadcasted_iota(jnp.int32, (blk, nk), 0)
    ki = lax.broadcasted_iota(jnp.int32, (blk, nk), 1)
    valid = (ki <= qi) if first else ((ki >= qi) & (ki <= qi + blk))
    lane_head = lax.broadcasted_iota(jnp.int32, (1, GROUP_WIDTH), 1) // HEAD_DIM
    in_head = [lane_head == hd for hd in range(HEADS_PER_GROUP)]
    q_heads = jnp.concatenate([jnp.where(m, qb, jnp.zeros_like(qb)) for m in in_head], axis=0)
    s = lax.dot_general(q_heads, kb, (((1,), (1,)), ((), ())), preferred_element_type=F32)
    probs, row_max, row_den = [], [], []
    for hd in range(HEADS_PER_GROUP):
        sh = jnp.where(valid, s[hd * blk:(hd + 1) * blk, :], MASK_VALUE)
        m = jnp.max(sh, axis=-1, keepdims=True)
        p = jnp.exp(sh - m)
        row_max.append(m)
        row_den.append(jnp.sum(p, axis=-1, keepdims=True))
        probs.append(p.astype(BF16))
    pv = _dot(jnp.concatenate(probs, axis=0), vb)
    out = jnp.zeros((blk, GROUP_WIDTH), F32)
    lse = jnp.zeros((blk, GROUP_WIDTH), F32)
    for hd in range(HEADS_PER_GROUP):
        out = jnp.where(in_head[hd], pv[hd * blk:(hd + 1) * blk, :] / row_den[hd], out)
        lse = jnp.where(in_head[hd], row_max[hd] + jnp.log(row_den[hd]), lse)
    return out, lse


def _block_rows(ref, n, r, dilation, tile):
    chunk = tile // dilation
    piece = min(ATTN_BLOCK, chunk)
    parts = []
    for j in range(ATTN_BLOCK // piece):
        pos = n * ATTN_BLOCK + j * piece
        start = (pos // chunk) * tile + r * chunk + pos % chunk
        if not isinstance(start, int):
            start = pl.multiple_of(start, piece)
        parts.append(ref[pl.ds(start, piece), :])
    return parts[0] if len(parts) == 1 else jnp.concatenate(parts, axis=0)


def _attn_group(g, dilation, q_ref, k_ref, v_ref, y_ref, o0, l0, o1, l1, *, seq, tile):
    blk = ATTN_BLOCK
    n_blocks = seq // dilation // blk

    def token_rows(n, r):
        start = n * (blk * dilation) + r
        return pl.ds(start, blk) if dilation == 1 else pl.ds(start, blk, stride=dilation)

    def emit(n, r, out, lse):
        rows = token_rows(n, r)
        if g == 0:
            _put_halves(o0, rows, out)
            _put_halves(l0, rows, lse)
        elif g == 1:
            _put_halves(o1, rows, out)
            _put_halves(l1, rows, lse)
        else:
            la, lb = _get_halves(l0, rows), _get_halves(l1, rows)
            oa, ob = _get_halves(o0, rows), _get_halves(o1, rows)
            lmax = jnp.maximum(jnp.maximum(la, lb), lse)
            ea, eb, ec = jnp.exp(la - lmax), jnp.exp(lb - lmax), jnp.exp(lse - lmax)
            den = ea + eb + ec
            _put_halves(o0, rows, (ea / den) * oa + (eb / den) * ob + (ec / den) * out)

    def per_residue(r, carry):
        qb = _block_rows(q_ref, 0, r, dilation, tile)
        kb = _block_rows(k_ref, 0, r, dilation, tile)
        vb = _block_rows(v_ref, 0, r, dilation, tile)
        emit(0, r, *_attn_block(qb, kb, vb, first=True))

        def per_block(n, c):
            qb = _block_rows(q_ref, n, r, dilation, tile)
            kb = jnp.concatenate([_block_rows(k_ref, n - 1, r, dilation, tile),
                                  _block_rows(k_ref, n, r, dilation, tile)], axis=0)
            vb = jnp.concatenate([_block_rows(v_ref, n - 1, r, dilation, tile),
                                  _block_rows(v_ref, n, r, dilation, tile)], axis=0)
            emit(n, r, *_attn_block(qb, kb, vb, first=False))
            return c

        return lax.fori_loop(1, n_blocks, per_block, carry)

    if dilation == 1:
        per_residue(0, 0)
    else:
        lax.fori_loop(0, dilation, per_residue, 0)

    if g == N_GROUPS - 1:
        def write_out(c, carry):
            rows = pl.ds(pl.multiple_of(c * OUT_CHUNK, OUT_CHUNK), OUT_CHUNK)
            y_ref[rows, :] = _get_halves(o0, rows).astype(y_ref.dtype)
            return carry

        lax.fori_loop(0, seq // OUT_CHUNK, write_out, 0)


def _attn_kernel(q_ref, k_ref, v_ref, y_ref, o0, l0, o1, l1, *, seq, tile):
    g = pl.program_id(1)
    for gi, (_, dilation) in enumerate(ATTN_GROUPS):
        @pl.when(g == gi)
        def _(gi=gi, dilation=dilation):
            _attn_group(gi, dilation, q_ref, k_ref, v_ref, y_ref, o0, l0, o1, l1, seq=seq, tile=tile)


def _attention(q, k, v, *, tile):
    batch, _, seq, _ = q.shape
    spec = pl.BlockSpec((None, None, seq, GROUP_WIDTH), lambda b, g: (b, g, 0, 0))
    return pl.pallas_call(
        functools.partial(_attn_kernel, seq=seq, tile=tile),
        grid=(batch, N_GROUPS),
        in_specs=[spec, spec, spec],
        out_specs=pl.BlockSpec((None, seq, GROUP_WIDTH), lambda b, g: (b, 0, 0)),
        out_shape=jax.ShapeDtypeStruct((batch, seq, GROUP_WIDTH), BF16),
        scratch_shapes=[_halves_scratch(seq)] * 4,
        compiler_params=pltpu.CompilerParams(dimension_semantics=("arbitrary", "arbitrary"),
                                             vmem_limit_bytes=VMEM_LIMIT_BYTES),
        name="attention",
    )(q, k, v)


def _merge_kernel(x_ref, gain_ref, w_ref, bg_ref, d_ref, wmix_ref, pscale_ref, yc_ref, ya_ref,
                  wp_ref, wc_ref, wa_ref, wo_ref, gain2_ref, xo_ref, h2_ref):
    dm = x_ref.shape[1]
    x = x_ref[...]
    h = _rms_norm(x, gain_ref[...]).astype(BF16)

    y_pool = jnp.concatenate(
        [_dot(d_ref[:, g * POOL_GROUP:(g + 1) * POOL_GROUP], wmix_ref[g]) for g in range(len(POOL_WINDOWS))],
        axis=-1)
    y_pool = (y_pool * pscale_ref[...]).astype(BF16)

    merged = None
    for j, (y, wj_ref) in enumerate(((y_pool, wp_ref), (yc_ref[...], wc_ref), (ya_ref[...], wa_ref))):
        gate_cols = slice(OFF_GATE + j * dm, OFF_GATE + (j + 1) * dm)
        gate = jax.nn.sigmoid(_dot(h, w_ref[:, gate_cols]) + bg_ref[:, j * dm:(j + 1) * dm])
        term = gate * _dot(y, wj_ref[...])
        merged = term if merged is None else merged + term

    x_new = x + _dot(merged.astype(BF16), wo_ref[...])
    xo_ref[...] = x_new
    h2_ref[...] = _rms_norm(x_new, gain2_ref[...]).astype(BF16)


def _merge(x2d, gain, w_in, b_gate, d, w_mix, pool_scale, yc, y_attn, w_pool_up, w_conv_out, w_attn_up, w_o,
           gain2):
    t, dm = x2d.shape
    tm = TM_MERGE
    assert t % tm == 0
    row = lambda width: pl.BlockSpec((tm, width), lambda i: (i, 0))
    return pl.pallas_call(
        _merge_kernel,
        grid=(t // tm,),
        in_specs=[row(dm), _resident((1, dm)), _resident(w_in.shape), _resident(b_gate.shape),
                  row(POOL_WIDTH), _resident(w_mix.shape), _resident((1, POOL_WIDTH)), row(CONV_WIDTH),
                  row(GROUP_WIDTH), _resident(w_pool_up.shape), _resident(w_conv_out.shape),
                  _resident(w_attn_up.shape), _resident(w_o.shape), _resident((1, dm))],
        out_specs=[row(dm), row(dm)],
        out_shape=[jax.ShapeDtypeStruct((t, dm), F32), jax.ShapeDtypeStruct((t, dm), BF16)],
        compiler_params=pltpu.CompilerParams(dimension_semantics=("arbitrary",),
                                             vmem_limit_bytes=VMEM_LIMIT_BYTES),
        name="merge",
    )(x2d, gain, w_in, b_gate, d, w_mix, pool_scale, yc, y_attn, w_pool_up, w_conv_out, w_attn_up, w_o, gain2)


def _ffn_kernel(x_ref, h2_ref, w1_ref, w2_ref, o_ref):
    d_ff = w1_ref.shape[1]
    h2 = h2_ref[...]
    acc = None
    for c in range(d_ff // FF_CHUNK):
        cols = slice(c * FF_CHUNK, (c + 1) * FF_CHUNK)
        a = jnp.square(jnp.maximum(_dot(h2, w1_ref[:, cols]), 0.0)).astype(BF16)
        part = _dot(a, w2_ref[cols, :])
        acc = part if acc is None else acc + part
    o_ref[...] = x_ref[...] + acc


def _ffn(x2d, h2, w1, w2):
    t, dm = x2d.shape
    tm = TM_FFN
    assert t % tm == 0 and w1.shape[1] % FF_CHUNK == 0
    row = pl.BlockSpec((tm, dm), lambda i: (i, 0))
    return pl.pallas_call(
        _ffn_kernel,
        grid=(t // tm,),
        in_specs=[row, row, _resident(w1.shape), _resident(w2.shape)],
        out_specs=row,
        out_shape=jax.ShapeDtypeStruct((t, dm), F32),
        compiler_params=pltpu.CompilerParams(dimension_semantics=("arbitrary",),
                                             vmem_limit_bytes=VMEM_LIMIT_BYTES),
        name="ffn",
    )(x2d, h2, w1, w2)


def kernel(x, norm_mix, w_in, b_gate, pool_mix, pool_scale, conv_w, q_gain, k_gain, w_pool_up, w_conv_out,
           w_attn_up, w_o, norm_mlp, w_ff1, w_ff2):
    batch, seq, dm = x.shape
    depth = norm_mix.shape[0]
    assert w_in.shape[2] == OFF_GATE + N_BRANCH * dm
    head_id = jnp.arange(GROUP_WIDTH) // HEAD_DIM
    headmean = jnp.where(head_id[:, None] == head_id[None, :], 1.0 / HEAD_DIM, 0.0).astype(BF16)
    x2d = x.reshape(batch * seq, dm)
    for l in range(depth):
        w_in_l = w_in[l].astype(BF16)
        gain = norm_mix[l].reshape(1, dm)
        qgain = (jnp.tile(q_gain[l], HEADS_PER_GROUP) * (HEAD_DIM ** -0.5)).reshape(1, GROUP_WIDTH)
        kgain = jnp.tile(k_gain[l], HEADS_PER_GROUP).reshape(1, GROUP_WIDTH)
        d, yc, q, k, v = _in_proj(x2d, gain, w_in_l, qgain, kgain, conv_w[l], headmean, batch=batch, seq=seq)
        y_attn = _attention(q, k, v, tile=TM_IN).reshape(batch * seq, GROUP_WIDTH)
        x2d, h2 = _merge(x2d, gain, w_in_l, b_gate[l].reshape(1, N_BRANCH * dm), d, pool_mix[l].astype(BF16),
                         pool_scale[l].reshape(1, POOL_WIDTH), yc, y_attn, w_pool_up[l].astype(BF16),
                         w_conv_out[l].astype(BF16), w_attn_up[l].astype(BF16), w_o[l].astype(BF16),
                         norm_mlp[l].reshape(1, dm))
        x2d = _ffn(x2d, h2, w_ff1[l].astype(BF16), w_ff2[l].astype(BF16))
    return x2d.reshape(batch, seq, dm)
```

```python
import functools

import jax
import jax.numpy as jnp
from jax import lax
from jax.experimental import pallas as pl
from jax.experimental.pallas import tpu as pltpu

F32 = jnp.float32
BF16 = jnp.bfloat16

POOL_WINDOWS = (2, 4, 8, 16)
POOL_GROUP = 128
POOL_WIDTH = POOL_GROUP * len(POOL_WINDOWS)
CONV_WIDTH = 512
CONV_K = 3
HEAD_DIM = 64
ATTN_GROUPS = ((128, 1), (512, 4), (2048, 16))
HEADS_PER_GROUP = 4
GROUP_WIDTH = HEADS_PER_GROUP * HEAD_DIM
N_GROUPS = len(ATTN_GROUPS)
ATTN_WIDTH = N_GROUPS * GROUP_WIDTH
ATTN_BLOCK = 128
LANES = 128
N_BRANCH = 3
EPS = 1e-6
MASK_VALUE = -1e30
LOG2_E = 1.4426950408889634

OFF_POOL = 0
OFF_CONV = OFF_POOL + POOL_WIDTH
OFF_Q = OFF_CONV + 3 * CONV_WIDTH
OFF_K = OFF_Q + ATTN_WIDTH
OFF_V = OFF_K + ATTN_WIDTH
OFF_GATE = OFF_V + ATTN_WIDTH

POOL_HALO = 16
CONV_HALO = 8

VMEM_LIMIT_BYTES = 56 * 1024 * 1024
TM_IN = 512
TM_MERGE = 512
TM_FFN = 512
FF_CHUNK = 1024
OUT_CHUNK = 512


def _resident(shape):
    return pl.BlockSpec(shape, lambda *_: (0,) * len(shape), pipeline_mode=pl.Buffered(1))


def _rms_norm(x, gain):
    ms = jnp.mean(x * x, axis=-1, keepdims=True)
    return x * lax.rsqrt(ms + EPS) * gain


def _dot(a, b):
    return jnp.dot(a, b, preferred_element_type=F32)


def _halves_scratch(rows):
    return pltpu.VMEM((GROUP_WIDTH // LANES, rows, LANES), F32)


def _put_halves(ref, rows, val):
    for j in range(GROUP_WIDTH // LANES):
        ref[j, rows, :] = val[:, j * LANES:(j + 1) * LANES]


def _get_halves(ref, rows):
    return jnp.concatenate([ref[j, rows, :] for j in range(GROUP_WIDTH // LANES)], axis=-1)


def _in_proj_kernel(x_ref, gain_ref, w_ref, qgain_ref, kgain_ref, convw_ref, headmean_ref,
                    d_ref, yc_ref, q_ref, k_ref, v_ref, uhalo, chalo, zbuf, *, tm, tiles_per_seq):
    i = pl.program_id(0)

    @pl.when(i == 0)
    def _():
        uhalo[...] = jnp.zeros_like(uhalo)
        chalo[...] = jnp.zeros_like(chalo)

    tile_in_seq = i % tiles_per_seq
    carry = tile_in_seq != 0
    h = _rms_norm(x_ref[...], gain_ref[...]).astype(BF16)

    u = _dot(h, w_ref[:, OFF_POOL:OFF_POOL + POOL_WIDTH])
    ext = jnp.concatenate([jnp.where(carry, uhalo[...], 0.0), u], axis=0)
    uhalo[...] = u[tm - POOL_HALO:tm, :]
    pos = tile_in_seq * tm + lax.broadcasted_iota(jnp.int32, (tm, 1), 0)
    run = ext
    for g, w in enumerate(POOL_WINDOWS):
        run = run + pltpu.roll(run, w // 2, 0)
        cols = slice(g * POOL_GROUP, (g + 1) * POOL_GROUP)
        inv_count = 1.0 / jnp.minimum(pos + 1, w).astype(F32)
        d_ref[:, cols] = (run[POOL_HALO:, 0:POOL_GROUP] * inv_count - u[:, cols]).astype(BF16)
        run = run[:, POOL_GROUP:]

    zc = _dot(h, w_ref[:, OFF_CONV:OFF_CONV + 3 * CONV_WIDTH])
    uc = zc[:, CONV_WIDTH:2 * CONV_WIDTH] * zc[:, 2 * CONV_WIDTH:3 * CONV_WIDTH]
    ext = jnp.concatenate([jnp.where(carry, chalo[...], 0.0), uc], axis=0)
    chalo[...] = uc[tm - CONV_HALO:tm, :]
    y = convw_ref[CONV_K - 1:CONV_K, :] * uc
    for j in range(CONV_K - 1):
        y = y + convw_ref[j:j + 1, :] * pltpu.roll(ext, CONV_K - 1 - j, 0)[CONV_HALO:, :]
    yc_ref[...] = (zc[:, 0:CONV_WIDTH] * y).astype(BF16)

    headmean = headmean_ref[...]
    for g, (_, dilation) in enumerate(ATTN_GROUPS):
        cols = slice(g * GROUP_WIDTH, (g + 1) * GROUP_WIDTH)
        for slot, (off, gain, o_ref) in enumerate(((OFF_Q, qgain_ref, q_ref), (OFF_K, kgain_ref, k_ref),
                                                   (OFF_V, None, v_ref))):
            z = _dot(h, w_ref[:, off + g * GROUP_WIDTH:off + (g + 1) * GROUP_WIDTH])
            if gain is not None:
                ms = _dot((z * z).astype(BF16), headmean)
                z = z * lax.rsqrt(ms + EPS) * gain[...]
            if dilation == 1:
                o_ref[g] = z.astype(BF16)
            else:
                buf = zbuf.at[(g - 1) * 3 + slot]
                _put_halves(buf, slice(None), z)
                chunk = tm // dilation
                for r in range(dilation):
                    rows = pl.ds(r, chunk, stride=dilation)
                    o_ref[g, r * chunk:(r + 1) * chunk, :] = _get_halves(buf, rows).astype(BF16)


def _in_proj(x2d, gain, w_in, qgain, kgain, convw, headmean, *, batch, seq):
    t, dm = x2d.shape
    tm = TM_IN
    tiles_per_seq = seq // tm
    assert seq % tm == 0 and all(tm % (16 * d) == 0 for _, d in ATTN_GROUPS)
    row = lambda width: pl.BlockSpec((tm, width), lambda i: (i, 0))
    qkv_spec = pl.BlockSpec((None, N_GROUPS, tm, GROUP_WIDTH),
                            lambda i: (i // tiles_per_seq, 0, i % tiles_per_seq, 0))
    qkv_shape = jax.ShapeDtypeStruct((batch, N_GROUPS, seq, GROUP_WIDTH), BF16)
    n_regrouped = 3 * sum(1 for _, d in ATTN_GROUPS if d > 1)
    return pl.pallas_call(
        functools.partial(_in_proj_kernel, tm=tm, tiles_per_seq=tiles_per_seq),
        grid=(t // tm,),
        in_specs=[row(dm), _resident((1, dm)), _resident(w_in.shape), _resident((1, GROUP_WIDTH)),
                  _resident((1, GROUP_WIDTH)), _resident((CONV_K, CONV_WIDTH)),
                  _resident((GROUP_WIDTH, GROUP_WIDTH))],
        out_specs=[row(POOL_WIDTH), row(CONV_WIDTH), qkv_spec, qkv_spec, qkv_spec],
        out_shape=[jax.ShapeDtypeStruct((t, POOL_WIDTH), BF16), jax.ShapeDtypeStruct((t, CONV_WIDTH), BF16),
                   qkv_shape, qkv_shape, qkv_shape],
        scratch_shapes=[pltpu.VMEM((POOL_HALO, POOL_WIDTH), F32), pltpu.VMEM((CONV_HALO, CONV_WIDTH), F32),
                        pltpu.VMEM((n_regrouped, GROUP_WIDTH // LANES, tm, LANES), F32)],
        compiler_params=pltpu.CompilerParams(dimension_semantics=("arbitrary",),
                                             vmem_limit_bytes=VMEM_LIMIT_BYTES),
        name="in_proj",
    )(x2d, gain, w_in, qgain, kgain, convw, headmean)


def _in_head_masks():
    lane_head = lax.broadcasted_iota(jnp.int32, (1, GROUP_WIDTH), 1) // HEAD_DIM
    return [lane_head == hd for hd in range(HEADS_PER_GROUP)]


def _attn_scores(qb, kb):
    q_heads = jnp.concatenate([jnp.where(m, qb, jnp.zeros_like(qb)) for m in _in_head_masks()], axis=0)
    return lax.dot_general(q_heads, kb, (((1,), (1,)), ((), ())), preferred_element_type=F32)


def _attn_softmax(s, first):
    blk = ATTN_BLOCK
    row = lax.broadcasted_iota(jnp.int32, (blk, blk), 0)
    col = lax.broadcasted_iota(jnp.int32, (blk, blk), 1)
    mask_value = MASK_VALUE * LOG2_E
    probs, row_max, row_den = [], [], []
    for hd in range(HEADS_PER_GROUP):
        sh = s[hd * blk:(hd + 1) * blk, :]
        if first:
            sh = jnp.where(col <= row, sh, mask_value)
        else:
            sh = jnp.concatenate([jnp.where(col >= row, sh[:, :blk], mask_value),
                                  jnp.where(col <= row, sh[:, blk:], mask_value)], axis=1)
        m = jnp.max(sh, axis=-1, keepdims=True)
        p = jnp.exp2(sh - m)
        row_max.append(m)
        row_den.append(jnp.sum(p, axis=-1, keepdims=True))
        probs.append(p.astype(BF16))
    return jnp.concatenate(probs, axis=0), row_max, row_den


def _attn_values(probs, row_max, row_den, vb):
    blk = ATTN_BLOCK
    in_head = _in_head_masks()
    pv = _dot(probs, vb)
    shape = (blk, GROUP_WIDTH)
    acc, mx, den = pv[0:blk, :], jnp.broadcast_to(row_max[0], shape), jnp.broadcast_to(row_den[0], shape)
    for hd in range(1, HEADS_PER_GROUP):
        acc = jnp.where(in_head[hd], pv[hd * blk:(hd + 1) * blk, :], acc)
        mx = jnp.where(in_head[hd], row_max[hd], mx)
        den = jnp.where(in_head[hd], row_den[hd], den)
    return acc, mx, den


def _block_rows(ref, n, r, dilation, tile):
    chunk = tile // dilation
    piece = min(ATTN_BLOCK, chunk)
    parts = []
    for j in range(ATTN_BLOCK // piece):
        pos = n * ATTN_BLOCK + j * piece
        start = (pos // chunk) * tile + r * chunk + pos % chunk
        if not isinstance(start, int):
            start = pl.multiple_of(start, piece)
        parts.append(ref[pl.ds(start, piece), :])
    return parts[0] if len(parts) == 1 else jnp.concatenate(parts, axis=0)


def _attn_group(g, dilation, q_ref, k_ref, v_ref, y_ref, acc_s, max_s, den_s, *, seq, tile):
    blk = ATTN_BLOCK
    n_blocks = seq // dilation // blk
    assert n_blocks % 2 == 0

    def token_rows(n, r):
        start = n * (blk * dilation) + r
        return pl.ds(start, blk) if dilation == 1 else pl.ds(start, blk, stride=dilation)

    def load_state(rows):
        if g == 0:
            return None
        return _get_halves(acc_s, rows), _get_halves(max_s, rows), _get_halves(den_s, rows)

    def store_state(rows, state, new):
        acc, mx, den = new
        if state is not None:
            acc_old, mx_old, den_old = state
            mx_new = jnp.maximum(mx_old, mx)
            w_old, w_new = jnp.exp2(mx_old - mx_new), jnp.exp2(mx - mx_new)
            acc, den, mx = w_old * acc_old + w_new * acc, w_old * den_old + w_new * den, mx_new
        if g == N_GROUPS - 1:
            _put_halves(acc_s, rows, acc / den)
        else:
            _put_halves(acc_s, rows, acc)
            _put_halves(max_s, rows, mx)
            _put_halves(den_s, rows, den)

    def keys_values(ref, n, r, first):
        own = _block_rows(ref, n, r, dilation, tile)
        if first:
            return own
        return jnp.concatenate([_block_rows(ref, n - 1, r, dilation, tile), own], axis=0)

    def block_pair(n, r, first):
        blocks = ((n, first), (n + 1, False))
        rows = [token_rows(nb, r) for nb, _ in blocks]
        states = [load_state(rw) for rw in rows]
        scores = [_attn_scores(_block_rows(q_ref, nb, r, dilation, tile), keys_values(k_ref, nb, r, fb))
                  for nb, fb in blocks]
        soft = [_attn_softmax(s, fb) for s, (_, fb) in zip(scores, blocks)]
        new = [_attn_values(*sm, keys_values(v_ref, nb, r, fb)) for sm, (nb, fb) in zip(soft, blocks)]
        for rw, st, nw in zip(rows, states, new):
            store_state(rw, st, nw)

    def per_residue(r, carry):
        block_pair(0, r, True)

        def per_pair(j, c):
            block_pair(2 * j, r, False)
            return c

        return lax.fori_loop(1, n_blocks // 2, per_pair, carry)

    if dilation == 1:
        per_residue(0, 0)
    else:
        lax.fori_loop(0, dilation, per_residue, 0)

    if g == N_GROUPS - 1:
        def write_out(c, carry):
            rows = pl.ds(pl.multiple_of(c * OUT_CHUNK, OUT_CHUNK), OUT_CHUNK)
            y_ref[rows, :] = _get_halves(acc_s, rows).astype(y_ref.dtype)
            return carry

        lax.fori_loop(0, seq // OUT_CHUNK, write_out, 0)


def _attn_kernel(q_ref, k_ref, v_ref, y_ref, acc_s, max_s, den_s, *, seq, tile):
    g = pl.program_id(1)
    for gi, (_, dilation) in enumerate(ATTN_GROUPS):
        @pl.when(g == gi)
        def _(gi=gi, dilation=dilation):
            _attn_group(gi, dilation, q_ref, k_ref, v_ref, y_ref, acc_s, max_s, den_s, seq=seq, tile=tile)


def _attention(q, k, v, *, tile):
    batch, _, seq, _ = q.shape
    spec = pl.BlockSpec((None, None, seq, GROUP_WIDTH), lambda b, g: (b, g, 0, 0))
    return pl.pallas_call(
        functools.partial(_attn_kernel, seq=seq, tile=tile),
        grid=(batch, N_GROUPS),
        in_specs=[spec, spec, spec],
        out_specs=pl.BlockSpec((None, seq, GROUP_WIDTH), lambda b, g: (b, 0, 0)),
        out_shape=jax.ShapeDtypeStruct((batch, seq, GROUP_WIDTH), BF16),
        scratch_shapes=[_halves_scratch(seq)] * 3,
        compiler_params=pltpu.CompilerParams(dimension_semantics=("arbitrary", "arbitrary"),
                                             vmem_limit_bytes=VMEM_LIMIT_BYTES),
        name="attention",
    )(q, k, v)


def _merge_kernel(x_ref, gain_ref, w_ref, bg_ref, d_ref, wmix_ref, pscale_ref, yc_ref, ya_ref,
                  wp_ref, wc_ref, wa_ref, wo_ref, gain2_ref, xo_ref, h2_ref):
    dm = x_ref.shape[1]
    x = x_ref[...]
    h = _rms_norm(x, gain_ref[...]).astype(BF16)

    y_pool = jnp.concatenate(
        [_dot(d_ref[:, g * POOL_GROUP:(g + 1) * POOL_GROUP], wmix_ref[g]) for g in range(len(POOL_WINDOWS))],
        axis=-1)
    y_pool = (y_pool * pscale_ref[...]).astype(BF16)

    merged = None
    for j, (y, wj_ref) in enumerate(((y_pool, wp_ref), (yc_ref[...], wc_ref), (ya_ref[...], wa_ref))):
        gate_cols = slice(OFF_GATE + j * dm, OFF_GATE + (j + 1) * dm)
        gate = jax.nn.sigmoid(_dot(h, w_ref[:, gate_cols]) + bg_ref[:, j * dm:(j + 1) * dm])
        term = gate * _dot(y, wj_ref[...])
        merged = term if merged is None else merged + term

    x_new = x + _dot(merged.astype(BF16), wo_ref[...])
    xo_ref[...] = x_new
    h2_ref[...] = _rms_norm(x_new, gain2_ref[...]).astype(BF16)


def _merge(x2d, gain, w_in, b_gate, d, w_mix, pool_scale, yc, y_attn, w_pool_up, w_conv_out, w_attn_up, w_o,
           gain2):
    t, dm = x2d.shape
    tm = TM_MERGE
    assert t % tm == 0
    row = lambda width: pl.BlockSpec((tm, width), lambda i: (i, 0))
    return pl.pallas_call(
        _merge_kernel,
        grid=(t // tm,),
        in_specs=[row(dm), _resident((1, dm)), _resident(w_in.shape), _resident(b_gate.shape),
                  row(POOL_WIDTH), _resident(w_mix.shape), _resident((1, POOL_WIDTH)), row(CONV_WIDTH),
                  row(GROUP_WIDTH), _resident(w_pool_up.shape), _resident(w_conv_out.shape),
                  _resident(w_attn_up.shape), _resident(w_o.shape), _resident((1, dm))],
        out_specs=[row(dm), row(dm)],
        out_shape=[jax.ShapeDtypeStruct((t, dm), F32), jax.ShapeDtypeStruct((t, dm), BF16)],
        compiler_params=pltpu.CompilerParams(dimension_semantics=("arbitrary",),
                                             vmem_limit_bytes=VMEM_LIMIT_BYTES),
        name="merge",
    )(x2d, gain, w_in, b_gate, d, w_mix, pool_scale, yc, y_attn, w_pool_up, w_conv_out, w_attn_up, w_o, gain2)


def _ffn_kernel(x_ref, h2_ref, w1_ref, w2_ref, o_ref):
    d_ff = w1_ref.shape[1]
    h2 = h2_ref[...]
    acc = None
    for c in range(d_ff // FF_CHUNK):
        cols = slice(c * FF_CHUNK, (c + 1) * FF_CHUNK)
        a = jnp.square(jnp.maximum(_dot(h2, w1_ref[:, cols]), 0.0)).astype(BF16)
        part = _dot(a, w2_ref[cols, :])
        acc = part if acc is None else acc + part
    o_ref[...] = x_ref[...] + acc


def _ffn(x2d, h2, w1, w2):
    t, dm = x2d.shape
    tm = TM_FFN
    assert t % tm == 0 and w1.shape[1] % FF_CHUNK == 0
    row = pl.BlockSpec((tm, dm), lambda i: (i, 0))
    return pl.pallas_call(
        _ffn_kernel,
        grid=(t // tm,),
        in_specs=[row, row, _resident(w1.shape), _resident(w2.shape)],
        out_specs=row,
        out_shape=jax.ShapeDtypeStruct((t, dm), F32),
        compiler_params=pltpu.CompilerParams(dimension_semantics=("arbitrary",),
                                             vmem_limit_bytes=VMEM_LIMIT_BYTES),
        name="ffn",
    )(x2d, h2, w1, w2)


def kernel(x, norm_mix, w_in, b_gate, pool_mix, pool_scale, conv_w, q_gain, k_gain, w_pool_up, w_conv_out,
           w_attn_up, w_o, norm_mlp, w_ff1, w_ff2):
    batch, seq, dm = x.shape
    depth = norm_mix.shape[0]
    assert w_in.shape[2] == OFF_GATE + N_BRANCH * dm
    head_id = jnp.arange(GROUP_WIDTH) // HEAD_DIM
    headmean = jnp.where(head_id[:, None] == head_id[None, :], 1.0 / HEAD_DIM, 0.0).astype(BF16)
    x2d = x.reshape(batch * seq, dm)
    for l in range(depth):
        w_in_l = w_in[l].astype(BF16)
        gain = norm_mix[l].reshape(1, dm)
        qgain = (jnp.tile(q_gain[l], HEADS_PER_GROUP) * (LOG2_E * HEAD_DIM ** -0.5)).reshape(1, GROUP_WIDTH)
        kgain = jnp.tile(k_gain[l], HEADS_PER_GROUP).reshape(1, GROUP_WIDTH)
        d, yc, q, k, v = _in_proj(x2d, gain, w_in_l, qgain, kgain, conv_w[l], headmean, batch=batch, seq=seq)
        y_attn = _attention(q, k, v, tile=TM_IN).reshape(batch * seq, GROUP_WIDTH)
        x2d, h2 = _merge(x2d, gain, w_in_l, b_gate[l].reshape(1, N_BRANCH * dm), d, pool_mix[l].astype(BF16),
                         pool_scale[l].reshape(1, POOL_WIDTH), yc, y_attn, w_pool_up[l].astype(BF16),
                         w_conv_out[l].astype(BF16), w_attn_up[l].astype(BF16), w_o[l].astype(BF16),
                         norm_mlp[l].reshape(1, dm))
        x2d = _ffn(x2d, h2, w_ff1[l].astype(BF16), w_ff2[l].astype(BF16))
    return x2d.reshape(batch, seq, dm)
```

```python
import functools

import jax
import jax.numpy as jnp
from jax import lax
from jax.experimental import pallas as pl
from jax.experimental.pallas import tpu as pltpu

F32 = jnp.float32
BF16 = jnp.bfloat16

POOL_WINDOWS = (2, 4, 8, 16)
POOL_GROUP = 128
POOL_WIDTH = POOL_GROUP * len(POOL_WINDOWS)
CONV_WIDTH = 512
CONV_K = 3
HEAD_DIM = 64
ATTN_GROUPS = ((128, 1), (512, 4), (2048, 16))
HEADS_PER_GROUP = 4
GROUP_WIDTH = HEADS_PER_GROUP * HEAD_DIM
N_GROUPS = len(ATTN_GROUPS)
ATTN_WIDTH = N_GROUPS * GROUP_WIDTH
ATTN_BLOCK = 128
LANES = 128
N_BRANCH = 3
EPS = 1e-6
MASK_VALUE = -1e30
LOG2_E = 1.4426950408889634

OFF_POOL = 0
OFF_CONV = OFF_POOL + POOL_WIDTH
OFF_Q = OFF_CONV + 3 * CONV_WIDTH
OFF_K = OFF_Q + ATTN_WIDTH
OFF_V = OFF_K + ATTN_WIDTH
OFF_GATE = OFF_V + ATTN_WIDTH

POOL_HALO = 16
CONV_HALO = 8

VMEM_LIMIT_BYTES = 56 * 1024 * 1024
TM_IN = 512
TM_MERGE = 512
TM_FFN = 512
FF_CHUNK = 1024
OUT_CHUNK = 512


def _resident(shape):
    return pl.BlockSpec(shape, lambda *_: (0,) * len(shape), pipeline_mode=pl.Buffered(1))


def _rms_norm(x, gain):
    ms = jnp.mean(x * x, axis=-1, keepdims=True)
    return x * lax.rsqrt(ms + EPS) * gain


def _dot(a, b):
    return jnp.dot(a, b, preferred_element_type=F32)


def _halves_scratch(rows):
    return pltpu.VMEM((GROUP_WIDTH // LANES, rows, LANES), F32)


def _put_halves(ref, rows, val):
    for j in range(GROUP_WIDTH // LANES):
        ref[j, rows, :] = val[:, j * LANES:(j + 1) * LANES]


def _get_halves(ref, rows):
    return jnp.concatenate([ref[j, rows, :] for j in range(GROUP_WIDTH // LANES)], axis=-1)


def _in_proj_kernel(x_ref, gain_ref, w_ref, qgain_ref, kgain_ref, convw_ref, headmean_ref,
                    d_ref, yc_ref, q_ref, k_ref, v_ref, uhalo, chalo, zbuf, *, tm, tiles_per_seq):
    i = pl.program_id(0)

    @pl.when(i == 0)
    def _():
        uhalo[...] = jnp.zeros_like(uhalo)
        chalo[...] = jnp.zeros_like(chalo)

    tile_in_seq = i % tiles_per_seq
    carry = tile_in_seq != 0
    h = _rms_norm(x_ref[...], gain_ref[...]).astype(BF16)

    zpc = _dot(h, w_ref[:, OFF_POOL:OFF_Q])

    u = zpc[:, OFF_POOL:OFF_POOL + POOL_WIDTH]
    ext = jnp.concatenate([jnp.where(carry, uhalo[...], 0.0), u], axis=0)
    uhalo[...] = u[tm - POOL_HALO:tm, :]
    pos = tile_in_seq * tm + lax.broadcasted_iota(jnp.int32, (tm, 1), 0)
    run = ext
    for g, w in enumerate(POOL_WINDOWS):
        run = run + pltpu.roll(run, w // 2, 0)
        cols = slice(g * POOL_GROUP, (g + 1) * POOL_GROUP)
        inv_count = 1.0 / jnp.minimum(pos + 1, w).astype(F32)
        d_ref[:, cols] = (run[POOL_HALO:, 0:POOL_GROUP] * inv_count - u[:, cols]).astype(BF16)
        run = run[:, POOL_GROUP:]

    zc = zpc[:, OFF_CONV:OFF_CONV + 3 * CONV_WIDTH]
    uc = zc[:, CONV_WIDTH:2 * CONV_WIDTH] * zc[:, 2 * CONV_WIDTH:3 * CONV_WIDTH]
    ext = jnp.concatenate([jnp.where(carry, chalo[...], 0.0), uc], axis=0)
    chalo[...] = uc[tm - CONV_HALO:tm, :]
    y = convw_ref[CONV_K - 1:CONV_K, :] * uc
    for j in range(CONV_K - 1):
        y = y + convw_ref[j:j + 1, :] * pltpu.roll(ext, CONV_K - 1 - j, 0)[CONV_HALO:, :]
    yc_ref[...] = (zc[:, 0:CONV_WIDTH] * y).astype(BF16)

    zqkv = _dot(h, w_ref[:, OFF_Q:OFF_Q + 3 * ATTN_WIDTH])
    for slot, (gain, o_ref) in enumerate(((qgain_ref, q_ref), (kgain_ref, k_ref), (None, v_ref))):
        zs = [zqkv[:, slot * ATTN_WIDTH + g * GROUP_WIDTH:slot * ATTN_WIDTH + (g + 1) * GROUP_WIDTH]
              for g in range(N_GROUPS)]
        if gain is not None:
            ms = _dot(jnp.concatenate([(z * z).astype(BF16) for z in zs], axis=0), headmean_ref[...])
            zs = [z * lax.rsqrt(ms[g * tm:(g + 1) * tm, :] + EPS) * gain[...] for g, z in enumerate(zs)]
        for g, (_, dilation) in enumerate(ATTN_GROUPS):
            z = zs[g]
            if dilation == 1:
                o_ref[g] = z.astype(BF16)
            else:
                buf = zbuf.at[(g - 1) * 3 + slot]
                _put_halves(buf, slice(None), z)
                chunk = tm // dilation
                for r in range(dilation):
                    rows = pl.ds(r, chunk, stride=dilation)
                    o_ref[g, r * chunk:(r + 1) * chunk, :] = _get_halves(buf, rows).astype(BF16)


def _in_proj(x2d, gain, w_in, qgain, kgain, convw, headmean, *, batch, seq):
    t, dm = x2d.shape
    tm = TM_IN
    tiles_per_seq = seq // tm
    assert seq % tm == 0 and all(tm % (16 * d) == 0 for _, d in ATTN_GROUPS)
    row = lambda width: pl.BlockSpec((tm, width), lambda i: (i, 0))
    qkv_spec = pl.BlockSpec((None, N_GROUPS, tm, GROUP_WIDTH),
                            lambda i: (i // tiles_per_seq, 0, i % tiles_per_seq, 0))
    qkv_shape = jax.ShapeDtypeStruct((batch, N_GROUPS, seq, GROUP_WIDTH), BF16)
    n_regrouped = 3 * sum(1 for _, d in ATTN_GROUPS if d > 1)
    return pl.pallas_call(
        functools.partial(_in_proj_kernel, tm=tm, tiles_per_seq=tiles_per_seq),
        grid=(t // tm,),
        in_specs=[row(dm), _resident((1, dm)), _resident(w_in.shape), _resident((1, GROUP_WIDTH)),
                  _resident((1, GROUP_WIDTH)), _resident((CONV_K, CONV_WIDTH)),
                  _resident((GROUP_WIDTH, GROUP_WIDTH))],
        out_specs=[row(POOL_WIDTH), row(CONV_WIDTH), qkv_spec, qkv_spec, qkv_spec],
        out_shape=[jax.ShapeDtypeStruct((t, POOL_WIDTH), BF16), jax.ShapeDtypeStruct((t, CONV_WIDTH), BF16),
                   qkv_shape, qkv_shape, qkv_shape],
        scratch_shapes=[pltpu.VMEM((POOL_HALO, POOL_WIDTH), F32), pltpu.VMEM((CONV_HALO, CONV_WIDTH), F32),
                        pltpu.VMEM((n_regrouped, GROUP_WIDTH // LANES, tm, LANES), F32)],
        compiler_params=pltpu.CompilerParams(dimension_semantics=("arbitrary",),
                                             vmem_limit_bytes=VMEM_LIMIT_BYTES),
        name="in_proj",
    )(x2d, gain, w_in, qgain, kgain, convw, headmean)


def _in_head_masks():
    lane_head = lax.broadcasted_iota(jnp.int32, (1, GROUP_WIDTH), 1) // HEAD_DIM
    return [lane_head == hd for hd in range(HEADS_PER_GROUP)]


def _attn_scores(qb, kb):
    q_heads = jnp.concatenate([jnp.where(m, qb, jnp.zeros_like(qb)) for m in _in_head_masks()], axis=0)
    return lax.dot_general(q_heads, kb, (((1,), (1,)), ((), ())), preferred_element_type=F32)


def _attn_softmax(s, first):
    blk = ATTN_BLOCK
    row = lax.broadcasted_iota(jnp.int32, (blk, blk), 0)
    col = lax.broadcasted_iota(jnp.int32, (blk, blk), 1)
    mask_value = MASK_VALUE * LOG2_E
    probs, row_max, row_den = [], [], []
    for hd in range(HEADS_PER_GROUP):
        sh = s[hd * blk:(hd + 1) * blk, :]
        if first:
            sh = jnp.where(col <= row, sh, mask_value)
        else:
            sh = jnp.concatenate([jnp.where(col >= row, sh[:, :blk], mask_value),
                                  jnp.where(col <= row, sh[:, blk:], mask_value)], axis=1)
        m = jnp.max(sh, axis=-1, keepdims=True)
        p = jnp.exp2(sh - m)
        row_max.append(m)
        row_den.append(jnp.sum(p, axis=-1, keepdims=True))
        probs.append(p.astype(BF16))
    return jnp.concatenate(probs, axis=0), row_max, row_den


def _attn_values(probs, row_max, row_den, vb):
    blk = ATTN_BLOCK
    in_head = _in_head_masks()
    pv = _dot(probs, vb)
    shape = (blk, GROUP_WIDTH)
    acc, mx, den = pv[0:blk, :], jnp.broadcast_to(row_max[0], shape), jnp.broadcast_to(row_den[0], shape)
    for hd in range(1, HEADS_PER_GROUP):
        acc = jnp.where(in_head[hd], pv[hd * blk:(hd + 1) * blk, :], acc)
        mx = jnp.where(in_head[hd], row_max[hd], mx)
        den = jnp.where(in_head[hd], row_den[hd], den)
    return acc, mx, den


def _block_rows(ref, n, r, dilation, tile):
    chunk = tile // dilation
    piece = min(ATTN_BLOCK, chunk)
    parts = []
    for j in range(ATTN_BLOCK // piece):
        pos = n * ATTN_BLOCK + j * piece
        start = (pos // chunk) * tile + r * chunk + pos % chunk
        if not isinstance(start, int):
            start = pl.multiple_of(start, piece)
        parts.append(ref[pl.ds(start, piece), :])
    return parts[0] if len(parts) == 1 else jnp.concatenate(parts, axis=0)


def _attn_group(g, dilation, q_ref, k_ref, v_ref, y_ref, acc_s, max_s, den_s, *, seq, tile):
    blk = ATTN_BLOCK
    n_blocks = seq // dilation // blk
    assert n_blocks % 2 == 0

    def token_rows(n, r):
        start = n * (blk * dilation) + r
        return pl.ds(start, blk) if dilation == 1 else pl.ds(start, blk, stride=dilation)

    def load_state(rows):
        if g == 0:
            return None
        return _get_halves(acc_s, rows), _get_halves(max_s, rows), _get_halves(den_s, rows)

    def store_state(rows, state, new):
        acc, mx, den = new
        if state is not None:
            acc_old, mx_old, den_old = state
            mx_new = jnp.maximum(mx_old, mx)
            w_old, w_new = jnp.exp2(mx_old - mx_new), jnp.exp2(mx - mx_new)
            acc, den, mx = w_old * acc_old + w_new * acc, w_old * den_old + w_new * den, mx_new
        if g == N_GROUPS - 1:
            _put_halves(acc_s, rows, acc / den)
        else:
            _put_halves(acc_s, rows, acc)
            _put_halves(max_s, rows, mx)
            _put_halves(den_s, rows, den)

    def keys_values(ref, n, r, first):
        own = _block_rows(ref, n, r, dilation, tile)
        if first:
            return own
        return jnp.concatenate([_block_rows(ref, n - 1, r, dilation, tile), own], axis=0)

    def block_pair(n, r, first):
        blocks = ((n, first), (n + 1, False))
        rows = [token_rows(nb, r) for nb, _ in blocks]
        states = [load_state(rw) for rw in rows]
        scores = [_attn_scores(_block_rows(q_ref, nb, r, dilation, tile), keys_values(k_ref, nb, r, fb))
                  for nb, fb in blocks]
        soft = [_attn_softmax(s, fb) for s, (_, fb) in zip(scores, blocks)]
        new = [_attn_values(*sm, keys_values(v_ref, nb, r, fb)) for sm, (nb, fb) in zip(soft, blocks)]
        for rw, st, nw in zip(rows, states, new):
            store_state(rw, st, nw)

    def per_residue(r, carry):
        block_pair(0, r, True)

        def per_pair(j, c):
            block_pair(2 * j, r, False)
            return c

        return lax.fori_loop(1, n_blocks // 2, per_pair, carry)

    if dilation == 1:
        per_residue(0, 0)
    else:
        lax.fori_loop(0, dilation, per_residue, 0)

    if g == N_GROUPS - 1:
        def write_out(c, carry):
            rows = pl.ds(pl.multiple_of(c * OUT_CHUNK, OUT_CHUNK), OUT_CHUNK)
            y_ref[rows, :] = _get_halves(acc_s, rows).astype(y_ref.dtype)
            return carry

        lax.fori_loop(0, seq // OUT_CHUNK, write_out, 0)


def _attn_kernel(q_ref, k_ref, v_ref, y_ref, acc_s, max_s, den_s, *, seq, tile):
    g = pl.program_id(1)
    for gi, (_, dilation) in enumerate(ATTN_GROUPS):
        @pl.when(g == gi)
        def _(gi=gi, dilation=dilation):
            _attn_group(gi, dilation, q_ref, k_ref, v_ref, y_ref, acc_s, max_s, den_s, seq=seq, tile=tile)


def _attention(q, k, v, *, tile):
    batch, _, seq, _ = q.shape
    spec = pl.BlockSpec((None, None, seq, GROUP_WIDTH), lambda b, g: (b, g, 0, 0))
    return pl.pallas_call(
        functools.partial(_attn_kernel, seq=seq, tile=tile),
        grid=(batch, N_GROUPS),
        in_specs=[spec, spec, spec],
        out_specs=pl.BlockSpec((None, seq, GROUP_WIDTH), lambda b, g: (b, 0, 0)),
        out_shape=jax.ShapeDtypeStruct((batch, seq, GROUP_WIDTH), BF16),
        scratch_shapes=[_halves_scratch(seq)] * 3,
        compiler_params=pltpu.CompilerParams(dimension_semantics=("arbitrary", "arbitrary"),
                                             vmem_limit_bytes=VMEM_LIMIT_BYTES),
        name="attention",
    )(q, k, v)


def _merge_kernel(x_ref, gain_ref, w_ref, bg_ref, d_ref, wmix_ref, pscale_ref, yc_ref, ya_ref,
                  wp_ref, wc_ref, wa_ref, wo_ref, gain2_ref, xo_ref, h2_ref):
    dm = x_ref.shape[1]
    x = x_ref[...]
    h = _rms_norm(x, gain_ref[...]).astype(BF16)

    y_pool = jnp.concatenate(
        [_dot(d_ref[:, g * POOL_GROUP:(g + 1) * POOL_GROUP], wmix_ref[g]) for g in range(len(POOL_WINDOWS))],
        axis=-1)
    y_pool = (y_pool * pscale_ref[...]).astype(BF16)

    merged = None
    for j, (y, wj_ref) in enumerate(((y_pool, wp_ref), (yc_ref[...], wc_ref), (ya_ref[...], wa_ref))):
        gate_cols = slice(j * dm, (j + 1) * dm)
        gate = jax.nn.sigmoid(_dot(h, w_ref[:, gate_cols]) + bg_ref[:, gate_cols])
        term = gate * _dot(y, wj_ref[...])
        merged = term if merged is None else merged + term

    x_new = x + _dot(merged.astype(BF16), wo_ref[...])
    xo_ref[...] = x_new
    h2_ref[...] = _rms_norm(x_new, gain2_ref[...]).astype(BF16)


def _merge(x2d, gain, w_in, b_gate, d, w_mix, pool_scale, yc, y_attn, w_pool_up, w_conv_out, w_attn_up, w_o,
           gain2):
    t, dm = x2d.shape
    tm = TM_MERGE
    assert t % tm == 0
    row = lambda width: pl.BlockSpec((tm, width), lambda i: (i, 0))
    return pl.pallas_call(
        _merge_kernel,
        grid=(t // tm,),
        in_specs=[row(dm), _resident((1, dm)), _resident(w_in.shape), _resident(b_gate.shape),
                  row(POOL_WIDTH), _resident(w_mix.shape), _resident((1, POOL_WIDTH)), row(CONV_WIDTH),
                  row(GROUP_WIDTH), _resident(w_pool_up.shape), _resident(w_conv_out.shape),
                  _resident(w_attn_up.shape), _resident(w_o.shape), _resident((1, dm))],
        out_specs=[row(dm), row(dm)],
        out_shape=[jax.ShapeDtypeStruct((t, dm), F32), jax.ShapeDtypeStruct((t, dm), BF16)],
        compiler_params=pltpu.CompilerParams(dimension_semantics=("arbitrary",),
                                             vmem_limit_bytes=VMEM_LIMIT_BYTES),
        name="merge",
    )(x2d, gain, w_in, b_gate, d, w_mix, pool_scale, yc, y_attn, w_pool_up, w_conv_out, w_attn_up, w_o, gain2)


def _ffn_kernel(x_ref, h2_ref, w1_ref, w2_ref, o_ref):
    d_ff = w1_ref.shape[1]
    h2 = h2_ref[...]
    acc = None
    for c in range(d_ff // FF_CHUNK):
        cols = slice(c * FF_CHUNK, (c + 1) * FF_CHUNK)
        a = jnp.square(jnp.maximum(_dot(h2, w1_ref[:, cols]), 0.0)).astype(BF16)
        part = _dot(a, w2_ref[cols, :])
        acc = part if acc is None else acc + part
    o_ref[...] = x_ref[...] + acc


def _ffn(x2d, h2, w1, w2):
    t, dm = x2d.shape
    tm = TM_FFN
    assert t % tm == 0 and w1.shape[1] % FF_CHUNK == 0
    row = pl.BlockSpec((tm, dm), lambda i: (i, 0))
    return pl.pallas_call(
        _ffn_kernel,
        grid=(t // tm,),
        in_specs=[row, row, _resident(w1.shape), _resident(w2.shape)],
        out_specs=row,
        out_shape=jax.ShapeDtypeStruct((t, dm), F32),
        compiler_params=pltpu.CompilerParams(dimension_semantics=("arbitrary",),
                                             vmem_limit_bytes=VMEM_LIMIT_BYTES),
        name="ffn",
    )(x2d, h2, w1, w2)


def kernel(x, norm_mix, w_in, b_gate, pool_mix, pool_scale, conv_w, q_gain, k_gain, w_pool_up, w_conv_out,
           w_attn_up, w_o, norm_mlp, w_ff1, w_ff2):
    batch, seq, dm = x.shape
    depth = norm_mix.shape[0]
    assert w_in.shape[2] == OFF_GATE + N_BRANCH * dm
    head_id = jnp.arange(GROUP_WIDTH) // HEAD_DIM
    headmean = jnp.where(head_id[:, None] == head_id[None, :], 1.0 / HEAD_DIM, 0.0).astype(BF16)
    x2d = x.reshape(batch * seq, dm)
    for l in range(depth):
        w_mixers = w_in[l, :, :OFF_GATE].astype(BF16)
        w_gates = w_in[l, :, OFF_GATE:].astype(BF16)
        gain = norm_mix[l].reshape(1, dm)
        qgain = (jnp.tile(q_gain[l], HEADS_PER_GROUP) * (LOG2_E * HEAD_DIM ** -0.5)).reshape(1, GROUP_WIDTH)
        kgain = jnp.tile(k_gain[l], HEADS_PER_GROUP).reshape(1, GROUP_WIDTH)
        d, yc, q, k, v = _in_proj(x2d, gain, w_mixers, qgain, kgain, conv_w[l], headmean, batch=batch, seq=seq)
        y_attn = _attention(q, k, v, tile=TM_IN).reshape(batch * seq, GROUP_WIDTH)
        x2d, h2 = _merge(x2d, gain, w_gates, b_gate[l].reshape(1, N_BRANCH * dm), d, pool_mix[l].astype(BF16),
                         pool_scale[l].reshape(1, POOL_WIDTH), yc, y_attn, w_pool_up[l].astype(BF16),
                         w_conv_out[l].astype(BF16), w_attn_up[l].astype(BF16), w_o[l].astype(BF16),
                         norm_mlp[l].reshape(1, dm))
        x2d = _ffn(x2d, h2, w_ff1[l].astype(BF16), w_ff2[l].astype(BF16))
    return x2d.reshape(batch, seq, dm)
```

```python
import functools

import jax
import jax.numpy as jnp
from jax import lax
from jax.experimental import pallas as pl
from jax.experimental.pallas import tpu as pltpu

F32 = jnp.float32
BF16 = jnp.bfloat16

POOL_WINDOWS = (2, 4, 8, 16)
POOL_GROUP = 128
POOL_WIDTH = POOL_GROUP * len(POOL_WINDOWS)
CONV_WIDTH = 512
CONV_K = 3
HEAD_DIM = 64
ATTN_GROUPS = ((128, 1), (512, 4), (2048, 16))
HEADS_PER_GROUP = 4
GROUP_WIDTH = HEADS_PER_GROUP * HEAD_DIM
N_GROUPS = len(ATTN_GROUPS)
ATTN_WIDTH = N_GROUPS * GROUP_WIDTH
ATTN_BLOCK = 128
LANES = 128
N_BRANCH = 3
EPS = 1e-6
MASK_VALUE = -1e30
LOG2_E = 1.4426950408889634

OFF_POOL = 0
OFF_CONV = OFF_POOL + POOL_WIDTH
OFF_Q = OFF_CONV + 3 * CONV_WIDTH
OFF_K = OFF_Q + ATTN_WIDTH
OFF_V = OFF_K + ATTN_WIDTH
OFF_GATE = OFF_V + ATTN_WIDTH

POOL_HALO = 16
CONV_HALO = 8

VMEM_LIMIT_BYTES = 56 * 1024 * 1024
TM_IN = 512
TM_MERGE = 512
TM_FFN = 512
FF_CHUNK = 1024
OUT_CHUNK = 512


def _resident(shape):
    return pl.BlockSpec(shape, lambda *_: (0,) * len(shape), pipeline_mode=pl.Buffered(1))


def _layer_resident(layer, shape):
    return pl.BlockSpec((None,) + tuple(shape), lambda *_: (layer,) + (0,) * len(shape),
                        pipeline_mode=pl.Buffered(1))


def _rms_norm(x, gain):
    ms = jnp.mean(x * x, axis=-1, keepdims=True)
    return x * lax.rsqrt(ms + EPS) * gain


def _dot(a, b):
    return jnp.dot(a, b, preferred_element_type=F32)


def _halves_scratch(rows):
    return pltpu.VMEM((GROUP_WIDTH // LANES, rows, LANES), F32)


def _put_halves(ref, rows, val):
    for j in range(GROUP_WIDTH // LANES):
        ref[j, rows, :] = val[:, j * LANES:(j + 1) * LANES]


def _get_halves(ref, rows):
    return jnp.concatenate([ref[j, rows, :] for j in range(GROUP_WIDTH // LANES)], axis=-1)


def _in_proj_kernel(x_ref, gain_ref, w_ref, qgain_ref, kgain_ref, convw_ref, headmean_ref,
                    d_ref, yc_ref, q_ref, k_ref, v_ref, uhalo, chalo, zbuf, *, tm, tiles_per_seq):
    i = pl.program_id(0)

    @pl.when(i == 0)
    def _():
        uhalo[...] = jnp.zeros_like(uhalo)
        chalo[...] = jnp.zeros_like(chalo)

    tile_in_seq = i % tiles_per_seq
    carry = tile_in_seq != 0
    h = _rms_norm(x_ref[...], gain_ref[...]).astype(BF16)

    zpc = _dot(h, w_ref[:, OFF_POOL:OFF_Q])

    u = zpc[:, OFF_POOL:OFF_POOL + POOL_WIDTH]
    ext = jnp.concatenate([jnp.where(carry, uhalo[...], 0.0), u], axis=0)
    uhalo[...] = u[tm - POOL_HALO:tm, :]
    pos = tile_in_seq * tm + lax.broadcasted_iota(jnp.int32, (tm, 1), 0)
    run = ext
    for g, w in enumerate(POOL_WINDOWS):
        run = run + pltpu.roll(run, w // 2, 0)
        cols = slice(g * POOL_GROUP, (g + 1) * POOL_GROUP)
        inv_count = 1.0 / jnp.minimum(pos + 1, w).astype(F32)
        d_ref[:, cols] = (run[POOL_HALO:, 0:POOL_GROUP] * inv_count - u[:, cols]).astype(BF16)
        run = run[:, POOL_GROUP:]

    zc = zpc[:, OFF_CONV:OFF_CONV + 3 * CONV_WIDTH]
    uc = zc[:, CONV_WIDTH:2 * CONV_WIDTH] * zc[:, 2 * CONV_WIDTH:3 * CONV_WIDTH]
    ext = jnp.concatenate([jnp.where(carry, chalo[...], 0.0), uc], axis=0)
    chalo[...] = uc[tm - CONV_HALO:tm, :]
    y = convw_ref[CONV_K - 1:CONV_K, :] * uc
    for j in range(CONV_K - 1):
        y = y + convw_ref[j:j + 1, :] * pltpu.roll(ext, CONV_K - 1 - j, 0)[CONV_HALO:, :]
    yc_ref[...] = (zc[:, 0:CONV_WIDTH] * y).astype(BF16)

    zqkv = _dot(h, w_ref[:, OFF_Q:OFF_Q + 3 * ATTN_WIDTH])
    for slot, (gain, o_ref) in enumerate(((qgain_ref, q_ref), (kgain_ref, k_ref), (None, v_ref))):
        zs = [zqkv[:, slot * ATTN_WIDTH + g * GROUP_WIDTH:slot * ATTN_WIDTH + (g + 1) * GROUP_WIDTH]
              for g in range(N_GROUPS)]
        if gain is not None:
            ms = _dot(jnp.concatenate([(z * z).astype(BF16) for z in zs], axis=0), headmean_ref[...])
            zs = [z * lax.rsqrt(ms[g * tm:(g + 1) * tm, :] + EPS) * gain[...] for g, z in enumerate(zs)]
        for g, (_, dilation) in enumerate(ATTN_GROUPS):
            z = zs[g]
            if dilation == 1:
                o_ref[g] = z.astype(BF16)
            else:
                buf = zbuf.at[(g - 1) * 3 + slot]
                _put_halves(buf, slice(None), z)
                chunk = tm // dilation
                for r in range(dilation):
                    rows = pl.ds(r, chunk, stride=dilation)
                    o_ref[g, r * chunk:(r + 1) * chunk, :] = _get_halves(buf, rows).astype(BF16)


def _in_proj(x2d, gain, w_in, qgain, kgain, convw, headmean, *, layer, batch, seq):
    t, dm = x2d.shape
    tm = TM_IN
    tiles_per_seq = seq // tm
    assert seq % tm == 0 and all(tm % (16 * d) == 0 for _, d in ATTN_GROUPS)
    row = lambda width: pl.BlockSpec((tm, width), lambda i: (i, 0))
    qkv_spec = pl.BlockSpec((None, N_GROUPS, tm, GROUP_WIDTH),
                            lambda i: (i // tiles_per_seq, 0, i % tiles_per_seq, 0))
    qkv_shape = jax.ShapeDtypeStruct((batch, N_GROUPS, seq, GROUP_WIDTH), BF16)
    n_regrouped = 3 * sum(1 for _, d in ATTN_GROUPS if d > 1)
    return pl.pallas_call(
        functools.partial(_in_proj_kernel, tm=tm, tiles_per_seq=tiles_per_seq),
        grid=(t // tm,),
        in_specs=[row(dm), _resident((1, dm)), _layer_resident(layer, (dm, OFF_GATE)),
                  _resident((1, GROUP_WIDTH)), _resident((1, GROUP_WIDTH)), _resident((CONV_K, CONV_WIDTH)),
                  _resident((GROUP_WIDTH, GROUP_WIDTH))],
        out_specs=[row(POOL_WIDTH), row(CONV_WIDTH), qkv_spec, qkv_spec, qkv_spec],
        out_shape=[jax.ShapeDtypeStruct((t, POOL_WIDTH), BF16), jax.ShapeDtypeStruct((t, CONV_WIDTH), BF16),
                   qkv_shape, qkv_shape, qkv_shape],
        scratch_shapes=[pltpu.VMEM((POOL_HALO, POOL_WIDTH), F32), pltpu.VMEM((CONV_HALO, CONV_WIDTH), F32),
                        pltpu.VMEM((n_regrouped, GROUP_WIDTH // LANES, tm, LANES), F32)],
        compiler_params=pltpu.CompilerParams(dimension_semantics=("arbitrary",),
                                             vmem_limit_bytes=VMEM_LIMIT_BYTES),
        name="in_proj",
    )(x2d, gain, w_in, qgain, kgain, convw, headmean)


def _in_head_masks():
    lane_head = lax.broadcasted_iota(jnp.int32, (1, GROUP_WIDTH), 1) // HEAD_DIM
    return [lane_head == hd for hd in range(HEADS_PER_GROUP)]


def _attn_scores(qb, kb):
    q_heads = jnp.concatenate([jnp.where(m, qb, jnp.zeros_like(qb)) for m in _in_head_masks()], axis=0)
    return lax.dot_general(q_heads, kb, (((1,), (1,)), ((), ())), preferred_element_type=F32)


def _attn_softmax(s, first):
    blk = ATTN_BLOCK
    row = lax.broadcasted_iota(jnp.int32, (blk, blk), 0)
    col = lax.broadcasted_iota(jnp.int32, (blk, blk), 1)
    mask_value = MASK_VALUE * LOG2_E
    probs, row_max, row_den = [], [], []
    for hd in range(HEADS_PER_GROUP):
        sh = s[hd * blk:(hd + 1) * blk, :]
        if first:
            sh = jnp.where(col <= row, sh, mask_value)
        else:
            sh = jnp.concatenate([jnp.where(col >= row, sh[:, :blk], mask_value),
                                  jnp.where(col <= row, sh[:, blk:], mask_value)], axis=1)
        m = jnp.max(sh, axis=-1, keepdims=True)
        p = jnp.exp2(sh - m)
        row_max.append(m)
        row_den.append(jnp.sum(p, axis=-1, keepdims=True))
        probs.append(p.astype(BF16))
    return jnp.concatenate(probs, axis=0), row_max, row_den


def _attn_values(probs, row_max, row_den, vb):
    blk = ATTN_BLOCK
    in_head = _in_head_masks()
    pv = _dot(probs, vb)
    shape = (blk, GROUP_WIDTH)
    acc, mx, den = pv[0:blk, :], jnp.broadcast_to(row_max[0], shape), jnp.broadcast_to(row_den[0], shape)
    for hd in range(1, HEADS_PER_GROUP):
        acc = jnp.where(in_head[hd], pv[hd * blk:(hd + 1) * blk, :], acc)
        mx = jnp.where(in_head[hd], row_max[hd], mx)
        den = jnp.where(in_head[hd], row_den[hd], den)
    return acc, mx, den


def _block_rows(ref, n, r, dilation, tile):
    chunk = tile // dilation
    piece = min(ATTN_BLOCK, chunk)
    parts = []
    for j in range(ATTN_BLOCK // piece):
        pos = n * ATTN_BLOCK + j * piece
        start = (pos // chunk) * tile + r * chunk + pos % chunk
        if not isinstance(start, int):
            start = pl.multiple_of(start, piece)
        parts.append(ref[pl.ds(start, piece), :])
    return parts[0] if len(parts) == 1 else jnp.concatenate(parts, axis=0)


def _attn_group(g, dilation, q_ref, k_ref, v_ref, y_ref, acc_s, max_s, den_s, *, seq, tile):
    blk = ATTN_BLOCK
    n_blocks = seq // dilation // blk
    assert n_blocks % 2 == 0

    def token_rows(n, r):
        start = n * (blk * dilation) + r
        return pl.ds(start, blk) if dilation == 1 else pl.ds(start, blk, stride=dilation)

    def load_state(rows):
        if g == 0:
            return None
        return _get_halves(acc_s, rows), _get_halves(max_s, rows), _get_halves(den_s, rows)

    def store_state(rows, state, new):
        acc, mx, den = new
        if state is not None:
            acc_old, mx_old, den_old = state
            mx_new = jnp.maximum(mx_old, mx)
            w_old, w_new = jnp.exp2(mx_old - mx_new), jnp.exp2(mx - mx_new)
            acc, den, mx = w_old * acc_old + w_new * acc, w_old * den_old + w_new * den, mx_new
        if g == N_GROUPS - 1:
            _put_halves(acc_s, rows, acc / den)
        else:
            _put_halves(acc_s, rows, acc)
            _put_halves(max_s, rows, mx)
            _put_halves(den_s, rows, den)

    def keys_values(ref, n, r, first):
        own = _block_rows(ref, n, r, dilation, tile)
        if first:
            return own
        return jnp.concatenate([_block_rows(ref, n - 1, r, dilation, tile), own], axis=0)

    def block_pair(n, r, first):
        blocks = ((n, first), (n + 1, False))
        rows = [token_rows(nb, r) for nb, _ in blocks]
        states = [load_state(rw) for rw in rows]
        scores = [_attn_scores(_block_rows(q_ref, nb, r, dilation, tile), keys_values(k_ref, nb, r, fb))
                  for nb, fb in blocks]
        soft = [_attn_softmax(s, fb) for s, (_, fb) in zip(scores, blocks)]
        new = [_attn_values(*sm, keys_values(v_ref, nb, r, fb)) for sm, (nb, fb) in zip(soft, blocks)]
        for rw, st, nw in zip(rows, states, new):
            store_state(rw, st, nw)

    def per_residue(r, carry):
        block_pair(0, r, True)

        def per_pair(j, c):
            block_pair(2 * j, r, False)
            return c

        return lax.fori_loop(1, n_blocks // 2, per_pair, carry)

    if dilation == 1:
        per_residue(0, 0)
    else:
        lax.fori_loop(0, dilation, per_residue, 0)

    if g == N_GROUPS - 1:
        def write_out(c, carry):
            rows = pl.ds(pl.multiple_of(c * OUT_CHUNK, OUT_CHUNK), OUT_CHUNK)
            y_ref[rows, :] = _get_halves(acc_s, rows).astype(y_ref.dtype)
            return carry

        lax.fori_loop(0, seq // OUT_CHUNK, write_out, 0)


def _attn_kernel(q_ref, k_ref, v_ref, y_ref, acc_s, max_s, den_s, *, seq, tile):
    g = pl.program_id(1)
    for gi, (_, dilation) in enumerate(ATTN_GROUPS):
        @pl.when(g == gi)
        def _(gi=gi, dilation=dilation):
            _attn_group(gi, dilation, q_ref, k_ref, v_ref, y_ref, acc_s, max_s, den_s, seq=seq, tile=tile)


def _attention(q, k, v, *, tile):
    batch, _, seq, _ = q.shape
    spec = pl.BlockSpec((None, None, seq, GROUP_WIDTH), lambda b, g: (b, g, 0, 0))
    return pl.pallas_call(
        functools.partial(_attn_kernel, seq=seq, tile=tile),
        grid=(batch, N_GROUPS),
        in_specs=[spec, spec, spec],
        out_specs=pl.BlockSpec((None, seq, GROUP_WIDTH), lambda b, g: (b, 0, 0)),
        out_shape=jax.ShapeDtypeStruct((batch, seq, GROUP_WIDTH), BF16),
        scratch_shapes=[_halves_scratch(seq)] * 3,
        compiler_params=pltpu.CompilerParams(dimension_semantics=("arbitrary", "arbitrary"),
                                             vmem_limit_bytes=VMEM_LIMIT_BYTES),
        name="attention",
    )(q, k, v)


def _merge_kernel(x_ref, gain_ref, w_ref, bg_ref, d_ref, wmix_ref, pscale_ref, yc_ref, ya_ref,
                  wp_ref, wc_ref, wa_ref, wo_ref, gain2_ref, xo_ref, h2_ref):
    dm = x_ref.shape[1]
    x = x_ref[...]
    h = _rms_norm(x, gain_ref[...]).astype(BF16)

    y_pool = jnp.concatenate(
        [_dot(d_ref[:, g * POOL_GROUP:(g + 1) * POOL_GROUP], wmix_ref[g]) for g in range(len(POOL_WINDOWS))],
        axis=-1)
    y_pool = (y_pool * pscale_ref[...]).astype(BF16)

    merged = None
    for j, (y, wj_ref) in enumerate(((y_pool, wp_ref), (yc_ref[...], wc_ref), (ya_ref[...], wa_ref))):
        gate = jax.nn.sigmoid(_dot(h, w_ref[:, OFF_GATE + j * dm:OFF_GATE + (j + 1) * dm])
                              + bg_ref[:, j * dm:(j + 1) * dm])
        term = gate * _dot(y, wj_ref[...])
        merged = term if merged is None else merged + term

    x_new = x + _dot(merged.astype(BF16), wo_ref[...])
    xo_ref[...] = x_new
    h2_ref[...] = _rms_norm(x_new, gain2_ref[...]).astype(BF16)


def _merge(x2d, gain, w_in, b_gate, d, w_mix, pool_scale, yc, y_attn, w_pool_up, w_conv_out, w_attn_up, w_o,
           gain2, *, layer):
    t, dm = x2d.shape
    slab = lambda w: _layer_resident(layer, w.shape[1:])
    tm = TM_MERGE
    assert t % tm == 0
    row = lambda width: pl.BlockSpec((tm, width), lambda i: (i, 0))
    return pl.pallas_call(
        _merge_kernel,
        grid=(t // tm,),
        in_specs=[row(dm), _resident((1, dm)), slab(w_in), _resident(b_gate.shape),
                  row(POOL_WIDTH), slab(w_mix), _resident((1, POOL_WIDTH)), row(CONV_WIDTH),
                  row(GROUP_WIDTH), slab(w_pool_up), slab(w_conv_out), slab(w_attn_up), slab(w_o),
                  _resident((1, dm))],
        out_specs=[row(dm), row(dm)],
        out_shape=[jax.ShapeDtypeStruct((t, dm), F32), jax.ShapeDtypeStruct((t, dm), BF16)],
        compiler_params=pltpu.CompilerParams(dimension_semantics=("arbitrary",),
                                             vmem_limit_bytes=VMEM_LIMIT_BYTES),
        name="merge",
    )(x2d, gain, w_in, b_gate, d, w_mix, pool_scale, yc, y_attn, w_pool_up, w_conv_out, w_attn_up, w_o, gain2)


def _ffn_kernel(x_ref, h2_ref, w1_ref, w2_ref, o_ref):
    d_ff = w1_ref.shape[1]
    h2 = h2_ref[...]
    acc = None
    for c in range(d_ff // FF_CHUNK):
        cols = slice(c * FF_CHUNK, (c + 1) * FF_CHUNK)
        a = jnp.square(jnp.maximum(_dot(h2, w1_ref[:, cols]), 0.0)).astype(BF16)
        part = _dot(a, w2_ref[cols, :])
        acc = part if acc is None else acc + part
    o_ref[...] = x_ref[...] + acc


def _ffn(x2d, h2, w1, w2, *, layer):
    t, dm = x2d.shape
    tm = TM_FFN
    assert t % tm == 0 and w1.shape[2] % FF_CHUNK == 0
    row = pl.BlockSpec((tm, dm), lambda i: (i, 0))
    return pl.pallas_call(
        _ffn_kernel,
        grid=(t // tm,),
        in_specs=[row, row, _layer_resident(layer, w1.shape[1:]), _layer_resident(layer, w2.shape[1:])],
        out_specs=row,
        out_shape=jax.ShapeDtypeStruct((t, dm), F32),
        compiler_params=pltpu.CompilerParams(dimension_semantics=("arbitrary",),
                                             vmem_limit_bytes=VMEM_LIMIT_BYTES),
        name="ffn",
    )(x2d, h2, w1, w2)


def kernel(x, norm_mix, w_in, b_gate, pool_mix, pool_scale, conv_w, q_gain, k_gain, w_pool_up, w_conv_out,
           w_attn_up, w_o, norm_mlp, w_ff1, w_ff2):
    batch, seq, dm = x.shape
    depth = norm_mix.shape[0]
    assert w_in.shape[2] == OFF_GATE + N_BRANCH * dm
    head_id = jnp.arange(GROUP_WIDTH) // HEAD_DIM
    headmean = jnp.where(head_id[:, None] == head_id[None, :], 1.0 / HEAD_DIM, 0.0).astype(BF16)
    x2d = x.reshape(batch * seq, dm)
    w_in, pool_mix, w_pool_up, w_conv_out, w_attn_up, w_o, w_ff1, w_ff2 = (
        w.astype(BF16) for w in (w_in, pool_mix, w_pool_up, w_conv_out, w_attn_up, w_o, w_ff1, w_ff2))
    for l in range(depth):
        gain = norm_mix[l].reshape(1, dm)
        qgain = (jnp.tile(q_gain[l], HEADS_PER_GROUP) * (LOG2_E * HEAD_DIM ** -0.5)).reshape(1, GROUP_WIDTH)
        kgain = jnp.tile(k_gain[l], HEADS_PER_GROUP).reshape(1, GROUP_WIDTH)
        d, yc, q, k, v = _in_proj(x2d, gain, w_in, qgain, kgain, conv_w[l], headmean, layer=l, batch=batch,
                                  seq=seq)
        y_attn = _attention(q, k, v, tile=TM_IN).reshape(batch * seq, GROUP_WIDTH)
        x2d, h2 = _merge(x2d, gain, w_in, b_gate[l].reshape(1, N_BRANCH * dm), d, pool_mix,
                         pool_scale[l].reshape(1, POOL_WIDTH), yc, y_attn, w_pool_up, w_conv_out, w_attn_up, w_o,
                         norm_mlp[l].reshape(1, dm), layer=l)
        x2d = _ffn(x2d, h2, w_ff1, w_ff2, layer=l)
    return x2d.reshape(batch, seq, dm)
```

```python
import functools

import jax
import jax.numpy as jnp
from jax import lax
from jax.experimental import pallas as pl
from jax.experimental.pallas import tpu as pltpu

F32 = jnp.float32
BF16 = jnp.bfloat16

POOL_WINDOWS = (2, 4, 8, 16)
POOL_GROUP = 128
POOL_WIDTH = POOL_GROUP * len(POOL_WINDOWS)
CONV_WIDTH = 512
CONV_K = 3
HEAD_DIM = 64
ATTN_GROUPS = ((128, 1), (512, 4), (2048, 16))
HEADS_PER_GROUP = 4
GROUP_WIDTH = HEADS_PER_GROUP * HEAD_DIM
N_GROUPS = len(ATTN_GROUPS)
ATTN_WIDTH = N_GROUPS * GROUP_WIDTH
ATTN_BLOCK = 128
LANES = 128
BF16_ROWS = 16
N_BRANCH = 3
EPS = 1e-6
MASK_VALUE = -1e30
LOG2_E = 1.4426950408889634

OFF_POOL = 0
OFF_CONV = OFF_POOL + POOL_WIDTH
OFF_Q = OFF_CONV + 3 * CONV_WIDTH
OFF_K = OFF_Q + ATTN_WIDTH
OFF_V = OFF_K + ATTN_WIDTH
OFF_GATE = OFF_V + ATTN_WIDTH

POOL_HALO = 16
CONV_HALO = 8

VMEM_LIMIT_BYTES = 56 * 1024 * 1024
TM_IN = 512
TM_MERGE = 1024
TM_FFN = 1024
FF_CHUNK = 1024
ITEM_BLOCKS = 2
OUT_CHUNK = 512


def _resident(shape):
    return pl.BlockSpec(shape, lambda *_: (0,) * len(shape), pipeline_mode=pl.Buffered(1))


def _cast_specs(w, layer, steps, splits=()):
    _, rows, cols = w.shape
    assert rows % steps == 0 and (rows // steps) % BF16_ROWS == 0
    rb = rows // steps
    bounds = (0,) + tuple(splits) + (cols,)
    widths = [b - a for a, b in zip(bounds, bounds[1:])]
    in_spec = pl.BlockSpec((None, rb, cols), lambda i: (layer, i, 0))
    out_specs = [pl.BlockSpec((rb, wd), lambda i: (i, 0)) for wd in widths]
    out_shapes = [jax.ShapeDtypeStruct((rows, wd), BF16) for wd in widths]
    return in_spec, out_specs, out_shapes


def _cast_rows(src_ref, dst_refs):
    start = 0
    for dst in dst_refs:
        dst[...] = src_ref[:, start:start + dst.shape[1]].astype(BF16)
        start += dst.shape[1]


def _rms_norm(x, gain):
    ms = jnp.mean(x * x, axis=-1, keepdims=True)
    return x * lax.rsqrt(ms + EPS) * gain


def _dot(a, b):
    return jnp.dot(a, b, preferred_element_type=F32)


def _halves_scratch(rows):
    return pltpu.VMEM((GROUP_WIDTH // LANES, rows, LANES), F32)


def _put_halves(ref, rows, val):
    for j in range(GROUP_WIDTH // LANES):
        ref[j, rows, :] = val[:, j * LANES:(j + 1) * LANES]


def _get_halves(ref, rows):
    return jnp.concatenate([ref[j, rows, :] for j in range(GROUP_WIDTH // LANES)], axis=-1)


def _in_proj_kernel(x_ref, gain_ref, w_ref, qgain_ref, kgain_ref, convw_ref, headmean_ref,
                    d_ref, yc_ref, q_ref, k_ref, v_ref, uhalo, chalo, zbuf, *, tm, tiles_per_seq):
    i = pl.program_id(0)

    @pl.when(i == 0)
    def _():
        uhalo[...] = jnp.zeros_like(uhalo)
        chalo[...] = jnp.zeros_like(chalo)

    tile_in_seq = i % tiles_per_seq
    carry = tile_in_seq != 0
    h = _rms_norm(x_ref[...], gain_ref[...]).astype(BF16)

    def emit_heads(z_all, slot, gain, o_ref):
        zs = [z_all[:, g * GROUP_WIDTH:(g + 1) * GROUP_WIDTH] for g in range(N_GROUPS)]
        if gain is not None:
            ms = _dot(jnp.concatenate([(z * z).astype(BF16) for z in zs], axis=0), headmean_ref[...])
            zs = [z * lax.rsqrt(ms[g * tm:(g + 1) * tm, :] + EPS) * gain[...] for g, z in enumerate(zs)]
        for g, (_, dilation) in enumerate(ATTN_GROUPS):
            if dilation == 1:
                o_ref[g] = zs[g].astype(BF16)
            else:
                buf = zbuf.at[(g - 1) * 3 + slot]
                _put_halves(buf, slice(None), zs[g])
                chunk = tm // dilation
                for r in range(dilation):
                    rows = pl.ds(r, chunk, stride=dilation)
                    o_ref[g, r * chunk:(r + 1) * chunk, :] = _get_halves(buf, rows).astype(BF16)

    zpc = _dot(h, w_ref[:, OFF_POOL:OFF_Q])

    u = zpc[:, OFF_POOL:OFF_POOL + POOL_WIDTH]
    ext = jnp.concatenate([jnp.where(carry, uhalo[...], 0.0), u], axis=0)
    uhalo[...] = u[tm - POOL_HALO:tm, :]
    pos = tile_in_seq * tm + lax.broadcasted_iota(jnp.int32, (tm, 1), 0)
    run = ext
    for g, w in enumerate(POOL_WINDOWS):
        run = run + pltpu.roll(run, w // 2, 0)
        cols = slice(g * POOL_GROUP, (g + 1) * POOL_GROUP)
        inv_count = 1.0 / jnp.minimum(pos + 1, w).astype(F32)
        d_ref[:, cols] = (run[POOL_HALO:, 0:POOL_GROUP] * inv_count - u[:, cols]).astype(BF16)
        run = run[:, POOL_GROUP:]

    zc = zpc[:, OFF_CONV:OFF_CONV + 3 * CONV_WIDTH]
    uc = zc[:, CONV_WIDTH:2 * CONV_WIDTH] * zc[:, 2 * CONV_WIDTH:3 * CONV_WIDTH]
    ext = jnp.concatenate([jnp.where(carry, chalo[...], 0.0), uc], axis=0)
    chalo[...] = uc[tm - CONV_HALO:tm, :]
    y = convw_ref[CONV_K - 1:CONV_K, :] * uc
    for j in range(CONV_K - 1):
        y = y + convw_ref[j:j + 1, :] * pltpu.roll(ext, CONV_K - 1 - j, 0)[CONV_HALO:, :]
    yc_ref[...] = (zc[:, 0:CONV_WIDTH] * y).astype(BF16)

    zqkv = _dot(h, w_ref[:, OFF_Q:OFF_GATE])
    for slot, (gain, o_ref) in enumerate(((qgain_ref, q_ref), (kgain_ref, k_ref), (None, v_ref))):
        emit_heads(zqkv[:, slot * ATTN_WIDTH:(slot + 1) * ATTN_WIDTH], slot, gain, o_ref)


def _in_proj(x2d, gain, w_mixers, qgain, kgain, convw, headmean, *, batch, seq):
    t, dm = x2d.shape
    assert w_mixers.shape == (dm, OFF_GATE)
    tm = TM_IN
    tiles_per_seq = seq // tm
    assert seq % tm == 0 and all(tm % (16 * d) == 0 for _, d in ATTN_GROUPS)
    row = lambda width: pl.BlockSpec((tm, width), lambda i: (i, 0))
    qkv_spec = pl.BlockSpec((None, N_GROUPS, tm, GROUP_WIDTH),
                            lambda i: (i // tiles_per_seq, 0, i % tiles_per_seq, 0))
    qkv_shape = jax.ShapeDtypeStruct((batch, N_GROUPS, seq, GROUP_WIDTH), BF16)
    n_regrouped = 3 * sum(1 for _, d in ATTN_GROUPS if d > 1)
    return pl.pallas_call(
        functools.partial(_in_proj_kernel, tm=tm, tiles_per_seq=tiles_per_seq),
        grid=(t // tm,),
        in_specs=[row(dm), _resident((1, dm)), _resident(w_mixers.shape),
                  _resident((1, GROUP_WIDTH)), _resident((1, GROUP_WIDTH)), _resident((CONV_K, CONV_WIDTH)),
                  _resident((GROUP_WIDTH, GROUP_WIDTH))],
        out_specs=[row(POOL_WIDTH), row(CONV_WIDTH), qkv_spec, qkv_spec, qkv_spec],
        out_shape=[jax.ShapeDtypeStruct((t, POOL_WIDTH), BF16), jax.ShapeDtypeStruct((t, CONV_WIDTH), BF16),
                   qkv_shape, qkv_shape, qkv_shape],
        scratch_shapes=[pltpu.VMEM((POOL_HALO, POOL_WIDTH), F32), pltpu.VMEM((CONV_HALO, CONV_WIDTH), F32),
                        pltpu.VMEM((n_regrouped, GROUP_WIDTH // LANES, tm, LANES), F32)],
        compiler_params=pltpu.CompilerParams(dimension_semantics=("arbitrary",),
                                             vmem_limit_bytes=VMEM_LIMIT_BYTES),
        name="in_proj",
    )(x2d, gain, w_mixers, qgain, kgain, convw, headmean)


def _in_head_masks():
    lane_head = lax.broadcasted_iota(jnp.int32, (1, GROUP_WIDTH), 1) // HEAD_DIM
    return [lane_head == hd for hd in range(HEADS_PER_GROUP)]


def _attn_scores(qb, kb):
    q_heads = jnp.concatenate([jnp.where(m, qb, jnp.zeros_like(qb)) for m in _in_head_masks()], axis=0)
    return lax.dot_general(q_heads, kb, (((1,), (1,)), ((), ())), preferred_element_type=F32)


def _attn_softmax(s_ref, first):
    blk = ATTN_BLOCK
    row = lax.broadcasted_iota(jnp.int32, (blk, blk), 0)
    col = lax.broadcasted_iota(jnp.int32, (blk, blk), 1)
    mask_value = MASK_VALUE * LOG2_E
    probs, row_max, row_den = [], [], []
    for hd in range(HEADS_PER_GROUP):
        rows = slice(hd * blk, (hd + 1) * blk)

        def masked():
            if first:
                return jnp.where(col <= row, s_ref[rows, 0:blk], mask_value)
            return jnp.concatenate([jnp.where(col >= row, s_ref[rows, 0:blk], mask_value),
                                    jnp.where(col <= row, s_ref[rows, blk:2 * blk], mask_value)], axis=1)

        m = jnp.max(masked(), axis=-1, keepdims=True)
        p = jnp.exp2(masked() - m)
        row_max.append(m)
        row_den.append(jnp.sum(p, axis=-1, keepdims=True))
        probs.append(p.astype(BF16))
    return jnp.concatenate(probs, axis=0), row_max, row_den


def _attn_values(probs, row_max, row_den, vb):
    blk = ATTN_BLOCK
    in_head = _in_head_masks()
    pv = _dot(probs, vb)
    shape = (blk, GROUP_WIDTH)
    acc, mx, den = pv[0:blk, :], jnp.broadcast_to(row_max[0], shape), jnp.broadcast_to(row_den[0], shape)
    for hd in range(1, HEADS_PER_GROUP):
        acc = jnp.where(in_head[hd], pv[hd * blk:(hd + 1) * blk, :], acc)
        mx = jnp.where(in_head[hd], row_max[hd], mx)
        den = jnp.where(in_head[hd], row_den[hd], den)
    return acc, mx, den


def _block_rows(ref, n, r, dilation, tile):
    chunk = tile // dilation
    piece = min(ATTN_BLOCK, chunk)
    parts = []
    for j in range(ATTN_BLOCK // piece):
        pos = n * ATTN_BLOCK + j * piece
        start = (pos // chunk) * tile + r * chunk + pos % chunk
        if not isinstance(start, int):
            start = pl.multiple_of(start, piece)
        parts.append(ref[pl.ds(start, piece), :])
    return parts[0] if len(parts) == 1 else jnp.concatenate(parts, axis=0)


def _attn_group(g, dilation, q_ref, k_ref, v_ref, y_ref, acc_s, max_s, den_s, score_s, *, seq, tile):
    blk = ATTN_BLOCK
    n_blocks = seq // dilation // blk
    assert n_blocks % 2 == 0

    def token_rows(n, r):
        start = n * (blk * dilation) + r
        return pl.ds(start, blk) if dilation == 1 else pl.ds(start, blk, stride=dilation)

    def load_state(rows):
        if g == 0:
            return None
        return _get_halves(acc_s, rows), _get_halves(max_s, rows), _get_halves(den_s, rows)

    def store_state(rows, state, new):
        acc, mx, den = new
        if state is not None:
            acc_old, mx_old, den_old = state
            mx_new = jnp.maximum(mx_old, mx)
            w_old, w_new = jnp.exp2(mx_old - mx_new), jnp.exp2(mx - mx_new)
            acc, den, mx = w_old * acc_old + w_new * acc, w_old * den_old + w_new * den, mx_new
        if g == N_GROUPS - 1:
            _put_halves(acc_s, rows, acc / den)
        else:
            _put_halves(acc_s, rows, acc)
            _put_halves(max_s, rows, mx)
            _put_halves(den_s, rows, den)

    def keys_values(ref, n, r, first):
        own = _block_rows(ref, n, r, dilation, tile)
        if first:
            return own
        return jnp.concatenate([_block_rows(ref, n - 1, r, dilation, tile), own], axis=0)

    width = 2 if n_blocks == 2 else ITEM_BLOCKS

    def pair(n, first):
        return tuple((n + i, first and i == 0) for i in range(width))

    def park_scores(slot, n, r, first):
        for i, (nb, fb) in enumerate(pair(n, first)):
            s = _attn_scores(_block_rows(q_ref, nb, r, dilation, tile), keys_values(k_ref, nb, r, fb))
            score_s[slot, i, :, 0:s.shape[1]] = s

    def finish(slot, n, r, first):
        blocks = pair(n, first)
        rows = [token_rows(nb, r) for nb, _ in blocks]
        states = [load_state(rw) for rw in rows]
        soft = [_attn_softmax(score_s.at[slot, i], fb) for i, (_, fb) in enumerate(blocks)]
        new = [_attn_values(*sm, keys_values(v_ref, nb, r, fb)) for sm, (nb, fb) in zip(soft, blocks)]
        for rw, st, nw in zip(rows, states, new):
            store_state(rw, st, nw)

    def run_items(n_items, item):
        assert n_items >= 2 and n_items % 2 == 0
        park_scores(0, *item(0, True))
        park_scores(1, *item(1, False))
        finish(0, *item(0, True))

        def double_step(t, c):
            i = 2 * t + 1
            park_scores(0, *item(i + 1, False))
            finish(1, *item(i, False))
            park_scores(1, *item(i + 2, False))
            finish(0, *item(i + 1, False))
            return c

        lax.fori_loop(0, (n_items - 2) // 2, double_step, 0)
        finish(1, *item(n_items - 1, False))

    if n_blocks == 2:
        run_items(dilation, lambda i, lead: (0, i, True))
    elif dilation == 1:
        run_items(n_blocks // width, lambda i, lead: (width * i, 0, lead))
    else:
        def per_residue(r, carry):
            run_items(n_blocks // width, lambda i, lead: (width * i, r, lead))
            return carry

        lax.fori_loop(0, dilation, per_residue, 0)

    if g == N_GROUPS - 1:
        def write_out(c, carry):
            rows = pl.ds(pl.multiple_of(c * OUT_CHUNK, OUT_CHUNK), OUT_CHUNK)
            y_ref[rows, :] = _get_halves(acc_s, rows).astype(y_ref.dtype)
            return carry

        lax.fori_loop(0, seq // OUT_CHUNK, write_out, 0)


def _attn_kernel(q_ref, k_ref, v_ref, y_ref, acc_s, max_s, den_s, score_s, *, seq, tile):
    g = pl.program_id(1)
    for gi, (_, dilation) in enumerate(ATTN_GROUPS):
        @pl.when(g == gi)
        def _(gi=gi, dilation=dilation):
            _attn_group(gi, dilation, q_ref, k_ref, v_ref, y_ref, acc_s, max_s, den_s, score_s, seq=seq,
                        tile=tile)


def _attention(q, k, v, *, tile):
    batch, _, seq, _ = q.shape
    spec = pl.BlockSpec((None, None, seq, GROUP_WIDTH), lambda b, g: (b, g, 0, 0))
    return pl.pallas_call(
        functools.partial(_attn_kernel, seq=seq, tile=tile),
        grid=(batch, N_GROUPS),
        in_specs=[spec, spec, spec],
        out_specs=pl.BlockSpec((None, seq, GROUP_WIDTH), lambda b, g: (b, 0, 0)),
        out_shape=jax.ShapeDtypeStruct((batch, seq, GROUP_WIDTH), BF16),
        scratch_shapes=[_halves_scratch(seq)] * 3
                       + [pltpu.VMEM((2, 2, HEADS_PER_GROUP * ATTN_BLOCK, 2 * ATTN_BLOCK), F32)],
        compiler_params=pltpu.CompilerParams(dimension_semantics=("arbitrary", "arbitrary"),
                                             vmem_limit_bytes=VMEM_LIMIT_BYTES),
        name="attention",
    )(q, k, v)


def _merge_kernel(x_ref, gain_ref, w_ref, bg_ref, d_ref, wmix_ref, pscale_ref, yc_ref, ya_ref,
                  wp_ref, wc_ref, wa_ref, wo_ref, gain2_ref, wff1_src, wff2_src,
                  xo_ref, h2_ref, wff1_dst, wff2_dst):
    _cast_rows(wff1_src, [wff1_dst])
    _cast_rows(wff2_src, [wff2_dst])
    dm = x_ref.shape[1]
    x = x_ref[...]
    h = _rms_norm(x, gain_ref[...]).astype(BF16)

    y_pool = jnp.concatenate(
        [_dot(d_ref[:, g * POOL_GROUP:(g + 1) * POOL_GROUP], wmix_ref[g]) for g in range(len(POOL_WINDOWS))],
        axis=-1)
    y_pool = (y_pool * pscale_ref[...]).astype(BF16)

    merged = None
    for j, (y, wj_ref) in enumerate(((y_pool, wp_ref), (yc_ref[...], wc_ref), (ya_ref[...], wa_ref))):
        gate_cols = slice(j * dm, (j + 1) * dm)
        gate = jax.nn.sigmoid(_dot(h, w_ref[:, gate_cols]) + bg_ref[:, gate_cols])
        term = gate * _dot(y, wj_ref[...])
        merged = term if merged is None else merged + term

    x_new = x + _dot(merged.astype(BF16), wo_ref[...])
    xo_ref[...] = x_new
    h2_ref[...] = _rms_norm(x_new, gain2_ref[...]).astype(BF16)


def _merge(x2d, gain, w_gates, b_gate, d, w_mix, pool_scale, yc, y_attn, w_pool_up, w_conv_out, w_attn_up, w_o,
           gain2, w_ff1, w_ff2, *, layer):
    t, dm = x2d.shape
    tm = TM_MERGE
    assert t % tm == 0
    steps = t // tm
    row = lambda width: pl.BlockSpec((tm, width), lambda i: (i, 0))
    cast_in, cast_out, cast_shape = zip(*(_cast_specs(w, layer, steps) for w in (w_ff1, w_ff2)))
    res = lambda w: _resident(w.shape)
    return pl.pallas_call(
        _merge_kernel,
        grid=(steps,),
        in_specs=[row(dm), _resident((1, dm)), res(w_gates), res(b_gate), row(POOL_WIDTH), res(w_mix),
                  _resident((1, POOL_WIDTH)), row(CONV_WIDTH), row(GROUP_WIDTH), res(w_pool_up),
                  res(w_conv_out), res(w_attn_up), res(w_o), _resident((1, dm)), *cast_in],
        out_specs=[row(dm), row(dm)] + [s for specs in cast_out for s in specs],
        out_shape=[jax.ShapeDtypeStruct((t, dm), F32), jax.ShapeDtypeStruct((t, dm), BF16)]
                  + [s for shapes in cast_shape for s in shapes],
        compiler_params=pltpu.CompilerParams(dimension_semantics=("arbitrary",),
                                             vmem_limit_bytes=VMEM_LIMIT_BYTES),
        name="merge",
    )(x2d, gain, w_gates, b_gate, d, w_mix, pool_scale, yc, y_attn, w_pool_up, w_conv_out, w_attn_up, w_o, gain2,
      w_ff1, w_ff2)


def _ffn_kernel(x_ref, h2_ref, w1_ref, w2_ref, *rest, cast_layout):
    srcs, o_ref, dsts = rest[:len(cast_layout)], rest[len(cast_layout)], list(rest[len(cast_layout) + 1:])
    for src, n_dst in zip(srcs, cast_layout):
        _cast_rows(src, [dsts.pop(0) for _ in range(n_dst)])
    d_ff = w1_ref.shape[1]
    h2 = h2_ref[...]
    acc = None
    for c in range(d_ff // FF_CHUNK):
        cols = slice(c * FF_CHUNK, (c + 1) * FF_CHUNK)
        a = jnp.square(jnp.maximum(_dot(h2, w1_ref[:, cols]), 0.0)).astype(BF16)
        part = _dot(a, w2_ref[cols, :])
        acc = part if acc is None else acc + part
    o_ref[...] = x_ref[...] + acc


def _ffn(x2d, h2, w1, w2, cast_jobs=()):
    t, dm = x2d.shape
    tm = TM_FFN
    assert t % tm == 0 and w1.shape[1] % FF_CHUNK == 0
    steps = t // tm
    row = pl.BlockSpec((tm, dm), lambda i: (i, 0))
    cast_in, cast_out, cast_shape = [], [], []
    for w, layer, splits in cast_jobs:
        in_spec, out_specs, out_shapes = _cast_specs(w, layer, steps, splits)
        cast_in.append(in_spec)
        cast_out.append(out_specs)
        cast_shape.append(out_shapes)
    return pl.pallas_call(
        functools.partial(_ffn_kernel, cast_layout=tuple(len(s) for s in cast_out)),
        grid=(steps,),
        in_specs=[row, row, _resident(w1.shape), _resident(w2.shape), *cast_in],
        out_specs=[row] + [s for specs in cast_out for s in specs],
        out_shape=[jax.ShapeDtypeStruct((t, dm), F32)] + [s for shapes in cast_shape for s in shapes],
        compiler_params=pltpu.CompilerParams(dimension_semantics=("arbitrary",),
                                             vmem_limit_bytes=VMEM_LIMIT_BYTES),
        name="ffn",
    )(x2d, h2, w1, w2, *(w for w, _, _ in cast_jobs))


def kernel(x, norm_mix, w_in, b_gate, pool_mix, pool_scale, conv_w, q_gain, k_gain, w_pool_up, w_conv_out,
           w_attn_up, w_o, norm_mlp, w_ff1, w_ff2):
    batch, seq, dm = x.shape
    depth = norm_mix.shape[0]
    assert w_in.shape[2] == OFF_GATE + N_BRANCH * dm
    head_id = jnp.arange(GROUP_WIDTH) // HEAD_DIM
    headmean = jnp.where(head_id[:, None] == head_id[None, :], 1.0 / HEAD_DIM, 0.0).astype(BF16)
    x2d = x.reshape(batch * seq, dm)
    pool_mix = pool_mix.astype(BF16)
    w_mixers, w_gates = w_in[0, :, :OFF_GATE].astype(BF16), w_in[0, :, OFF_GATE:].astype(BF16)
    up_weights = (w_pool_up, w_conv_out, w_attn_up, w_o)
    ups = [w[0].astype(BF16) for w in up_weights]
    for l in range(depth):
        gain = norm_mix[l].reshape(1, dm)
        qgain = (jnp.tile(q_gain[l], HEADS_PER_GROUP) * (LOG2_E * HEAD_DIM ** -0.5)).reshape(1, GROUP_WIDTH)
        kgain = jnp.tile(k_gain[l], HEADS_PER_GROUP).reshape(1, GROUP_WIDTH)
        d, yc, q, k, v = _in_proj(x2d, gain, w_mixers, qgain, kgain, conv_w[l], headmean, batch=batch, seq=seq)
        y_attn = _attention(q, k, v, tile=TM_IN).reshape(batch * seq, GROUP_WIDTH)
        x2d, h2, w1, w2 = _merge(x2d, gain, w_gates, b_gate[l].reshape(1, N_BRANCH * dm), d, pool_mix[l],
                                 pool_scale[l].reshape(1, POOL_WIDTH), yc, y_attn, *ups,
                                 norm_mlp[l].reshape(1, dm), w_ff1, w_ff2, layer=l)
        if l + 1 < depth:
            jobs = [(w_in, l + 1, (OFF_GATE,))] + [(w, l + 1, ()) for w in up_weights]
            x2d, w_mixers, w_gates, *ups = _ffn(x2d, h2, w1, w2, jobs)
        else:
            x2d, = _ffn(x2d, h2, w1, w2)
    return x2d.reshape(batch, seq, dm)
```

```python
import functools

import jax
import jax.numpy as jnp
from jax import lax
from jax.experimental import pallas as pl
from jax.experimental.pallas import tpu as pltpu

F32 = jnp.float32
BF16 = jnp.bfloat16

POOL_WINDOWS = (2, 4, 8, 16)
POOL_GROUP = 128
POOL_WIDTH = POOL_GROUP * len(POOL_WINDOWS)
CONV_WIDTH = 512
CONV_K = 3
HEAD_DIM = 64
ATTN_GROUPS = ((128, 1), (512, 4), (2048, 16))
HEADS_PER_GROUP = 4
GROUP_WIDTH = HEADS_PER_GROUP * HEAD_DIM
N_GROUPS = len(ATTN_GROUPS)
ATTN_WIDTH = N_GROUPS * GROUP_WIDTH
ATTN_BLOCK = 128
LANES = 128
BF16_ROWS = 16
N_BRANCH = 3
EPS = 1e-6
MASK_VALUE = -1e30
LOG2_E = 1.4426950408889634

OFF_POOL = 0
OFF_CONV = OFF_POOL + POOL_WIDTH
OFF_Q = OFF_CONV + 3 * CONV_WIDTH
OFF_K = OFF_Q + ATTN_WIDTH
OFF_V = OFF_K + ATTN_WIDTH
OFF_GATE = OFF_V + ATTN_WIDTH

POOL_HALO = 16
CONV_HALO = 8

VMEM_LIMIT_BYTES = 56 * 1024 * 1024
TM_IN = 512
TM_MERGE = 1024
TM_FFN = 1024
FF_CHUNK = 1024
CAST_STEPS = 8
ITEM_BLOCKS = 2
OUT_CHUNK = 512


def _resident(shape):
    return pl.BlockSpec(shape, lambda *_: (0,) * len(shape), pipeline_mode=pl.Buffered(1))


def _cast_specs(w, layer, steps, splits=()):
    _, rows, cols = w.shape
    assert rows % steps == 0 and (rows // steps) % BF16_ROWS == 0
    rb = rows // steps
    bounds = (0,) + tuple(splits) + (cols,)
    widths = [b - a for a, b in zip(bounds, bounds[1:])]
    in_spec = pl.BlockSpec((None, rb, cols), lambda i: (layer, i, 0))
    out_specs = [pl.BlockSpec((rb, wd), lambda i: (i, 0)) for wd in widths]
    out_shapes = [jax.ShapeDtypeStruct((rows, wd), BF16) for wd in widths]
    return in_spec, out_specs, out_shapes


def _cast_rows(src_ref, dst_refs):
    start = 0
    for dst in dst_refs:
        dst[...] = src_ref[:, start:start + dst.shape[1]].astype(BF16)
        start += dst.shape[1]


def _cast_jobs_specs(jobs, steps):
    in_specs, out_specs, out_shapes, layout = [], [], [], []
    for w, layer, splits in jobs:
        in_spec, o_specs, o_shapes = _cast_specs(w, layer, steps, splits)
        in_specs.append(in_spec)
        out_specs += o_specs
        out_shapes += o_shapes
        layout.append(len(o_specs))
    return in_specs, out_specs, out_shapes, tuple(layout)


def _run_cast_jobs(srcs, dsts, layout):
    dsts = list(dsts)
    for src, n_dst in zip(srcs, layout):
        _cast_rows(src, [dsts.pop(0) for _ in range(n_dst)])


def _cast_kernel(*refs, layout):
    _run_cast_jobs(refs[:len(layout)], refs[len(layout):], layout)


def _cast_weights(jobs):
    in_specs, out_specs, out_shapes, layout = _cast_jobs_specs(jobs, CAST_STEPS)
    return pl.pallas_call(
        functools.partial(_cast_kernel, layout=layout),
        grid=(CAST_STEPS,),
        in_specs=in_specs,
        out_specs=out_specs,
        out_shape=out_shapes,
        compiler_params=pltpu.CompilerParams(dimension_semantics=("arbitrary",),
                                             vmem_limit_bytes=VMEM_LIMIT_BYTES),
        name="cast_weights",
    )(*(w for w, _, _ in jobs))


def _rms_norm(x, gain):
    ms = jnp.mean(x * x, axis=-1, keepdims=True)
    return x * lax.rsqrt(ms + EPS) * gain


def _dot(a, b):
    return jnp.dot(a, b, preferred_element_type=F32)


def _halves_scratch(rows):
    return pltpu.VMEM((GROUP_WIDTH // LANES, rows, LANES), F32)


def _put_halves(ref, rows, val):
    for j in range(GROUP_WIDTH // LANES):
        ref[j, rows, :] = val[:, j * LANES:(j + 1) * LANES]


def _get_halves(ref, rows):
    return jnp.concatenate([ref[j, rows, :] for j in range(GROUP_WIDTH // LANES)], axis=-1)


def _in_proj_kernel(x_ref, gain_ref, w_ref, qgain_ref, kgain_ref, convw_ref, headmean_ref,
                    d_ref, yc_ref, q_ref, k_ref, v_ref, uhalo, chalo, zbuf, *, tm, tiles_per_seq):
    i = pl.program_id(0)

    @pl.when(i == 0)
    def _():
        uhalo[...] = jnp.zeros_like(uhalo)
        chalo[...] = jnp.zeros_like(chalo)

    tile_in_seq = i % tiles_per_seq
    carry = tile_in_seq != 0
    h = _rms_norm(x_ref[...], gain_ref[...]).astype(BF16)

    def emit_heads(z_all, slot, gain, o_ref):
        zs = [z_all[:, g * GROUP_WIDTH:(g + 1) * GROUP_WIDTH] for g in range(N_GROUPS)]
        if gain is not None:
            ms = _dot(jnp.concatenate([(z * z).astype(BF16) for z in zs], axis=0), headmean_ref[...])
            zs = [z * lax.rsqrt(ms[g * tm:(g + 1) * tm, :] + EPS) * gain[...] for g, z in enumerate(zs)]
        for g, (_, dilation) in enumerate(ATTN_GROUPS):
            if dilation == 1:
                o_ref[g] = zs[g].astype(BF16)
            else:
                buf = zbuf.at[(g - 1) * 3 + slot]
                _put_halves(buf, slice(None), zs[g])
                chunk = tm // dilation
                for r in range(dilation):
                    rows = pl.ds(r, chunk, stride=dilation)
                    o_ref[g, r * chunk:(r + 1) * chunk, :] = _get_halves(buf, rows).astype(BF16)

    zpc = _dot(h, w_ref[:, OFF_POOL:OFF_Q])

    u = zpc[:, OFF_POOL:OFF_POOL + POOL_WIDTH]
    ext = jnp.concatenate([jnp.where(carry, uhalo[...], 0.0), u], axis=0)
    uhalo[...] = u[tm - POOL_HALO:tm, :]
    pos = tile_in_seq * tm + lax.broadcasted_iota(jnp.int32, (tm, 1), 0)
    run = ext
    for g, w in enumerate(POOL_WINDOWS):
        run = run + pltpu.roll(run, w // 2, 0)
        cols = slice(g * POOL_GROUP, (g + 1) * POOL_GROUP)
        inv_count = 1.0 / jnp.minimum(pos + 1, w).astype(F32)
        d_ref[:, cols] = (run[POOL_HALO:, 0:POOL_GROUP] * inv_count - u[:, cols]).astype(BF16)
        run = run[:, POOL_GROUP:]

    zc = zpc[:, OFF_CONV:OFF_CONV + 3 * CONV_WIDTH]
    uc = zc[:, CONV_WIDTH:2 * CONV_WIDTH] * zc[:, 2 * CONV_WIDTH:3 * CONV_WIDTH]
    ext = jnp.concatenate([jnp.where(carry, chalo[...], 0.0), uc], axis=0)
    chalo[...] = uc[tm - CONV_HALO:tm, :]
    y = convw_ref[CONV_K - 1:CONV_K, :] * uc
    for j in range(CONV_K - 1):
        y = y + convw_ref[j:j + 1, :] * pltpu.roll(ext, CONV_K - 1 - j, 0)[CONV_HALO:, :]
    yc_ref[...] = (zc[:, 0:CONV_WIDTH] * y).astype(BF16)

    zqkv = _dot(h, w_ref[:, OFF_Q:OFF_GATE])
    for slot, (gain, o_ref) in enumerate(((qgain_ref, q_ref), (kgain_ref, k_ref), (None, v_ref))):
        emit_heads(zqkv[:, slot * ATTN_WIDTH:(slot + 1) * ATTN_WIDTH], slot, gain, o_ref)


def _in_proj(x2d, gain, w_mixers, qgain, kgain, convw, headmean, *, batch, seq):
    t, dm = x2d.shape
    assert w_mixers.shape == (dm, OFF_GATE)
    tm = TM_IN
    tiles_per_seq = seq // tm
    assert seq % tm == 0 and all(tm % (16 * d) == 0 for _, d in ATTN_GROUPS)
    row = lambda width: pl.BlockSpec((tm, width), lambda i: (i, 0))
    qkv_spec = pl.BlockSpec((None, N_GROUPS, tm, GROUP_WIDTH),
                            lambda i: (i // tiles_per_seq, 0, i % tiles_per_seq, 0))
    qkv_shape = jax.ShapeDtypeStruct((batch, N_GROUPS, seq, GROUP_WIDTH), BF16)
    n_regrouped = 3 * sum(1 for _, d in ATTN_GROUPS if d > 1)
    return pl.pallas_call(
        functools.partial(_in_proj_kernel, tm=tm, tiles_per_seq=tiles_per_seq),
        grid=(t // tm,),
        in_specs=[row(dm), _resident((1, dm)), _resident(w_mixers.shape),
                  _resident((1, GROUP_WIDTH)), _resident((1, GROUP_WIDTH)), _resident((CONV_K, CONV_WIDTH)),
                  _resident((GROUP_WIDTH, GROUP_WIDTH))],
        out_specs=[row(POOL_WIDTH), row(CONV_WIDTH), qkv_spec, qkv_spec, qkv_spec],
        out_shape=[jax.ShapeDtypeStruct((t, POOL_WIDTH), BF16), jax.ShapeDtypeStruct((t, CONV_WIDTH), BF16),
                   qkv_shape, qkv_shape, qkv_shape],
        scratch_shapes=[pltpu.VMEM((POOL_HALO, POOL_WIDTH), F32), pltpu.VMEM((CONV_HALO, CONV_WIDTH), F32),
                        pltpu.VMEM((n_regrouped, GROUP_WIDTH // LANES, tm, LANES), F32)],
        compiler_params=pltpu.CompilerParams(dimension_semantics=("arbitrary",),
                                             vmem_limit_bytes=VMEM_LIMIT_BYTES),
        name="in_proj",
    )(x2d, gain, w_mixers, qgain, kgain, convw, headmean)


def _in_head_masks():
    lane_head = lax.broadcasted_iota(jnp.int32, (1, GROUP_WIDTH), 1) // HEAD_DIM
    return [lane_head == hd for hd in range(HEADS_PER_GROUP)]


def _attn_scores(qb, kb):
    q_heads = jnp.concatenate([jnp.where(m, qb, jnp.zeros_like(qb)) for m in _in_head_masks()], axis=0)
    return lax.dot_general(q_heads, kb, (((1,), (1,)), ((), ())), preferred_element_type=F32)


def _attn_softmax(s_ref, first):
    blk = ATTN_BLOCK
    row = lax.broadcasted_iota(jnp.int32, (blk, blk), 0)
    col = lax.broadcasted_iota(jnp.int32, (blk, blk), 1)
    mask_value = MASK_VALUE * LOG2_E
    probs, row_max, row_den = [], [], []
    for hd in range(HEADS_PER_GROUP):
        rows = slice(hd * blk, (hd + 1) * blk)

        def masked():
            if first:
                return jnp.where(col <= row, s_ref[rows, 0:blk], mask_value)
            return jnp.concatenate([jnp.where(col >= row, s_ref[rows, 0:blk], mask_value),
                                    jnp.where(col <= row, s_ref[rows, blk:2 * blk], mask_value)], axis=1)

        m = jnp.max(masked(), axis=-1, keepdims=True)
        p = jnp.exp2(masked() - m)
        row_max.append(m)
        row_den.append(jnp.sum(p, axis=-1, keepdims=True))
        probs.append(p.astype(BF16))
    return jnp.concatenate(probs, axis=0), row_max, row_den


def _attn_values(probs, row_max, row_den, vb):
    blk = ATTN_BLOCK
    in_head = _in_head_masks()
    pv = _dot(probs, vb)
    shape = (blk, GROUP_WIDTH)
    acc, mx, den = pv[0:blk, :], jnp.broadcast_to(row_max[0], shape), jnp.broadcast_to(row_den[0], shape)
    for hd in range(1, HEADS_PER_GROUP):
        acc = jnp.where(in_head[hd], pv[hd * blk:(hd + 1) * blk, :], acc)
        mx = jnp.where(in_head[hd], row_max[hd], mx)
        den = jnp.where(in_head[hd], row_den[hd], den)
    return acc, mx, den


def _block_rows(ref, n, r, dilation, tile):
    chunk = tile // dilation
    piece = min(ATTN_BLOCK, chunk)
    parts = []
    for j in range(ATTN_BLOCK // piece):
        pos = n * ATTN_BLOCK + j * piece
        start = (pos // chunk) * tile + r * chunk + pos % chunk
        if not isinstance(start, int):
            start = pl.multiple_of(start, piece)
        parts.append(ref[pl.ds(start, piece), :])
    return parts[0] if len(parts) == 1 else jnp.concatenate(parts, axis=0)


def _attn_group(g, dilation, q_ref, k_ref, v_ref, y_ref, acc_s, max_s, den_s, score_s, *, seq, tile):
    blk = ATTN_BLOCK
    n_blocks = seq // dilation // blk
    assert n_blocks % 2 == 0

    def token_rows(n, r):
        start = n * (blk * dilation) + r
        return pl.ds(start, blk) if dilation == 1 else pl.ds(start, blk, stride=dilation)

    def load_state(rows):
        if g == 0:
            return None
        return _get_halves(acc_s, rows), _get_halves(max_s, rows), _get_halves(den_s, rows)

    def store_state(rows, state, new):
        acc, mx, den = new
        if state is not None:
            acc_old, mx_old, den_old = state
            mx_new = jnp.maximum(mx_old, mx)
            w_old, w_new = jnp.exp2(mx_old - mx_new), jnp.exp2(mx - mx_new)
            acc, den, mx = w_old * acc_old + w_new * acc, w_old * den_old + w_new * den, mx_new
        if g == N_GROUPS - 1:
            _put_halves(acc_s, rows, acc / den)
        else:
            _put_halves(acc_s, rows, acc)
            _put_halves(max_s, rows, mx)
            _put_halves(den_s, rows, den)

    def keys_values(ref, n, r, first):
        own = _block_rows(ref, n, r, dilation, tile)
        if first:
            return own
        return jnp.concatenate([_block_rows(ref, n - 1, r, dilation, tile), own], axis=0)

    width = 2 if n_blocks == 2 else ITEM_BLOCKS

    def pair(n, first):
        return tuple((n + i, first and i == 0) for i in range(width))

    def park_scores(slot, n, r, first):
        for i, (nb, fb) in enumerate(pair(n, first)):
            s = _attn_scores(_block_rows(q_ref, nb, r, dilation, tile), keys_values(k_ref, nb, r, fb))
            score_s[slot, i, :, 0:s.shape[1]] = s

    def finish(slot, n, r, first):
        blocks = pair(n, first)
        rows = [token_rows(nb, r) for nb, _ in blocks]
        states = [load_state(rw) for rw in rows]
        soft = [_attn_softmax(score_s.at[slot, i], fb) for i, (_, fb) in enumerate(blocks)]
        new = [_attn_values(*sm, keys_values(v_ref, nb, r, fb)) for sm, (nb, fb) in zip(soft, blocks)]
        for rw, st, nw in zip(rows, states, new):
            store_state(rw, st, nw)

    def run_items(n_items, item):
        assert n_items >= 2 and n_items % 2 == 0
        park_scores(0, *item(0, True))
        park_scores(1, *item(1, False))
        finish(0, *item(0, True))

        def double_step(t, c):
            i = 2 * t + 1
            park_scores(0, *item(i + 1, False))
            finish(1, *item(i, False))
            park_scores(1, *item(i + 2, False))
            finish(0, *item(i + 1, False))
            return c

        lax.fori_loop(0, (n_items - 2) // 2, double_step, 0)
        finish(1, *item(n_items - 1, False))

    if n_blocks == 2:
        run_items(dilation, lambda i, lead: (0, i, True))
    elif dilation == 1:
        run_items(n_blocks // width, lambda i, lead: (width * i, 0, lead))
    else:
        def per_residue(r, carry):
            run_items(n_blocks // width, lambda i, lead: (width * i, r, lead))
            return carry

        lax.fori_loop(0, dilation, per_residue, 0)

    if g == N_GROUPS - 1:
        def write_out(c, carry):
            rows = pl.ds(pl.multiple_of(c * OUT_CHUNK, OUT_CHUNK), OUT_CHUNK)
            y_ref[rows, :] = _get_halves(acc_s, rows).astype(y_ref.dtype)
            return carry

        lax.fori_loop(0, seq // OUT_CHUNK, write_out, 0)


def _attn_kernel(q_ref, k_ref, v_ref, y_ref, acc_s, max_s, den_s, score_s, *, seq, tile):
    g = pl.program_id(1)
    for gi, (_, dilation) in enumerate(ATTN_GROUPS):
        @pl.when(g == gi)
        def _(gi=gi, dilation=dilation):
            _attn_group(gi, dilation, q_ref, k_ref, v_ref, y_ref, acc_s, max_s, den_s, score_s, seq=seq,
                        tile=tile)


def _attention(q, k, v, *, tile):
    batch, _, seq, _ = q.shape
    spec = pl.BlockSpec((None, None, seq, GROUP_WIDTH), lambda b, g: (b, g, 0, 0))
    return pl.pallas_call(
        functools.partial(_attn_kernel, seq=seq, tile=tile),
        grid=(batch, N_GROUPS),
        in_specs=[spec, spec, spec],
        out_specs=pl.BlockSpec((None, seq, GROUP_WIDTH), lambda b, g: (b, 0, 0)),
        out_shape=jax.ShapeDtypeStruct((batch, seq, GROUP_WIDTH), BF16),
        scratch_shapes=[_halves_scratch(seq)] * 3
                       + [pltpu.VMEM((2, 2, HEADS_PER_GROUP * ATTN_BLOCK, 2 * ATTN_BLOCK), F32)],
        compiler_params=pltpu.CompilerParams(dimension_semantics=("arbitrary", "arbitrary"),
                                             vmem_limit_bytes=VMEM_LIMIT_BYTES),
        name="attention",
    )(q, k, v)


def _merge_kernel(x_ref, gain_ref, w_ref, bg_ref, d_ref, wmix_ref, pscale_ref, yc_ref, ya_ref,
                  wp_ref, wc_ref, wa_ref, wo_ref, gain2_ref, wff1_src, wff2_src,
                  xo_ref, h2_ref, wff1_dst, wff2_dst):
    _cast_rows(wff1_src, [wff1_dst])
    _cast_rows(wff2_src, [wff2_dst])
    dm = x_ref.shape[1]
    x = x_ref[...]
    h = _rms_norm(x, gain_ref[...]).astype(BF16)

    y_pool = jnp.concatenate(
        [_dot(d_ref[:, g * POOL_GROUP:(g + 1) * POOL_GROUP], wmix_ref[g]) for g in range(len(POOL_WINDOWS))],
        axis=-1)
    y_pool = (y_pool * pscale_ref[...]).astype(BF16)

    merged = None
    for j, (y, wj_ref) in enumerate(((y_pool, wp_ref), (yc_ref[...], wc_ref), (ya_ref[...], wa_ref))):
        gate_cols = slice(j * dm, (j + 1) * dm)
        gate = jax.nn.sigmoid(_dot(h, w_ref[:, gate_cols]) + bg_ref[:, gate_cols])
        term = gate * _dot(y, wj_ref[...])
        merged = term if merged is None else merged + term

    x_new = x + _dot(merged.astype(BF16), wo_ref[...])
    xo_ref[...] = x_new
    h2_ref[...] = _rms_norm(x_new, gain2_ref[...]).astype(BF16)


def _merge(x2d, gain, w_gates, b_gate, d, w_mix, pool_scale, yc, y_attn, w_pool_up, w_conv_out, w_attn_up, w_o,
           gain2, w_ff1, w_ff2, *, layer):
    t, dm = x2d.shape
    tm = TM_MERGE
    assert t % tm == 0
    steps = t // tm
    row = lambda width: pl.BlockSpec((tm, width), lambda i: (i, 0))
    cast_in, cast_out, cast_shape, _ = _cast_jobs_specs([(w_ff1, layer, ()), (w_ff2, layer, ())], steps)
    res = lambda w: _resident(w.shape)
    return pl.pallas_call(
        _merge_kernel,
        grid=(steps,),
        in_specs=[row(dm), _resident((1, dm)), res(w_gates), res(b_gate), row(POOL_WIDTH), res(w_mix),
                  _resident((1, POOL_WIDTH)), row(CONV_WIDTH), row(GROUP_WIDTH), res(w_pool_up),
                  res(w_conv_out), res(w_attn_up), res(w_o), _resident((1, dm)), *cast_in],
        out_specs=[row(dm), row(dm)] + cast_out,
        out_shape=[jax.ShapeDtypeStruct((t, dm), F32), jax.ShapeDtypeStruct((t, dm), BF16)] + cast_shape,
        compiler_params=pltpu.CompilerParams(dimension_semantics=("arbitrary",),
                                             vmem_limit_bytes=VMEM_LIMIT_BYTES),
        name="merge",
    )(x2d, gain, w_gates, b_gate, d, w_mix, pool_scale, yc, y_attn, w_pool_up, w_conv_out, w_attn_up, w_o, gain2,
      w_ff1, w_ff2)


def _ffn_kernel(x_ref, h2_ref, w1_ref, w2_ref, *rest, cast_layout):
    o_ref = rest[len(cast_layout)]
    _run_cast_jobs(rest[:len(cast_layout)], rest[len(cast_layout) + 1:], cast_layout)
    d_ff = w1_ref.shape[1]
    h2 = h2_ref[...]
    acc = None
    for c in range(d_ff // FF_CHUNK):
        cols = slice(c * FF_CHUNK, (c + 1) * FF_CHUNK)
        a = jnp.square(jnp.maximum(_dot(h2, w1_ref[:, cols]), 0.0)).astype(BF16)
        part = _dot(a, w2_ref[cols, :])
        acc = part if acc is None else acc + part
    o_ref[...] = x_ref[...] + acc


def _ffn(x2d, h2, w1, w2, cast_jobs=()):
    t, dm = x2d.shape
    tm = TM_FFN
    assert t % tm == 0 and w1.shape[1] % FF_CHUNK == 0
    steps = t // tm
    row = pl.BlockSpec((tm, dm), lambda i: (i, 0))
    cast_in, cast_out, cast_shape, cast_layout = _cast_jobs_specs(cast_jobs, steps)
    return pl.pallas_call(
        functools.partial(_ffn_kernel, cast_layout=cast_layout),
        grid=(steps,),
        in_specs=[row, row, _resident(w1.shape), _resident(w2.shape), *cast_in],
        out_specs=[row] + cast_out,
        out_shape=[jax.ShapeDtypeStruct((t, dm), F32)] + cast_shape,
        compiler_params=pltpu.CompilerParams(dimension_semantics=("arbitrary",),
                                             vmem_limit_bytes=VMEM_LIMIT_BYTES),
        name="ffn",
    )(x2d, h2, w1, w2, *(w for w, _, _ in cast_jobs))


def kernel(x, norm_mix, w_in, b_gate, pool_mix, pool_scale, conv_w, q_gain, k_gain, w_pool_up, w_conv_out,
           w_attn_up, w_o, norm_mlp, w_ff1, w_ff2):
    batch, seq, dm = x.shape
    depth = norm_mix.shape[0]
    assert w_in.shape[2] == OFF_GATE + N_BRANCH * dm
    head_id = jnp.arange(GROUP_WIDTH) // HEAD_DIM
    headmean = jnp.where(head_id[:, None] == head_id[None, :], 1.0 / HEAD_DIM, 0.0).astype(BF16)
    x2d = x.reshape(batch * seq, dm)
    pool_mix = pool_mix.astype(BF16)
    up_weights = (w_pool_up, w_conv_out, w_attn_up, w_o)
    layer_jobs = lambda l: [(w_in, l, (OFF_GATE,))] + [(w, l, ()) for w in up_weights]
    w_mixers, w_gates, *ups = _cast_weights(layer_jobs(0))
    for l in range(depth):
        gain = norm_mix[l].reshape(1, dm)
        qgain = (jnp.tile(q_gain[l], HEADS_PER_GROUP) * (LOG2_E * HEAD_DIM ** -0.5)).reshape(1, GROUP_WIDTH)
        kgain = jnp.tile(k_gain[l], HEADS_PER_GROUP).reshape(1, GROUP_WIDTH)
        d, yc, q, k, v = _in_proj(x2d, gain, w_mixers, qgain, kgain, conv_w[l], headmean, batch=batch, seq=seq)
        y_attn = _attention(q, k, v, tile=TM_IN).reshape(batch * seq, GROUP_WIDTH)
        x2d, h2, w1, w2 = _merge(x2d, gain, w_gates, b_gate[l].reshape(1, N_BRANCH * dm), d, pool_mix[l],
                                 pool_scale[l].reshape(1, POOL_WIDTH), yc, y_attn, *ups,
                                 norm_mlp[l].reshape(1, dm), w_ff1, w_ff2, layer=l)
        if l + 1 < depth:
            x2d, w_mixers, w_gates, *ups = _ffn(x2d, h2, w1, w2, layer_jobs(l + 1))
        else:
            x2d, = _ffn(x2d, h2, w1, w2)
    return x2d.reshape(batch, seq, dm)
```

```python
import functools

import jax
import jax.numpy as jnp
from jax import lax
from jax.experimental import pallas as pl
from jax.experimental.pallas import tpu as pltpu

F32 = jnp.float32
BF16 = jnp.bfloat16

POOL_WINDOWS = (2, 4, 8, 16)
POOL_GROUP = 128
POOL_WIDTH = POOL_GROUP * len(POOL_WINDOWS)
CONV_WIDTH = 512
CONV_K = 3
HEAD_DIM = 64
ATTN_GROUPS = ((128, 1), (512, 4), (2048, 16))
HEADS_PER_GROUP = 4
GROUP_WIDTH = HEADS_PER_GROUP * HEAD_DIM
N_GROUPS = len(ATTN_GROUPS)
ATTN_WIDTH = N_GROUPS * GROUP_WIDTH
ATTN_BLOCK = 128
LANES = 128
BF16_ROWS = 16
N_BRANCH = 3
EPS = 1e-6
MASK_VALUE = -1e30
LOG2_E = 1.4426950408889634

OFF_POOL = 0
OFF_CONV = OFF_POOL + POOL_WIDTH
OFF_Q = OFF_CONV + 3 * CONV_WIDTH
OFF_K = OFF_Q + ATTN_WIDTH
OFF_V = OFF_K + ATTN_WIDTH
OFF_GATE = OFF_V + ATTN_WIDTH

POOL_HALO = 16
CONV_HALO = 8

VMEM_LIMIT_BYTES = 56 * 1024 * 1024
TM_IN = 1024
TM_MERGE = 1024
TM_FFN = 1024
FF_CHUNK = 1024
CAST_STEPS = 8
ITEM_BLOCKS = 2
OUT_CHUNK = 512


def _resident(shape):
    return pl.BlockSpec(shape, lambda *_: (0,) * len(shape), pipeline_mode=pl.Buffered(1))


def _cast_specs(w, layer, steps, splits=()):
    _, rows, cols = w.shape
    assert rows % steps == 0 and (rows // steps) % BF16_ROWS == 0
    rb = rows // steps
    bounds = (0,) + tuple(splits) + (cols,)
    widths = [b - a for a, b in zip(bounds, bounds[1:])]
    in_spec = pl.BlockSpec((None, rb, cols), lambda i: (layer, i, 0))
    out_specs = [pl.BlockSpec((rb, wd), lambda i: (i, 0)) for wd in widths]
    out_shapes = [jax.ShapeDtypeStruct((rows, wd), BF16) for wd in widths]
    return in_spec, out_specs, out_shapes


def _cast_rows(src_ref, dst_refs):
    start = 0
    for dst in dst_refs:
        dst[...] = src_ref[:, start:start + dst.shape[1]].astype(BF16)
        start += dst.shape[1]


def _cast_jobs_specs(jobs, steps):
    in_specs, out_specs, out_shapes, layout = [], [], [], []
    for w, layer, splits in jobs:
        in_spec, o_specs, o_shapes = _cast_specs(w, layer, steps, splits)
        in_specs.append(in_spec)
        out_specs += o_specs
        out_shapes += o_shapes
        layout.append(len(o_specs))
    return in_specs, out_specs, out_shapes, tuple(layout)


def _run_cast_jobs(srcs, dsts, layout):
    dsts = list(dsts)
    for src, n_dst in zip(srcs, layout):
        _cast_rows(src, [dsts.pop(0) for _ in range(n_dst)])


def _cast_kernel(*refs, layout):
    _run_cast_jobs(refs[:len(layout)], refs[len(layout):], layout)


def _cast_weights(jobs):
    in_specs, out_specs, out_shapes, layout = _cast_jobs_specs(jobs, CAST_STEPS)
    return pl.pallas_call(
        functools.partial(_cast_kernel, layout=layout),
        grid=(CAST_STEPS,),
        in_specs=in_specs,
        out_specs=out_specs,
        out_shape=out_shapes,
        compiler_params=pltpu.CompilerParams(dimension_semantics=("arbitrary",),
                                             vmem_limit_bytes=VMEM_LIMIT_BYTES),
        name="cast_weights",
    )(*(w for w, _, _ in jobs))


def _rms_norm(x, gain):
    ms = jnp.mean(x * x, axis=-1, keepdims=True)
    return x * lax.rsqrt(ms + EPS) * gain


def _dot(a, b):
    return jnp.dot(a, b, preferred_element_type=F32)


def _halves_scratch(rows):
    return pltpu.VMEM((GROUP_WIDTH // LANES, rows, LANES), F32)


def _put_halves(ref, rows, val):
    for j in range(GROUP_WIDTH // LANES):
        ref[j, rows, :] = val[:, j * LANES:(j + 1) * LANES]


def _get_halves(ref, rows):
    return jnp.concatenate([ref[j, rows, :] for j in range(GROUP_WIDTH // LANES)], axis=-1)


def _in_proj_kernel(x_ref, gain_ref, w_ref, qgain_ref, kgain_ref, convw_ref, headmean_ref,
                    d_ref, yc_ref, q_ref, k_ref, v_ref, uhalo, chalo, zbuf, *, tm, tiles_per_seq):
    i = pl.program_id(0)

    @pl.when(i == 0)
    def _():
        uhalo[...] = jnp.zeros_like(uhalo)
        chalo[...] = jnp.zeros_like(chalo)

    tile_in_seq = i % tiles_per_seq
    carry = tile_in_seq != 0
    h = _rms_norm(x_ref[...], gain_ref[...]).astype(BF16)

    def emit_heads(z_all, slot, gain, o_ref):
        zs = [z_all[:, g * GROUP_WIDTH:(g + 1) * GROUP_WIDTH] for g in range(N_GROUPS)]
        if gain is not None:
            ms = _dot(jnp.concatenate([(z * z).astype(BF16) for z in zs], axis=0), headmean_ref[...])
            zs = [z * lax.rsqrt(ms[g * tm:(g + 1) * tm, :] + EPS) * gain[...] for g, z in enumerate(zs)]
        for g, (_, dilation) in enumerate(ATTN_GROUPS):
            if dilation == 1:
                o_ref[g] = zs[g].astype(BF16)
            else:
                buf = zbuf.at[(g - 1) * 3 + slot]
                _put_halves(buf, slice(None), zs[g])
                chunk = tm // dilation
                for r in range(dilation):
                    rows = pl.ds(r, chunk, stride=dilation)
                    o_ref[g, r * chunk:(r + 1) * chunk, :] = _get_halves(buf, rows).astype(BF16)

    zpc = _dot(h, w_ref[:, OFF_POOL:OFF_Q])

    u = zpc[:, OFF_POOL:OFF_POOL + POOL_WIDTH]
    ext = jnp.concatenate([jnp.where(carry, uhalo[...], 0.0), u], axis=0)
    uhalo[...] = u[tm - POOL_HALO:tm, :]
    pos = tile_in_seq * tm + lax.broadcasted_iota(jnp.int32, (tm, 1), 0)
    run = ext
    for g, w in enumerate(POOL_WINDOWS):
        run = run + pltpu.roll(run, w // 2, 0)
        cols = slice(g * POOL_GROUP, (g + 1) * POOL_GROUP)
        inv_count = 1.0 / jnp.minimum(pos + 1, w).astype(F32)
        d_ref[:, cols] = (run[POOL_HALO:, 0:POOL_GROUP] * inv_count - u[:, cols]).astype(BF16)
        run = run[:, POOL_GROUP:]

    zc = zpc[:, OFF_CONV:OFF_CONV + 3 * CONV_WIDTH]
    uc = zc[:, CONV_WIDTH:2 * CONV_WIDTH] * zc[:, 2 * CONV_WIDTH:3 * CONV_WIDTH]
    ext = jnp.concatenate([jnp.where(carry, chalo[...], 0.0), uc], axis=0)
    chalo[...] = uc[tm - CONV_HALO:tm, :]
    y = convw_ref[CONV_K - 1:CONV_K, :] * uc
    for j in range(CONV_K - 1):
        y = y + convw_ref[j:j + 1, :] * pltpu.roll(ext, CONV_K - 1 - j, 0)[CONV_HALO:, :]
    yc_ref[...] = (zc[:, 0:CONV_WIDTH] * y).astype(BF16)

    zqkv = _dot(h, w_ref[:, OFF_Q:OFF_GATE])
    for slot, (gain, o_ref) in enumerate(((qgain_ref, q_ref), (kgain_ref, k_ref), (None, v_ref))):
        emit_heads(zqkv[:, slot * ATTN_WIDTH:(slot + 1) * ATTN_WIDTH], slot, gain, o_ref)


def _in_proj(x2d, gain, w_mixers, qgain, kgain, convw, headmean, *, batch, seq):
    t, dm = x2d.shape
    assert w_mixers.shape == (dm, OFF_GATE)
    tm = TM_IN
    tiles_per_seq = seq // tm
    assert seq % tm == 0 and all(tm % (16 * d) == 0 for _, d in ATTN_GROUPS)
    row = lambda width: pl.BlockSpec((tm, width), lambda i: (i, 0))
    qkv_spec = pl.BlockSpec((None, N_GROUPS, tm, GROUP_WIDTH),
                            lambda i: (i // tiles_per_seq, 0, i % tiles_per_seq, 0))
    qkv_shape = jax.ShapeDtypeStruct((batch, N_GROUPS, seq, GROUP_WIDTH), BF16)
    n_regrouped = 3 * sum(1 for _, d in ATTN_GROUPS if d > 1)
    return pl.pallas_call(
        functools.partial(_in_proj_kernel, tm=tm, tiles_per_seq=tiles_per_seq),
        grid=(t // tm,),
        in_specs=[row(dm), _resident((1, dm)), _resident(w_mixers.shape),
                  _resident((1, GROUP_WIDTH)), _resident((1, GROUP_WIDTH)), _resident((CONV_K, CONV_WIDTH)),
                  _resident((GROUP_WIDTH, GROUP_WIDTH))],
        out_specs=[row(POOL_WIDTH), row(CONV_WIDTH), qkv_spec, qkv_spec, qkv_spec],
        out_shape=[jax.ShapeDtypeStruct((t, POOL_WIDTH), BF16), jax.ShapeDtypeStruct((t, CONV_WIDTH), BF16),
                   qkv_shape, qkv_shape, qkv_shape],
        scratch_shapes=[pltpu.VMEM((POOL_HALO, POOL_WIDTH), F32), pltpu.VMEM((CONV_HALO, CONV_WIDTH), F32),
                        pltpu.VMEM((n_regrouped, GROUP_WIDTH // LANES, tm, LANES), F32)],
        compiler_params=pltpu.CompilerParams(dimension_semantics=("arbitrary",),
                                             vmem_limit_bytes=VMEM_LIMIT_BYTES),
        name="in_proj",
    )(x2d, gain, w_mixers, qgain, kgain, convw, headmean)


def _in_head_masks():
    lane_head = lax.broadcasted_iota(jnp.int32, (1, GROUP_WIDTH), 1) // HEAD_DIM
    return [lane_head == hd for hd in range(HEADS_PER_GROUP)]


def _attn_scores(qb, kb):
    q_heads = jnp.concatenate([jnp.where(m, qb, jnp.zeros_like(qb)) for m in _in_head_masks()], axis=0)
    return lax.dot_general(q_heads, kb, (((1,), (1,)), ((), ())), preferred_element_type=F32)


def _attn_softmax(s_ref, first):
    blk = ATTN_BLOCK
    row = lax.broadcasted_iota(jnp.int32, (blk, blk), 0)
    col = lax.broadcasted_iota(jnp.int32, (blk, blk), 1)
    mask_value = MASK_VALUE * LOG2_E
    probs, row_max, row_den = [], [], []
    for hd in range(HEADS_PER_GROUP):
        rows = slice(hd * blk, (hd + 1) * blk)

        def masked():
            if first:
                return jnp.where(col <= row, s_ref[rows, 0:blk], mask_value)
            return jnp.concatenate([jnp.where(col >= row, s_ref[rows, 0:blk], mask_value),
                                    jnp.where(col <= row, s_ref[rows, blk:2 * blk], mask_value)], axis=1)

        m = jnp.max(masked(), axis=-1, keepdims=True)
        p = jnp.exp2(masked() - m)
        row_max.append(m)
        row_den.append(jnp.sum(p, axis=-1, keepdims=True))
        probs.append(p.astype(BF16))
    return jnp.concatenate(probs, axis=0), row_max, row_den


def _attn_values(probs, row_max, row_den, vb):
    blk = ATTN_BLOCK
    in_head = _in_head_masks()
    pv = _dot(probs, vb)
    shape = (blk, GROUP_WIDTH)
    acc, mx, den = pv[0:blk, :], jnp.broadcast_to(row_max[0], shape), jnp.broadcast_to(row_den[0], shape)
    for hd in range(1, HEADS_PER_GROUP):
        acc = jnp.where(in_head[hd], pv[hd * blk:(hd + 1) * blk, :], acc)
        mx = jnp.where(in_head[hd], row_max[hd], mx)
        den = jnp.where(in_head[hd], row_den[hd], den)
    return acc, mx, den


def _block_rows(ref, n, r, dilation, tile):
    chunk = tile // dilation
    piece = min(ATTN_BLOCK, chunk)
    parts = []
    for j in range(ATTN_BLOCK // piece):
        pos = n * ATTN_BLOCK + j * piece
        start = (pos // chunk) * tile + r * chunk + pos % chunk
        if not isinstance(start, int):
            start = pl.multiple_of(start, piece)
        parts.append(ref[pl.ds(start, piece), :])
    return parts[0] if len(parts) == 1 else jnp.concatenate(parts, axis=0)


def _attn_group(g, dilation, q_ref, k_ref, v_ref, y_ref, acc_s, max_s, den_s, score_s, *, seq, tile):
    blk = ATTN_BLOCK
    n_blocks = seq // dilation // blk
    assert n_blocks % 2 == 0

    def token_rows(n, r):
        start = n * (blk * dilation) + r
        return pl.ds(start, blk) if dilation == 1 else pl.ds(start, blk, stride=dilation)

    def load_state(rows):
        if g == 0:
            return None
        return _get_halves(acc_s, rows), _get_halves(max_s, rows), _get_halves(den_s, rows)

    def store_state(rows, state, new):
        acc, mx, den = new
        if state is not None:
            acc_old, mx_old, den_old = state
            mx_new = jnp.maximum(mx_old, mx)
            w_old, w_new = jnp.exp2(mx_old - mx_new), jnp.exp2(mx - mx_new)
            acc, den, mx = w_old * acc_old + w_new * acc, w_old * den_old + w_new * den, mx_new
        if g == N_GROUPS - 1:
            _put_halves(acc_s, rows, acc / den)
        else:
            _put_halves(acc_s, rows, acc)
            _put_halves(max_s, rows, mx)
            _put_halves(den_s, rows, den)

    def keys_values(ref, n, r, first):
        own = _block_rows(ref, n, r, dilation, tile)
        if first:
            return own
        return jnp.concatenate([_block_rows(ref, n - 1, r, dilation, tile), own], axis=0)

    width = 2 if n_blocks == 2 else ITEM_BLOCKS

    def pair(n, first):
        return tuple((n + i, first and i == 0) for i in range(width))

    def park_scores(slot, n, r, first):
        for i, (nb, fb) in enumerate(pair(n, first)):
            s = _attn_scores(_block_rows(q_ref, nb, r, dilation, tile), keys_values(k_ref, nb, r, fb))
            score_s[slot, i, :, 0:s.shape[1]] = s

    def finish(slot, n, r, first):
        blocks = pair(n, first)
        rows = [token_rows(nb, r) for nb, _ in blocks]
        states = [load_state(rw) for rw in rows]
        soft = [_attn_softmax(score_s.at[slot, i], fb) for i, (_, fb) in enumerate(blocks)]
        new = [_attn_values(*sm, keys_values(v_ref, nb, r, fb)) for sm, (nb, fb) in zip(soft, blocks)]
        for rw, st, nw in zip(rows, states, new):
            store_state(rw, st, nw)

    def run_items(n_items, item):
        assert n_items >= 2 and n_items % 2 == 0
        park_scores(0, *item(0, True))
        park_scores(1, *item(1, False))
        finish(0, *item(0, True))

        def double_step(t, c):
            i = 2 * t + 1
            park_scores(0, *item(i + 1, False))
            finish(1, *item(i, False))
            park_scores(1, *item(i + 2, False))
            finish(0, *item(i + 1, False))
            return c

        lax.fori_loop(0, (n_items - 2) // 2, double_step, 0)
        finish(1, *item(n_items - 1, False))

    if n_blocks == 2:
        run_items(dilation, lambda i, lead: (0, i, True))
    elif dilation == 1:
        run_items(n_blocks // width, lambda i, lead: (width * i, 0, lead))
    else:
        def per_residue(r, carry):
            run_items(n_blocks // width, lambda i, lead: (width * i, r, lead))
            return carry

        lax.fori_loop(0, dilation, per_residue, 0)

    if g == N_GROUPS - 1:
        def write_out(c, carry):
            rows = pl.ds(pl.multiple_of(c * OUT_CHUNK, OUT_CHUNK), OUT_CHUNK)
            y_ref[rows, :] = _get_halves(acc_s, rows).astype(y_ref.dtype)
            return carry

        lax.fori_loop(0, seq // OUT_CHUNK, write_out, 0)


def _attn_kernel(q_ref, k_ref, v_ref, y_ref, acc_s, max_s, den_s, score_s, *, seq, tile):
    g = pl.program_id(1)
    for gi, (_, dilation) in enumerate(ATTN_GROUPS):
        @pl.when(g == gi)
        def _(gi=gi, dilation=dilation):
            _attn_group(gi, dilation, q_ref, k_ref, v_ref, y_ref, acc_s, max_s, den_s, score_s, seq=seq,
                        tile=tile)


def _attention(q, k, v, *, tile):
    batch, _, seq, _ = q.shape
    spec = pl.BlockSpec((None, None, seq, GROUP_WIDTH), lambda b, g: (b, g, 0, 0))
    return pl.pallas_call(
        functools.partial(_attn_kernel, seq=seq, tile=tile),
        grid=(batch, N_GROUPS),
        in_specs=[spec, spec, spec],
        out_specs=pl.BlockSpec((None, seq, GROUP_WIDTH), lambda b, g: (b, 0, 0)),
        out_shape=jax.ShapeDtypeStruct((batch, seq, GROUP_WIDTH), BF16),
        scratch_shapes=[_halves_scratch(seq)] * 3
                       + [pltpu.VMEM((2, 2, HEADS_PER_GROUP * ATTN_BLOCK, 2 * ATTN_BLOCK), F32)],
        compiler_params=pltpu.CompilerParams(dimension_semantics=("arbitrary", "arbitrary"),
                                             vmem_limit_bytes=VMEM_LIMIT_BYTES),
        name="attention",
    )(q, k, v)


def _merge_kernel(x_ref, gain_ref, w_ref, bg_ref, d_ref, wmix_ref, pscale_ref, yc_ref, ya_ref,
                  wp_ref, wc_ref, wa_ref, wo_ref, gain2_ref, wff1_src, wff2_src,
                  xo_ref, h2_ref, wff1_dst, wff2_dst):
    _cast_rows(wff1_src, [wff1_dst])
    _cast_rows(wff2_src, [wff2_dst])
    dm = x_ref.shape[1]
    x = x_ref[...]
    h = _rms_norm(x, gain_ref[...]).astype(BF16)

    y_pool = jnp.concatenate(
        [_dot(d_ref[:, g * POOL_GROUP:(g + 1) * POOL_GROUP], wmix_ref[g]) for g in range(len(POOL_WINDOWS))],
        axis=-1)
    y_pool = (y_pool * pscale_ref[...]).astype(BF16)

    merged = None
    for j, (y, wj_ref) in enumerate(((y_pool, wp_ref), (yc_ref[...], wc_ref), (ya_ref[...], wa_ref))):
        gate_cols = slice(j * dm, (j + 1) * dm)
        gate = jax.nn.sigmoid(_dot(h, w_ref[:, gate_cols]) + bg_ref[:, gate_cols])
        term = gate * _dot(y, wj_ref[...])
        merged = term if merged is None else merged + term

    x_new = x + _dot(merged.astype(BF16), wo_ref[...])
    xo_ref[...] = x_new
    h2_ref[...] = _rms_norm(x_new, gain2_ref[...]).astype(BF16)


def _merge(x2d, gain, w_gates, b_gate, d, w_mix, pool_scale, yc, y_attn, w_pool_up, w_conv_out, w_attn_up, w_o,
           gain2, w_ff1, w_ff2, *, layer):
    t, dm = x2d.shape
    tm = TM_MERGE
    assert t % tm == 0
    steps = t // tm
    row = lambda width: pl.BlockSpec((tm, width), lambda i: (i, 0))
    cast_in, cast_out, cast_shape, _ = _cast_jobs_specs([(w_ff1, layer, ()), (w_ff2, layer, ())], steps)
    res = lambda w: _resident(w.shape)
    return pl.pallas_call(
        _merge_kernel,
        grid=(steps,),
        in_specs=[row(dm), _resident((1, dm)), res(w_gates), res(b_gate), row(POOL_WIDTH), res(w_mix),
                  _resident((1, POOL_WIDTH)), row(CONV_WIDTH), row(GROUP_WIDTH), res(w_pool_up),
                  res(w_conv_out), res(w_attn_up), res(w_o), _resident((1, dm)), *cast_in],
        out_specs=[row(dm), row(dm)] + cast_out,
        out_shape=[jax.ShapeDtypeStruct((t, dm), F32), jax.ShapeDtypeStruct((t, dm), BF16)] + cast_shape,
        compiler_params=pltpu.CompilerParams(dimension_semantics=("arbitrary",),
                                             vmem_limit_bytes=VMEM_LIMIT_BYTES),
        name="merge",
    )(x2d, gain, w_gates, b_gate, d, w_mix, pool_scale, yc, y_attn, w_pool_up, w_conv_out, w_attn_up, w_o, gain2,
      w_ff1, w_ff2)


def _ffn_kernel(x_ref, h2_ref, w1_ref, w2_ref, *rest, cast_layout):
    o_ref = rest[len(cast_layout)]
    _run_cast_jobs(rest[:len(cast_layout)], rest[len(cast_layout) + 1:], cast_layout)
    d_ff = w1_ref.shape[1]
    h2 = h2_ref[...]
    acc = None
    for c in range(d_ff // FF_CHUNK):
        cols = slice(c * FF_CHUNK, (c + 1) * FF_CHUNK)
        a = jnp.square(jnp.maximum(_dot(h2, w1_ref[:, cols]), 0.0)).astype(BF16)
        part = _dot(a, w2_ref[cols, :])
        acc = part if acc is None else acc + part
    o_ref[...] = x_ref[...] + acc


def _ffn(x2d, h2, w1, w2, cast_jobs=()):
    t, dm = x2d.shape
    tm = TM_FFN
    assert t % tm == 0 and w1.shape[1] % FF_CHUNK == 0
    steps = t // tm
    row = pl.BlockSpec((tm, dm), lambda i: (i, 0))
    cast_in, cast_out, cast_shape, cast_layout = _cast_jobs_specs(cast_jobs, steps)
    return pl.pallas_call(
        functools.partial(_ffn_kernel, cast_layout=cast_layout),
        grid=(steps,),
        in_specs=[row, row, _resident(w1.shape), _resident(w2.shape), *cast_in],
        out_specs=[row] + cast_out,
        out_shape=[jax.ShapeDtypeStruct((t, dm), F32)] + cast_shape,
        compiler_params=pltpu.CompilerParams(dimension_semantics=("arbitrary",),
                                             vmem_limit_bytes=VMEM_LIMIT_BYTES),
        name="ffn",
    )(x2d, h2, w1, w2, *(w for w, _, _ in cast_jobs))


def kernel(x, norm_mix, w_in, b_gate, pool_mix, pool_scale, conv_w, q_gain, k_gain, w_pool_up, w_conv_out,
           w_attn_up, w_o, norm_mlp, w_ff1, w_ff2):
    batch, seq, dm = x.shape
    depth = norm_mix.shape[0]
    assert w_in.shape[2] == OFF_GATE + N_BRANCH * dm
    head_id = jnp.arange(GROUP_WIDTH) // HEAD_DIM
    headmean = jnp.where(head_id[:, None] == head_id[None, :], 1.0 / HEAD_DIM, 0.0).astype(BF16)
    x2d = x.reshape(batch * seq, dm)
    pool_mix = pool_mix.astype(BF16)
    up_weights = (w_pool_up, w_conv_out, w_attn_up, w_o)
    layer_jobs = lambda l: [(w_in, l, (OFF_GATE,))] + [(w, l, ()) for w in up_weights]
    w_mixers, w_gates, *ups = _cast_weights(layer_jobs(0))
    for l in range(depth):
        gain = norm_mix[l].reshape(1, dm)
        qgain = (jnp.tile(q_gain[l], HEADS_PER_GROUP) * (LOG2_E * HEAD_DIM ** -0.5)).reshape(1, GROUP_WIDTH)
        kgain = jnp.tile(k_gain[l], HEADS_PER_GROUP).reshape(1, GROUP_WIDTH)
        d, yc, q, k, v = _in_proj(x2d, gain, w_mixers, qgain, kgain, conv_w[l], headmean, batch=batch, seq=seq)
        y_attn = _attention(q, k, v, tile=TM_IN).reshape(batch * seq, GROUP_WIDTH)
        x2d, h2, w1, w2 = _merge(x2d, gain, w_gates, b_gate[l].reshape(1, N_BRANCH * dm), d, pool_mix[l],
                                 pool_scale[l].reshape(1, POOL_WIDTH), yc, y_attn, *ups,
                                 norm_mlp[l].reshape(1, dm), w_ff1, w_ff2, layer=l)
        if l + 1 < depth:
            x2d, w_mixers, w_gates, *ups = _ffn(x2d, h2, w1, w2, layer_jobs(l + 1))
        else:
            x2d, = _ffn(x2d, h2, w1, w2)
    return x2d.reshape(batch, seq, dm)
```

```python
import functools

import jax
import jax.numpy as jnp
from jax import lax
from jax.experimental import pallas as pl
from jax.experimental.pallas import tpu as pltpu

F32 = jnp.float32
BF16 = jnp.bfloat16

POOL_WINDOWS = (2, 4, 8, 16)
POOL_GROUP = 128
POOL_WIDTH = POOL_GROUP * len(POOL_WINDOWS)
CONV_WIDTH = 512
CONV_K = 3
HEAD_DIM = 64
ATTN_GROUPS = ((128, 1), (512, 4), (2048, 16))
HEADS_PER_GROUP = 4
GROUP_WIDTH = HEADS_PER_GROUP * HEAD_DIM
N_GROUPS = len(ATTN_GROUPS)
ATTN_WIDTH = N_GROUPS * GROUP_WIDTH
ATTN_BLOCK = 128
LANES = 128
BF16_ROWS = 16
N_BRANCH = 3
EPS = 1e-6
MASK_VALUE = -1e30
LOG2_E = 1.4426950408889634

OFF_POOL = 0
OFF_CONV = OFF_POOL + POOL_WIDTH
OFF_Q = OFF_CONV + 3 * CONV_WIDTH
OFF_K = OFF_Q + ATTN_WIDTH
OFF_V = OFF_K + ATTN_WIDTH
OFF_GATE = OFF_V + ATTN_WIDTH

POOL_HALO = 16
CONV_HALO = 8

VMEM_LIMIT_BYTES = 56 * 1024 * 1024
TM_IN = 1024
TM_MERGE = 1024
TM_FFN = 1024
FF_CHUNK = 1024
CAST_STEPS = 8
ITEM_BLOCKS = 2
CAST_GROUPS = 2


def _resident(shape):
    return pl.BlockSpec(shape, lambda *_: (0,) * len(shape), pipeline_mode=pl.Buffered(1))


def _cast_specs(w, layer, steps, splits=()):
    _, rows, cols = w.shape
    assert rows % steps == 0 and (rows // steps) % BF16_ROWS == 0
    rb = rows // steps
    bounds = (0,) + tuple(splits) + (cols,)
    widths = [b - a for a, b in zip(bounds, bounds[1:])]
    in_spec = pl.BlockSpec((None, rb, cols), lambda i: (layer, i, 0))
    out_specs = [pl.BlockSpec((rb, wd), lambda i: (i, 0)) for wd in widths]
    out_shapes = [jax.ShapeDtypeStruct((rows, wd), BF16) for wd in widths]
    return in_spec, out_specs, out_shapes


def _cast_rows(src_ref, dst_refs):
    start = 0
    for dst in dst_refs:
        dst[...] = src_ref[:, start:start + dst.shape[1]].astype(BF16)
        start += dst.shape[1]


def _cast_jobs_specs(jobs, steps):
    in_specs, out_specs, out_shapes, layout = [], [], [], []
    for w, layer, splits in jobs:
        in_spec, o_specs, o_shapes = _cast_specs(w, layer, steps, splits)
        in_specs.append(in_spec)
        out_specs += o_specs
        out_shapes += o_shapes
        layout.append(len(o_specs))
    return in_specs, out_specs, out_shapes, tuple(layout)


def _run_cast_jobs(srcs, dsts, layout):
    dsts = list(dsts)
    for src, n_dst in zip(srcs, layout):
        _cast_rows(src, [dsts.pop(0) for _ in range(n_dst)])


def _cast_kernel(*refs, layout):
    _run_cast_jobs(refs[:len(layout)], refs[len(layout):], layout)


def _cast_weights(jobs):
    in_specs, out_specs, out_shapes, layout = _cast_jobs_specs(jobs, CAST_STEPS)
    return pl.pallas_call(
        functools.partial(_cast_kernel, layout=layout),
        grid=(CAST_STEPS,),
        in_specs=in_specs,
        out_specs=out_specs,
        out_shape=out_shapes,
        compiler_params=pltpu.CompilerParams(dimension_semantics=("arbitrary",),
                                             vmem_limit_bytes=VMEM_LIMIT_BYTES),
        name="cast_weights",
    )(*(w for w, _, _ in jobs))


def _rms_norm(x, gain):
    ms = jnp.mean(x * x, axis=-1, keepdims=True)
    return x * lax.rsqrt(ms + EPS) * gain


def _dot(a, b):
    return jnp.dot(a, b, preferred_element_type=F32)


def _halves_scratch(rows):
    return pltpu.VMEM((GROUP_WIDTH // LANES, rows, LANES), F32)


def _put_halves(ref, rows, val):
    for j in range(GROUP_WIDTH // LANES):
        ref[j, rows, :] = val[:, j * LANES:(j + 1) * LANES]


def _get_halves(ref, rows):
    return jnp.concatenate([ref[j, rows, :] for j in range(GROUP_WIDTH // LANES)], axis=-1)


def _in_proj_kernel(x_ref, gain_ref, w_ref, qgain_ref, kgain_ref, convw_ref, headmean_ref,
                    d_ref, yc_ref, q_ref, k_ref, v_ref, uhalo, chalo, zbuf, *, tm, tiles_per_seq):
    i = pl.program_id(0)

    @pl.when(i == 0)
    def _():
        uhalo[...] = jnp.zeros_like(uhalo)
        chalo[...] = jnp.zeros_like(chalo)

    tile_in_seq = i % tiles_per_seq
    carry = tile_in_seq != 0
    h = _rms_norm(x_ref[...], gain_ref[...]).astype(BF16)

    def emit_heads(z_all, slot, gain, o_ref):
        zs = [z_all[:, g * GROUP_WIDTH:(g + 1) * GROUP_WIDTH] for g in range(N_GROUPS)]
        if gain is not None:
            ms = _dot(jnp.concatenate([(z * z).astype(BF16) for z in zs], axis=0), headmean_ref[...])
            zs = [z * lax.rsqrt(ms[g * tm:(g + 1) * tm, :] + EPS) * gain[...] for g, z in enumerate(zs)]
        for g, (_, dilation) in enumerate(ATTN_GROUPS):
            if dilation == 1:
                o_ref[g] = zs[g].astype(BF16)
            else:
                buf = zbuf.at[(g - 1) * 3 + slot]
                _put_halves(buf, slice(None), zs[g])
                chunk = tm // dilation
                for r in range(dilation):
                    rows = pl.ds(r, chunk, stride=dilation)
                    o_ref[g, r * chunk:(r + 1) * chunk, :] = _get_halves(buf, rows).astype(BF16)

    zpc = _dot(h, w_ref[:, OFF_POOL:OFF_Q])

    u = zpc[:, OFF_POOL:OFF_POOL + POOL_WIDTH]
    ext = jnp.concatenate([jnp.where(carry, uhalo[...], 0.0), u], axis=0)
    uhalo[...] = u[tm - POOL_HALO:tm, :]
    pos = tile_in_seq * tm + lax.broadcasted_iota(jnp.int32, (tm, 1), 0)
    run = ext
    for g, w in enumerate(POOL_WINDOWS):
        run = run + pltpu.roll(run, w // 2, 0)
        cols = slice(g * POOL_GROUP, (g + 1) * POOL_GROUP)
        inv_count = 1.0 / jnp.minimum(pos + 1, w).astype(F32)
        d_ref[:, cols] = (run[POOL_HALO:, 0:POOL_GROUP] * inv_count - u[:, cols]).astype(BF16)
        run = run[:, POOL_GROUP:]

    zc = zpc[:, OFF_CONV:OFF_CONV + 3 * CONV_WIDTH]
    uc = zc[:, CONV_WIDTH:2 * CONV_WIDTH] * zc[:, 2 * CONV_WIDTH:3 * CONV_WIDTH]
    ext = jnp.concatenate([jnp.where(carry, chalo[...], 0.0), uc], axis=0)
    chalo[...] = uc[tm - CONV_HALO:tm, :]
    y = convw_ref[CONV_K - 1:CONV_K, :] * uc
    for j in range(CONV_K - 1):
        y = y + convw_ref[j:j + 1, :] * pltpu.roll(ext, CONV_K - 1 - j, 0)[CONV_HALO:, :]
    yc_ref[...] = (zc[:, 0:CONV_WIDTH] * y).astype(BF16)

    zqkv = _dot(h, w_ref[:, OFF_Q:OFF_GATE])
    for slot, (gain, o_ref) in enumerate(((qgain_ref, q_ref), (kgain_ref, k_ref), (None, v_ref))):
        emit_heads(zqkv[:, slot * ATTN_WIDTH:(slot + 1) * ATTN_WIDTH], slot, gain, o_ref)


def _in_proj(x2d, gain, w_mixers, qgain, kgain, convw, headmean, *, batch, seq):
    t, dm = x2d.shape
    assert w_mixers.shape == (dm, OFF_GATE)
    tm = TM_IN
    tiles_per_seq = seq // tm
    assert seq % tm == 0 and all(tm % (16 * d) == 0 for _, d in ATTN_GROUPS)
    row = lambda width: pl.BlockSpec((tm, width), lambda i: (i, 0))
    qkv_spec = pl.BlockSpec((None, N_GROUPS, tm, GROUP_WIDTH),
                            lambda i: (i // tiles_per_seq, 0, i % tiles_per_seq, 0))
    qkv_shape = jax.ShapeDtypeStruct((batch, N_GROUPS, seq, GROUP_WIDTH), BF16)
    n_regrouped = 3 * sum(1 for _, d in ATTN_GROUPS if d > 1)
    return pl.pallas_call(
        functools.partial(_in_proj_kernel, tm=tm, tiles_per_seq=tiles_per_seq),
        grid=(t // tm,),
        in_specs=[row(dm), _resident((1, dm)), _resident(w_mixers.shape),
                  _resident((1, GROUP_WIDTH)), _resident((1, GROUP_WIDTH)), _resident((CONV_K, CONV_WIDTH)),
                  _resident((GROUP_WIDTH, GROUP_WIDTH))],
        out_specs=[row(POOL_WIDTH), row(CONV_WIDTH), qkv_spec, qkv_spec, qkv_spec],
        out_shape=[jax.ShapeDtypeStruct((t, POOL_WIDTH), BF16), jax.ShapeDtypeStruct((t, CONV_WIDTH), BF16),
                   qkv_shape, qkv_shape, qkv_shape],
        scratch_shapes=[pltpu.VMEM((POOL_HALO, POOL_WIDTH), F32), pltpu.VMEM((CONV_HALO, CONV_WIDTH), F32),
                        pltpu.VMEM((n_regrouped, GROUP_WIDTH // LANES, tm, LANES), F32)],
        compiler_params=pltpu.CompilerParams(dimension_semantics=("arbitrary",),
                                             vmem_limit_bytes=VMEM_LIMIT_BYTES),
        name="in_proj",
    )(x2d, gain, w_mixers, qgain, kgain, convw, headmean)


def _in_head_masks():
    lane_head = lax.broadcasted_iota(jnp.int32, (1, GROUP_WIDTH), 1) // HEAD_DIM
    return [lane_head == hd for hd in range(HEADS_PER_GROUP)]


def _attn_scores(qb, kb):
    q_heads = jnp.concatenate([jnp.where(m, qb, jnp.zeros_like(qb)) for m in _in_head_masks()], axis=0)
    return lax.dot_general(q_heads, kb, (((1,), (1,)), ((), ())), preferred_element_type=F32)


def _attn_softmax(s_ref, first):
    blk = ATTN_BLOCK
    row = lax.broadcasted_iota(jnp.int32, (blk, blk), 0)
    col = lax.broadcasted_iota(jnp.int32, (blk, blk), 1)
    mask_value = MASK_VALUE * LOG2_E
    probs, row_max, row_den = [], [], []
    for hd in range(HEADS_PER_GROUP):
        rows = slice(hd * blk, (hd + 1) * blk)

        def masked():
            if first:
                return jnp.where(col <= row, s_ref[rows, 0:blk], mask_value)
            return jnp.concatenate([jnp.where(col >= row, s_ref[rows, 0:blk], mask_value),
                                    jnp.where(col <= row, s_ref[rows, blk:2 * blk], mask_value)], axis=1)

        m = jnp.max(masked(), axis=-1, keepdims=True)
        p = jnp.exp2(masked() - m)
        row_max.append(m)
        row_den.append(jnp.sum(p, axis=-1, keepdims=True))
        probs.append(p.astype(BF16))
    return jnp.concatenate(probs, axis=0), row_max, row_den


def _attn_values(probs, row_max, row_den, vb):
    blk = ATTN_BLOCK
    in_head = _in_head_masks()
    lane = lax.broadcasted_iota(jnp.int32, (1, LANES), 1)
    pv = _dot(probs, vb)
    out, lse = None, None
    for hd in range(HEADS_PER_GROUP):
        o_h = pv[hd * blk:(hd + 1) * blk, :] * (1.0 / row_den[hd])
        l_h = row_max[hd] + jnp.log2(row_den[hd])
        out = o_h if out is None else jnp.where(in_head[hd], o_h, out)
        lse = jnp.broadcast_to(l_h, (blk, LANES)) if lse is None else jnp.where(lane == hd, l_h, lse)
    return out, lse


def _block_row_slices(n, r, dilation, tile):
    chunk = tile // dilation
    piece = min(ATTN_BLOCK, chunk)
    slices = []
    for j in range(ATTN_BLOCK // piece):
        pos = n * ATTN_BLOCK + j * piece
        start = (pos // chunk) * tile + r * chunk + pos % chunk
        if not isinstance(start, int):
            start = pl.multiple_of(start, piece)
        slices.append(pl.ds(start, piece))
    return slices


def _block_rows(ref, n, r, dilation, tile):
    parts = [ref[rows, :] for rows in _block_row_slices(n, r, dilation, tile)]
    return parts[0] if len(parts) == 1 else jnp.concatenate(parts, axis=0)


def _store_block_rows(ref, n, r, dilation, tile, val):
    start = 0
    for rows in _block_row_slices(n, r, dilation, tile):
        ref[rows, :] = val[start:start + rows.size, :]
        start += rows.size


def _attn_group(dilation, q_ref, k_ref, v_ref, o_ref, lse_ref, score_s, *, seq, tile):
    blk = ATTN_BLOCK
    n_blocks = seq // dilation // blk
    assert n_blocks % 2 == 0

    def keys_values(ref, n, r, first):
        own = _block_rows(ref, n, r, dilation, tile)
        if first:
            return own
        return jnp.concatenate([_block_rows(ref, n - 1, r, dilation, tile), own], axis=0)

    width = 2 if n_blocks == 2 else ITEM_BLOCKS

    def pair(n, first):
        return tuple((n + i, first and i == 0) for i in range(width))

    def park_scores(slot, n, r, first):
        for i, (nb, fb) in enumerate(pair(n, first)):
            s = _attn_scores(_block_rows(q_ref, nb, r, dilation, tile), keys_values(k_ref, nb, r, fb))
            score_s[slot, i, :, 0:s.shape[1]] = s

    def finish(slot, n, r, first):
        blocks = pair(n, first)
        soft = [_attn_softmax(score_s.at[slot, i], fb) for i, (_, fb) in enumerate(blocks)]
        new = [_attn_values(*sm, keys_values(v_ref, nb, r, fb)) for sm, (nb, fb) in zip(soft, blocks)]
        for (nb, _), (out, lse) in zip(blocks, new):
            _store_block_rows(o_ref, nb, r, dilation, tile, out.astype(o_ref.dtype))
            _store_block_rows(lse_ref, nb, r, dilation, tile, lse)

    def run_items(n_items, item):
        assert n_items >= 2 and n_items % 2 == 0
        park_scores(0, *item(0, True))
        park_scores(1, *item(1, False))
        finish(0, *item(0, True))

        def double_step(t, c):
            i = 2 * t + 1
            park_scores(0, *item(i + 1, False))
            finish(1, *item(i, False))
            park_scores(1, *item(i + 2, False))
            finish(0, *item(i + 1, False))
            return c

        lax.fori_loop(0, (n_items - 2) // 2, double_step, 0)
        finish(1, *item(n_items - 1, False))

    if n_blocks == 2:
        run_items(dilation, lambda i, lead: (0, i, True))
    elif dilation == 1:
        run_items(n_blocks // width, lambda i, lead: (width * i, 0, lead))
    else:
        def per_residue(r, carry):
            run_items(n_blocks // width, lambda i, lead: (width * i, r, lead))
            return carry

        lax.fori_loop(0, dilation, per_residue, 0)


def _attn_kernel(q_ref, k_ref, v_ref, wff1_src, wff2_src, o_ref, lse_ref, wff1_dst, wff2_dst, score_s, *,
                 seq, tile):
    g = pl.program_id(1)

    @pl.when(g < CAST_GROUPS)
    def _():
        _cast_rows(wff1_src, [wff1_dst])
        _cast_rows(wff2_src, [wff2_dst])

    for gi, (_, dilation) in enumerate(ATTN_GROUPS):
        @pl.when(g == gi)
        def _(dilation=dilation):
            _attn_group(dilation, q_ref, k_ref, v_ref, o_ref, lse_ref, score_s, seq=seq, tile=tile)


def _attention(q, k, v, w_ff1, w_ff2, *, layer, tile):
    batch, _, seq, _ = q.shape
    spec = lambda width: pl.BlockSpec((None, None, seq, width), lambda b, g: (b, g, 0, 0))
    cast_step = lambda b, g: b * CAST_GROUPS + jnp.minimum(g, CAST_GROUPS - 1)
    cast_in, cast_out, cast_shape = [], [], []
    for w in (w_ff1, w_ff2):
        _, rows, cols = w.shape
        rb = rows // (batch * CAST_GROUPS)
        assert rows % (batch * CAST_GROUPS) == 0 and rb % BF16_ROWS == 0
        cast_in.append(pl.BlockSpec((None, rb, cols), lambda b, g: (layer, cast_step(b, g), 0)))
        cast_out.append(pl.BlockSpec((rb, cols), lambda b, g: (cast_step(b, g), 0)))
        cast_shape.append(jax.ShapeDtypeStruct((rows, cols), BF16))
    return pl.pallas_call(
        functools.partial(_attn_kernel, seq=seq, tile=tile),
        grid=(batch, N_GROUPS),
        in_specs=[spec(GROUP_WIDTH)] * 3 + cast_in,
        out_specs=[spec(GROUP_WIDTH), spec(LANES)] + cast_out,
        out_shape=[jax.ShapeDtypeStruct((batch, N_GROUPS, seq, GROUP_WIDTH), BF16),
                   jax.ShapeDtypeStruct((batch, N_GROUPS, seq, LANES), F32)] + cast_shape,
        scratch_shapes=[pltpu.VMEM((2, 2, HEADS_PER_GROUP * ATTN_BLOCK, 2 * ATTN_BLOCK), F32)],
        compiler_params=pltpu.CompilerParams(dimension_semantics=("arbitrary", "arbitrary"),
                                             vmem_limit_bytes=VMEM_LIMIT_BYTES),
        name="attention",
    )(q, k, v, w_ff1, w_ff2)


def _mix_attention_groups(o_ref, lse_ref, tok_o, tok_l):
    tm = o_ref.shape[1]
    outs, lses = [], []
    for g, (_, dilation) in enumerate(ATTN_GROUPS):
        o, lse = o_ref[g].astype(F32), lse_ref[g]
        if dilation > 1:
            slot = sum(1 for _, d in ATTN_GROUPS[:g] if d > 1)
            chunk = tm // dilation
            for r in range(dilation):
                rows, src = pl.ds(r, chunk, stride=dilation), slice(r * chunk, (r + 1) * chunk)
                _put_halves(tok_o.at[slot], rows, o[src, :])
                tok_l[slot, rows, :] = lse[src, :]
            o, lse = _get_halves(tok_o.at[slot], slice(None)), tok_l[slot]
        outs.append(o)
        lses.append(lse)
    lse_max = functools.reduce(jnp.maximum, lses)
    weights = [jnp.exp2(lse - lse_max) for lse in lses]
    inv_den = 1.0 / functools.reduce(jnp.add, weights)
    lane = lax.broadcasted_iota(jnp.int32, (1, LANES), 1)
    in_head = _in_head_masks()
    mixed = None
    for o, w in zip(outs, weights):
        w = w * inv_den
        w_lanes = None
        for hd in range(HEADS_PER_GROUP):
            col = jnp.sum(jnp.where(lane == hd, w, 0.0), axis=-1, keepdims=True)
            w_lanes = jnp.broadcast_to(col, o.shape) if w_lanes is None else jnp.where(in_head[hd], col, w_lanes)
        mixed = w_lanes * o if mixed is None else mixed + w_lanes * o
    return mixed


def _merge_kernel(x_ref, gain_ref, w_ref, bg_ref, d_ref, wmix_ref, pscale_ref, yc_ref, o_ref, lse_ref,
                  wp_ref, wc_ref, wa_ref, wo_ref, gain2_ref, xo_ref, h2_ref, tok_o, tok_l):
    dm = x_ref.shape[1]
    x = x_ref[...]
    h = _rms_norm(x, gain_ref[...]).astype(BF16)
    y_attn = _mix_attention_groups(o_ref, lse_ref, tok_o, tok_l).astype(BF16)

    y_pool = jnp.concatenate(
        [_dot(d_ref[:, g * POOL_GROUP:(g + 1) * POOL_GROUP], wmix_ref[g]) for g in range(len(POOL_WINDOWS))],
        axis=-1)
    y_pool = (y_pool * pscale_ref[...]).astype(BF16)

    merged = None
    for j, (y, wj_ref) in enumerate(((y_pool, wp_ref), (yc_ref[...], wc_ref), (y_attn, wa_ref))):
        gate_cols = slice(j * dm, (j + 1) * dm)
        gate = jax.nn.sigmoid(_dot(h, w_ref[:, gate_cols]) + bg_ref[:, gate_cols])
        term = gate * _dot(y, wj_ref[...])
        merged = term if merged is None else merged + term

    x_new = x + _dot(merged.astype(BF16), wo_ref[...])
    xo_ref[...] = x_new
    h2_ref[...] = _rms_norm(x_new, gain2_ref[...]).astype(BF16)


def _merge(x2d, gain, w_gates, b_gate, d, w_mix, pool_scale, yc, attn_o, attn_lse, w_pool_up, w_conv_out,
           w_attn_up, w_o, gain2):
    t, dm = x2d.shape
    tm = TM_MERGE
    seq = attn_o.shape[2]
    assert t % tm == 0 and seq % tm == 0 and TM_MERGE == TM_IN
    tiles_per_seq = seq // tm
    row = lambda width: pl.BlockSpec((tm, width), lambda i: (i, 0))
    groups = lambda width: pl.BlockSpec((None, N_GROUPS, tm, width),
                                        lambda i: (i // tiles_per_seq, 0, i % tiles_per_seq, 0))
    res = lambda w: _resident(w.shape)
    n_dilated = sum(1 for _, dilation in ATTN_GROUPS if dilation > 1)
    return pl.pallas_call(
        _merge_kernel,
        grid=(t // tm,),
        in_specs=[row(dm), _resident((1, dm)), res(w_gates), res(b_gate), row(POOL_WIDTH), res(w_mix),
                  _resident((1, POOL_WIDTH)), row(CONV_WIDTH), groups(GROUP_WIDTH), groups(LANES),
                  res(w_pool_up), res(w_conv_out), res(w_attn_up), res(w_o), _resident((1, dm))],
        out_specs=[row(dm), row(dm)],
        out_shape=[jax.ShapeDtypeStruct((t, dm), F32), jax.ShapeDtypeStruct((t, dm), BF16)],
        scratch_shapes=[pltpu.VMEM((n_dilated, GROUP_WIDTH // LANES, tm, LANES), F32),
                        pltpu.VMEM((n_dilated, tm, LANES), F32)],
        compiler_params=pltpu.CompilerParams(dimension_semantics=("arbitrary",),
                                             vmem_limit_bytes=VMEM_LIMIT_BYTES),
        name="merge",
    )(x2d, gain, w_gates, b_gate, d, w_mix, pool_scale, yc, attn_o, attn_lse, w_pool_up, w_conv_out, w_attn_up,
      w_o, gain2)


def _ffn_kernel(x_ref, h2_ref, w1_ref, w2_ref, *rest, cast_layout):
    o_ref = rest[len(cast_layout)]
    _run_cast_jobs(rest[:len(cast_layout)], rest[len(cast_layout) + 1:], cast_layout)
    d_ff = w1_ref.shape[1]
    h2 = h2_ref[...]
    acc = None
    for c in range(d_ff // FF_CHUNK):
        cols = slice(c * FF_CHUNK, (c + 1) * FF_CHUNK)
        a = jnp.square(jnp.maximum(_dot(h2, w1_ref[:, cols]), 0.0)).astype(BF16)
        part = _dot(a, w2_ref[cols, :])
        acc = part if acc is None else acc + part
    o_ref[...] = x_ref[...] + acc


def _ffn(x2d, h2, w1, w2, cast_jobs=()):
    t, dm = x2d.shape
    tm = TM_FFN
    assert t % tm == 0 and w1.shape[1] % FF_CHUNK == 0
    steps = t // tm
    row = pl.BlockSpec((tm, dm), lambda i: (i, 0))
    cast_in, cast_out, cast_shape, cast_layout = _cast_jobs_specs(cast_jobs, steps)
    return pl.pallas_call(
        functools.partial(_ffn_kernel, cast_layout=cast_layout),
        grid=(steps,),
        in_specs=[row, row, _resident(w1.shape), _resident(w2.shape), *cast_in],
        out_specs=[row] + cast_out,
        out_shape=[jax.ShapeDtypeStruct((t, dm), F32)] + cast_shape,
        compiler_params=pltpu.CompilerParams(dimension_semantics=("arbitrary",),
                                             vmem_limit_bytes=VMEM_LIMIT_BYTES),
        name="ffn",
    )(x2d, h2, w1, w2, *(w for w, _, _ in cast_jobs))


def kernel(x, norm_mix, w_in, b_gate, pool_mix, pool_scale, conv_w, q_gain, k_gain, w_pool_up, w_conv_out,
           w_attn_up, w_o, norm_mlp, w_ff1, w_ff2):
    batch, seq, dm = x.shape
    depth = norm_mix.shape[0]
    assert w_in.shape[2] == OFF_GATE + N_BRANCH * dm
    head_id = jnp.arange(GROUP_WIDTH) // HEAD_DIM
    headmean = jnp.where(head_id[:, None] == head_id[None, :], 1.0 / HEAD_DIM, 0.0).astype(BF16)
    x2d = x.reshape(batch * seq, dm)
    pool_mix = pool_mix.astype(BF16)
    up_weights = (w_pool_up, w_conv_out, w_attn_up, w_o)
    layer_jobs = lambda l: [(w_in, l, (OFF_GATE,))] + [(w, l, ()) for w in up_weights]
    w_mixers, w_gates, *ups = _cast_weights(layer_jobs(0))
    for l in range(depth):
        gain = norm_mix[l].reshape(1, dm)
        qgain = (jnp.tile(q_gain[l], HEADS_PER_GROUP) * (LOG2_E * HEAD_DIM ** -0.5)).reshape(1, GROUP_WIDTH)
        kgain = jnp.tile(k_gain[l], HEADS_PER_GROUP).reshape(1, GROUP_WIDTH)
        d, yc, q, k, v = _in_proj(x2d, gain, w_mixers, qgain, kgain, conv_w[l], headmean, batch=batch, seq=seq)
        attn_o, attn_lse, w1, w2 = _attention(q, k, v, w_ff1, w_ff2, layer=l, tile=TM_IN)
        x2d, h2 = _merge(x2d, gain, w_gates, b_gate[l].reshape(1, N_BRANCH * dm), d, pool_mix[l],
                         pool_scale[l].reshape(1, POOL_WIDTH), yc, attn_o, attn_lse, *ups,
                         norm_mlp[l].reshape(1, dm))
        if l + 1 < depth:
            x2d, w_mixers, w_gates, *ups = _ffn(x2d, h2, w1, w2, layer_jobs(l + 1))
        else:
            x2d, = _ffn(x2d, h2, w1, w2)
    return x2d.reshape(batch, seq, dm)
```

```python
import functools

import jax
import jax.numpy as jnp
from jax import lax
from jax.experimental import pallas as pl
from jax.experimental.pallas import tpu as pltpu

F32 = jnp.float32
BF16 = jnp.bfloat16

POOL_WINDOWS = (2, 4, 8, 16)
POOL_GROUP = 128
POOL_WIDTH = POOL_GROUP * len(POOL_WINDOWS)
CONV_WIDTH = 512
CONV_K = 3
HEAD_DIM = 64
ATTN_GROUPS = ((128, 1), (512, 4), (2048, 16))
HEADS_PER_GROUP = 4
GROUP_WIDTH = HEADS_PER_GROUP * HEAD_DIM
N_GROUPS = len(ATTN_GROUPS)
ATTN_WIDTH = N_GROUPS * GROUP_WIDTH
ATTN_BLOCK = 128
LANES = 128
BF16_ROWS = 16
N_BRANCH = 3
EPS = 1e-6
MASK_VALUE = -1e30
LOG2_E = 1.4426950408889634

OFF_POOL = 0
OFF_CONV = OFF_POOL + POOL_WIDTH
OFF_Q = OFF_CONV + 3 * CONV_WIDTH
OFF_K = OFF_Q + ATTN_WIDTH
OFF_V = OFF_K + ATTN_WIDTH
OFF_GATE = OFF_V + ATTN_WIDTH

POOL_HALO = 16
CONV_HALO = 8

VMEM_LIMIT_BYTES = 56 * 1024 * 1024
TM_IN = 1024
TM_MERGE = 1024
TM_FFN = 1024
FF_CHUNK = 1024
CAST_STEPS = 8
ITEM_BLOCKS = 2
CAST_GROUPS = 2


def _resident(shape):
    return pl.BlockSpec(shape, lambda *_: (0,) * len(shape), pipeline_mode=pl.Buffered(1))


def _cast_specs(w, layer, steps, splits=()):
    _, rows, cols = w.shape
    assert rows % steps == 0 and (rows // steps) % BF16_ROWS == 0
    rb = rows // steps
    bounds = (0,) + tuple(splits) + (cols,)
    widths = [b - a for a, b in zip(bounds, bounds[1:])]
    in_spec = pl.BlockSpec((None, rb, cols), lambda i: (layer, i, 0))
    out_specs = [pl.BlockSpec((rb, wd), lambda i: (i, 0)) for wd in widths]
    out_shapes = [jax.ShapeDtypeStruct((rows, wd), BF16) for wd in widths]
    return in_spec, out_specs, out_shapes


def _cast_rows(src_ref, dst_refs):
    start = 0
    for dst in dst_refs:
        dst[...] = src_ref[:, start:start + dst.shape[1]].astype(BF16)
        start += dst.shape[1]


def _cast_jobs_specs(jobs, steps):
    in_specs, out_specs, out_shapes, layout = [], [], [], []
    for w, layer, splits in jobs:
        in_spec, o_specs, o_shapes = _cast_specs(w, layer, steps, splits)
        in_specs.append(in_spec)
        out_specs += o_specs
        out_shapes += o_shapes
        layout.append(len(o_specs))
    return in_specs, out_specs, out_shapes, tuple(layout)


def _run_cast_jobs(srcs, dsts, layout):
    dsts = list(dsts)
    for src, n_dst in zip(srcs, layout):
        _cast_rows(src, [dsts.pop(0) for _ in range(n_dst)])


def _cast_kernel(*refs, layout):
    _run_cast_jobs(refs[:len(layout)], refs[len(layout):], layout)


def _cast_weights(jobs):
    in_specs, out_specs, out_shapes, layout = _cast_jobs_specs(jobs, CAST_STEPS)
    return pl.pallas_call(
        functools.partial(_cast_kernel, layout=layout),
        grid=(CAST_STEPS,),
        in_specs=in_specs,
        out_specs=out_specs,
        out_shape=out_shapes,
        compiler_params=pltpu.CompilerParams(dimension_semantics=("arbitrary",),
                                             vmem_limit_bytes=VMEM_LIMIT_BYTES),
        name="cast_weights",
    )(*(w for w, _, _ in jobs))


def _rms_norm(x, gain):
    ms = jnp.mean(x * x, axis=-1, keepdims=True)
    return x * lax.rsqrt(ms + EPS) * gain


def _dot(a, b):
    return jnp.dot(a, b, preferred_element_type=F32)


def _halves_scratch(rows):
    return pltpu.VMEM((GROUP_WIDTH // LANES, rows, LANES), F32)


def _put_halves(ref, rows, val):
    for j in range(GROUP_WIDTH // LANES):
        ref[j, rows, :] = val[:, j * LANES:(j + 1) * LANES]


def _get_halves(ref, rows):
    return jnp.concatenate([ref[j, rows, :] for j in range(GROUP_WIDTH // LANES)], axis=-1)


def _in_proj_kernel(x_ref, gain_ref, w_ref, qgain_ref, kgain_ref, convw_ref, headmean_ref,
                    d_ref, yc_ref, q_ref, k_ref, v_ref, uhalo, chalo, zbuf, *, tm, tiles_per_seq):
    i = pl.program_id(0)

    @pl.when(i == 0)
    def _():
        uhalo[...] = jnp.zeros_like(uhalo)
        chalo[...] = jnp.zeros_like(chalo)

    tile_in_seq = i % tiles_per_seq
    carry = tile_in_seq != 0
    h = _rms_norm(x_ref[...], gain_ref[...]).astype(BF16)

    def emit_heads(z_all, slot, gain, o_ref):
        zs = [z_all[:, g * GROUP_WIDTH:(g + 1) * GROUP_WIDTH] for g in range(N_GROUPS)]
        if gain is not None:
            ms = _dot(jnp.concatenate([(z * z).astype(BF16) for z in zs], axis=0), headmean_ref[...])
            zs = [z * lax.rsqrt(ms[g * tm:(g + 1) * tm, :] + EPS) * gain[...] for g, z in enumerate(zs)]
        for g, (_, dilation) in enumerate(ATTN_GROUPS):
            if dilation == 1:
                o_ref[g] = zs[g].astype(BF16)
            else:
                buf = zbuf.at[(g - 1) * 3 + slot]
                _put_halves(buf, slice(None), zs[g])
                chunk = tm // dilation
                for r in range(dilation):
                    rows = pl.ds(r, chunk, stride=dilation)
                    o_ref[g, r * chunk:(r + 1) * chunk, :] = _get_halves(buf, rows).astype(BF16)

    zpc = _dot(h, w_ref[:, OFF_POOL:OFF_Q])

    u = zpc[:, OFF_POOL:OFF_POOL + POOL_WIDTH]
    ext = jnp.concatenate([jnp.where(carry, uhalo[...], 0.0), u], axis=0)
    uhalo[...] = u[tm - POOL_HALO:tm, :]
    pos = tile_in_seq * tm + lax.broadcasted_iota(jnp.int32, (tm, 1), 0)
    run = ext
    for g, w in enumerate(POOL_WINDOWS):
        run = run + pltpu.roll(run, w // 2, 0)
        cols = slice(g * POOL_GROUP, (g + 1) * POOL_GROUP)
        inv_count = 1.0 / jnp.minimum(pos + 1, w).astype(F32)
        d_ref[:, cols] = (run[POOL_HALO:, 0:POOL_GROUP] * inv_count - u[:, cols]).astype(BF16)
        run = run[:, POOL_GROUP:]

    zc = zpc[:, OFF_CONV:OFF_CONV + 3 * CONV_WIDTH]
    uc = zc[:, CONV_WIDTH:2 * CONV_WIDTH] * zc[:, 2 * CONV_WIDTH:3 * CONV_WIDTH]
    ext = jnp.concatenate([jnp.where(carry, chalo[...], 0.0), uc], axis=0)
    chalo[...] = uc[tm - CONV_HALO:tm, :]
    y = convw_ref[CONV_K - 1:CONV_K, :] * uc
    for j in range(CONV_K - 1):
        y = y + convw_ref[j:j + 1, :] * pltpu.roll(ext, CONV_K - 1 - j, 0)[CONV_HALO:, :]
    yc_ref[...] = (zc[:, 0:CONV_WIDTH] * y).astype(BF16)

    zqkv = _dot(h, w_ref[:, OFF_Q:OFF_GATE])
    for slot, (gain, o_ref) in enumerate(((qgain_ref, q_ref), (kgain_ref, k_ref), (None, v_ref))):
        emit_heads(zqkv[:, slot * ATTN_WIDTH:(slot + 1) * ATTN_WIDTH], slot, gain, o_ref)


def _in_proj(x2d, gain, w_mixers, qgain, kgain, convw, headmean, *, batch, seq):
    t, dm = x2d.shape
    assert w_mixers.shape == (dm, OFF_GATE)
    tm = TM_IN
    tiles_per_seq = seq // tm
    assert seq % tm == 0 and all(tm % (16 * d) == 0 for _, d in ATTN_GROUPS)
    row = lambda width: pl.BlockSpec((tm, width), lambda i: (i, 0))
    qkv_spec = pl.BlockSpec((None, N_GROUPS, tm, GROUP_WIDTH),
                            lambda i: (i // tiles_per_seq, 0, i % tiles_per_seq, 0))
    qkv_shape = jax.ShapeDtypeStruct((batch, N_GROUPS, seq, GROUP_WIDTH), BF16)
    n_regrouped = 3 * sum(1 for _, d in ATTN_GROUPS if d > 1)
    return pl.pallas_call(
        functools.partial(_in_proj_kernel, tm=tm, tiles_per_seq=tiles_per_seq),
        grid=(t // tm,),
        in_specs=[row(dm), _resident((1, dm)), _resident(w_mixers.shape),
                  _resident((1, GROUP_WIDTH)), _resident((1, GROUP_WIDTH)), _resident((CONV_K, CONV_WIDTH)),
                  _resident((GROUP_WIDTH, GROUP_WIDTH))],
        out_specs=[row(POOL_WIDTH), row(CONV_WIDTH), qkv_spec, qkv_spec, qkv_spec],
        out_shape=[jax.ShapeDtypeStruct((t, POOL_WIDTH), BF16), jax.ShapeDtypeStruct((t, CONV_WIDTH), BF16),
                   qkv_shape, qkv_shape, qkv_shape],
        scratch_shapes=[pltpu.VMEM((POOL_HALO, POOL_WIDTH), F32), pltpu.VMEM((CONV_HALO, CONV_WIDTH), F32),
                        pltpu.VMEM((n_regrouped, GROUP_WIDTH // LANES, tm, LANES), F32)],
        compiler_params=pltpu.CompilerParams(dimension_semantics=("arbitrary",),
                                             vmem_limit_bytes=VMEM_LIMIT_BYTES),
        name="in_proj",
    )(x2d, gain, w_mixers, qgain, kgain, convw, headmean)


def _in_head_masks():
    lane_head = lax.broadcasted_iota(jnp.int32, (1, GROUP_WIDTH), 1) // HEAD_DIM
    return [lane_head == hd for hd in range(HEADS_PER_GROUP)]


def _attn_scores(qb, kb):
    q_heads = jnp.concatenate([jnp.where(m, qb, jnp.zeros_like(qb)) for m in _in_head_masks()], axis=0)
    return lax.dot_general(q_heads, kb, (((1,), (1,)), ((), ())), preferred_element_type=F32)


def _attn_softmax(s_ref, first):
    blk = ATTN_BLOCK
    row = lax.broadcasted_iota(jnp.int32, (blk, blk), 0)
    col = lax.broadcasted_iota(jnp.int32, (blk, blk), 1)
    mask_value = MASK_VALUE * LOG2_E
    probs, row_max, row_den = [], [], []
    for hd in range(HEADS_PER_GROUP):
        rows = slice(hd * blk, (hd + 1) * blk)

        def masked():
            if first:
                return jnp.where(col <= row, s_ref[rows, 0:blk], mask_value)
            return jnp.concatenate([jnp.where(col >= row, s_ref[rows, 0:blk], mask_value),
                                    jnp.where(col <= row, s_ref[rows, blk:2 * blk], mask_value)], axis=1)

        m = jnp.max(masked(), axis=-1, keepdims=True)
        p = jnp.exp2(masked() - m)
        row_max.append(m)
        row_den.append(jnp.sum(p, axis=-1, keepdims=True))
        probs.append(p.astype(BF16))
    return jnp.concatenate(probs, axis=0), row_max, row_den


def _attn_values(probs, row_max, row_den, vb):
    blk = ATTN_BLOCK
    in_head = _in_head_masks()
    lane = lax.broadcasted_iota(jnp.int32, (1, LANES), 1)
    pv = _dot(probs, vb)
    out, stats = pv[0:blk, :], jnp.ones((blk, LANES), F32)
    for hd in range(HEADS_PER_GROUP):
        if hd:
            out = jnp.where(in_head[hd], pv[hd * blk:(hd + 1) * blk, :], out)
        stats = jnp.where(lane == hd, row_max[hd], stats)
        stats = jnp.where(lane == HEADS_PER_GROUP + hd, row_den[hd], stats)
    return out, stats


def _block_row_slices(n, r, dilation, tile):
    chunk = tile // dilation
    piece = min(ATTN_BLOCK, chunk)
    slices = []
    for j in range(ATTN_BLOCK // piece):
        pos = n * ATTN_BLOCK + j * piece
        start = (pos // chunk) * tile + r * chunk + pos % chunk
        if not isinstance(start, int):
            start = pl.multiple_of(start, piece)
        slices.append(pl.ds(start, piece))
    return slices


def _block_rows(ref, n, r, dilation, tile):
    parts = [ref[rows, :] for rows in _block_row_slices(n, r, dilation, tile)]
    return parts[0] if len(parts) == 1 else jnp.concatenate(parts, axis=0)


def _store_block_rows(ref, n, r, dilation, tile, val):
    start = 0
    for rows in _block_row_slices(n, r, dilation, tile):
        ref[rows, :] = val[start:start + rows.size, :]
        start += rows.size


def _attn_group(dilation, q_ref, k_ref, v_ref, o_ref, lse_ref, score_s, *, seq, tile):
    blk = ATTN_BLOCK
    n_blocks = seq // dilation // blk
    assert n_blocks % 2 == 0

    def keys_values(ref, n, r, first):
        own = _block_rows(ref, n, r, dilation, tile)
        if first:
            return own
        return jnp.concatenate([_block_rows(ref, n - 1, r, dilation, tile), own], axis=0)

    width = 2 if n_blocks == 2 else ITEM_BLOCKS

    def pair(n, first):
        return tuple((n + i, first and i == 0) for i in range(width))

    def park_scores(slot, n, r, first):
        for i, (nb, fb) in enumerate(pair(n, first)):
            s = _attn_scores(_block_rows(q_ref, nb, r, dilation, tile), keys_values(k_ref, nb, r, fb))
            score_s[slot, i, :, 0:s.shape[1]] = s

    def finish(slot, n, r, first):
        blocks = pair(n, first)
        soft = [_attn_softmax(score_s.at[slot, i], fb) for i, (_, fb) in enumerate(blocks)]
        new = [_attn_values(*sm, keys_values(v_ref, nb, r, fb)) for sm, (nb, fb) in zip(soft, blocks)]
        for (nb, _), (out, lse) in zip(blocks, new):
            _store_block_rows(o_ref, nb, r, dilation, tile, out.astype(o_ref.dtype))
            _store_block_rows(lse_ref, nb, r, dilation, tile, lse)

    def run_items(n_items, item):
        assert n_items >= 2 and n_items % 2 == 0
        park_scores(0, *item(0, True))
        park_scores(1, *item(1, False))
        finish(0, *item(0, True))

        def double_step(t, c):
            i = 2 * t + 1
            park_scores(0, *item(i + 1, False))
            finish(1, *item(i, False))
            park_scores(1, *item(i + 2, False))
            finish(0, *item(i + 1, False))
            return c

        lax.fori_loop(0, (n_items - 2) // 2, double_step, 0)
        finish(1, *item(n_items - 1, False))

    if n_blocks == 2:
        run_items(dilation, lambda i, lead: (0, i, True))
    elif dilation == 1:
        run_items(n_blocks // width, lambda i, lead: (width * i, 0, lead))
    else:
        def per_residue(r, carry):
            run_items(n_blocks // width, lambda i, lead: (width * i, r, lead))
            return carry

        lax.fori_loop(0, dilation, per_residue, 0)


def _attn_kernel(q_ref, k_ref, v_ref, wff1_src, wff2_src, o_ref, lse_ref, wff1_dst, wff2_dst, score_s, *,
                 seq, tile):
    g = pl.program_id(1)

    @pl.when(g < CAST_GROUPS)
    def _():
        _cast_rows(wff1_src, [wff1_dst])
        _cast_rows(wff2_src, [wff2_dst])

    for gi, (_, dilation) in enumerate(ATTN_GROUPS):
        @pl.when(g == gi)
        def _(dilation=dilation):
            _attn_group(dilation, q_ref, k_ref, v_ref, o_ref, lse_ref, score_s, seq=seq, tile=tile)


def _attention(q, k, v, w_ff1, w_ff2, *, layer, tile):
    batch, _, seq, _ = q.shape
    spec = lambda width: pl.BlockSpec((None, None, seq, width), lambda b, g: (b, g, 0, 0))
    cast_step = lambda b, g: b * CAST_GROUPS + jnp.minimum(g, CAST_GROUPS - 1)
    cast_in, cast_out, cast_shape = [], [], []
    for w in (w_ff1, w_ff2):
        _, rows, cols = w.shape
        rb = rows // (batch * CAST_GROUPS)
        assert rows % (batch * CAST_GROUPS) == 0 and rb % BF16_ROWS == 0
        cast_in.append(pl.BlockSpec((None, rb, cols), lambda b, g: (layer, cast_step(b, g), 0)))
        cast_out.append(pl.BlockSpec((rb, cols), lambda b, g: (cast_step(b, g), 0)))
        cast_shape.append(jax.ShapeDtypeStruct((rows, cols), BF16))
    return pl.pallas_call(
        functools.partial(_attn_kernel, seq=seq, tile=tile),
        grid=(batch, N_GROUPS),
        in_specs=[spec(GROUP_WIDTH)] * 3 + cast_in,
        out_specs=[spec(GROUP_WIDTH), spec(LANES)] + cast_out,
        out_shape=[jax.ShapeDtypeStruct((batch, N_GROUPS, seq, GROUP_WIDTH), BF16),
                   jax.ShapeDtypeStruct((batch, N_GROUPS, seq, LANES), F32)] + cast_shape,
        scratch_shapes=[pltpu.VMEM((2, 2, HEADS_PER_GROUP * ATTN_BLOCK, 2 * ATTN_BLOCK), F32)],
        compiler_params=pltpu.CompilerParams(dimension_semantics=("arbitrary", "arbitrary"),
                                             vmem_limit_bytes=VMEM_LIMIT_BYTES),
        name="attention",
    )(q, k, v, w_ff1, w_ff2)


def _mix_attention_groups(o_ref, lse_ref, tok_o, tok_l):
    tm = o_ref.shape[1]
    lane = lax.broadcasted_iota(jnp.int32, (1, LANES), 1)
    outs, lses, dens = [], [], []
    for g, (_, dilation) in enumerate(ATTN_GROUPS):
        o, lse = o_ref[g].astype(F32), lse_ref[g]
        if dilation > 1:
            slot = sum(1 for _, d in ATTN_GROUPS[:g] if d > 1)
            chunk = tm // dilation
            for r in range(dilation):
                rows, src = pl.ds(r, chunk, stride=dilation), slice(r * chunk, (r + 1) * chunk)
                _put_halves(tok_o.at[slot], rows, o[src, :])
                tok_l[slot, rows, :] = lse[src, :]
            o, lse = _get_halves(tok_o.at[slot], slice(None)), tok_l[slot]
        den = jnp.where(lane < HEADS_PER_GROUP, pltpu.roll(lse, LANES - HEADS_PER_GROUP, 1), 1.0)
        outs.append(o)
        dens.append(den)
        lses.append(lse + jnp.log2(den))
    lse_max = functools.reduce(jnp.maximum, lses)
    weights = [jnp.exp2(lse - lse_max) for lse in lses]
    inv_sum = 1.0 / functools.reduce(jnp.add, weights)
    in_head = _in_head_masks()
    mixed = None
    for o, w, den in zip(outs, weights, dens):
        w = w * inv_sum / den
        w_lanes = None
        for hd in range(HEADS_PER_GROUP):
            col = jnp.sum(jnp.where(lane == hd, w, 0.0), axis=-1, keepdims=True)
            w_lanes = jnp.broadcast_to(col, o.shape) if w_lanes is None else jnp.where(in_head[hd], col, w_lanes)
        mixed = w_lanes * o if mixed is None else mixed + w_lanes * o
    return mixed


def _merge_kernel(x_ref, gain_ref, w_ref, bg_ref, d_ref, wmix_ref, pscale_ref, yc_ref, o_ref, lse_ref,
                  wp_ref, wc_ref, wa_ref, wo_ref, gain2_ref, xo_ref, h2_ref, tok_o, tok_l):
    dm = x_ref.shape[1]
    x = x_ref[...]
    h = _rms_norm(x, gain_ref[...]).astype(BF16)
    y_attn = _mix_attention_groups(o_ref, lse_ref, tok_o, tok_l).astype(BF16)

    y_pool = jnp.concatenate(
        [_dot(d_ref[:, g * POOL_GROUP:(g + 1) * POOL_GROUP], wmix_ref[g]) for g in range(len(POOL_WINDOWS))],
        axis=-1)
    y_pool = (y_pool * pscale_ref[...]).astype(BF16)

    merged = None
    for j, (y, wj_ref) in enumerate(((y_pool, wp_ref), (yc_ref[...], wc_ref), (y_attn, wa_ref))):
        gate_cols = slice(j * dm, (j + 1) * dm)
        gate = jax.nn.sigmoid(_dot(h, w_ref[:, gate_cols]) + bg_ref[:, gate_cols])
        term = gate * _dot(y, wj_ref[...])
        merged = term if merged is None else merged + term

    x_new = x + _dot(merged.astype(BF16), wo_ref[...])
    xo_ref[...] = x_new
    h2_ref[...] = _rms_norm(x_new, gain2_ref[...]).astype(BF16)


def _merge(x2d, gain, w_gates, b_gate, d, w_mix, pool_scale, yc, attn_o, attn_lse, w_pool_up, w_conv_out,
           w_attn_up, w_o, gain2):
    t, dm = x2d.shape
    tm = TM_MERGE
    seq = attn_o.shape[2]
    assert t % tm == 0 and seq % tm == 0 and TM_MERGE == TM_IN
    tiles_per_seq = seq // tm
    row = lambda width: pl.BlockSpec((tm, width), lambda i: (i, 0))
    groups = lambda width: pl.BlockSpec((None, N_GROUPS, tm, width),
                                        lambda i: (i // tiles_per_seq, 0, i % tiles_per_seq, 0))
    res = lambda w: _resident(w.shape)
    n_dilated = sum(1 for _, dilation in ATTN_GROUPS if dilation > 1)
    return pl.pallas_call(
        _merge_kernel,
        grid=(t // tm,),
        in_specs=[row(dm), _resident((1, dm)), res(w_gates), res(b_gate), row(POOL_WIDTH), res(w_mix),
                  _resident((1, POOL_WIDTH)), row(CONV_WIDTH), groups(GROUP_WIDTH), groups(LANES),
                  res(w_pool_up), res(w_conv_out), res(w_attn_up), res(w_o), _resident((1, dm))],
        out_specs=[row(dm), row(dm)],
        out_shape=[jax.ShapeDtypeStruct((t, dm), F32), jax.ShapeDtypeStruct((t, dm), BF16)],
        scratch_shapes=[pltpu.VMEM((n_dilated, GROUP_WIDTH // LANES, tm, LANES), F32),
                        pltpu.VMEM((n_dilated, tm, LANES), F32)],
        compiler_params=pltpu.CompilerParams(dimension_semantics=("arbitrary",),
                                             vmem_limit_bytes=VMEM_LIMIT_BYTES),
        name="merge",
    )(x2d, gain, w_gates, b_gate, d, w_mix, pool_scale, yc, attn_o, attn_lse, w_pool_up, w_conv_out, w_attn_up,
      w_o, gain2)


def _ffn_kernel(x_ref, h2_ref, w1_ref, w2_ref, *rest, cast_layout):
    o_ref = rest[len(cast_layout)]
    _run_cast_jobs(rest[:len(cast_layout)], rest[len(cast_layout) + 1:], cast_layout)
    d_ff = w1_ref.shape[1]
    h2 = h2_ref[...]
    acc = None
    for c in range(d_ff // FF_CHUNK):
        cols = slice(c * FF_CHUNK, (c + 1) * FF_CHUNK)
        a = jnp.square(jnp.maximum(_dot(h2, w1_ref[:, cols]), 0.0)).astype(BF16)
        part = _dot(a, w2_ref[cols, :])
        acc = part if acc is None else acc + part
    o_ref[...] = x_ref[...] + acc


def _ffn(x2d, h2, w1, w2, cast_jobs=()):
    t, dm = x2d.shape
    tm = TM_FFN
    assert t % tm == 0 and w1.shape[1] % FF_CHUNK == 0
    steps = t // tm
    row = pl.BlockSpec((tm, dm), lambda i: (i, 0))
    cast_in, cast_out, cast_shape, cast_layout = _cast_jobs_specs(cast_jobs, steps)
    return pl.pallas_call(
        functools.partial(_ffn_kernel, cast_layout=cast_layout),
        grid=(steps,),
        in_specs=[row, row, _resident(w1.shape), _resident(w2.shape), *cast_in],
        out_specs=[row] + cast_out,
        out_shape=[jax.ShapeDtypeStruct((t, dm), F32)] + cast_shape,
        compiler_params=pltpu.CompilerParams(dimension_semantics=("arbitrary",),
                                             vmem_limit_bytes=VMEM_LIMIT_BYTES),
        name="ffn",
    )(x2d, h2, w1, w2, *(w for w, _, _ in cast_jobs))


def kernel(x, norm_mix, w_in, b_gate, pool_mix, pool_scale, conv_w, q_gain, k_gain, w_pool_up, w_conv_out,
           w_attn_up, w_o, norm_mlp, w_ff1, w_ff2):
    batch, seq, dm = x.shape
    depth = norm_mix.shape[0]
    assert w_in.shape[2] == OFF_GATE + N_BRANCH * dm
    head_id = jnp.arange(GROUP_WIDTH) // HEAD_DIM
    headmean = jnp.where(head_id[:, None] == head_id[None, :], 1.0 / HEAD_DIM, 0.0).astype(BF16)
    x2d = x.reshape(batch * seq, dm)
    pool_mix = pool_mix.astype(BF16)
    up_weights = (w_pool_up, w_conv_out, w_attn_up, w_o)
    layer_jobs = lambda l: [(w_in, l, (OFF_GATE,))] + [(w, l, ()) for w in up_weights]
    w_mixers, w_gates, *ups = _cast_weights(layer_jobs(0))
    for l in range(depth):
        gain = norm_mix[l].reshape(1, dm)
        qgain = (jnp.tile(q_gain[l], HEADS_PER_GROUP) * (LOG2_E * HEAD_DIM ** -0.5)).reshape(1, GROUP_WIDTH)
        kgain = jnp.tile(k_gain[l], HEADS_PER_GROUP).reshape(1, GROUP_WIDTH)
        d, yc, q, k, v = _in_proj(x2d, gain, w_mixers, qgain, kgain, conv_w[l], headmean, batch=batch, seq=seq)
        attn_o, attn_lse, w1, w2 = _attention(q, k, v, w_ff1, w_ff2, layer=l, tile=TM_IN)
        x2d, h2 = _merge(x2d, gain, w_gates, b_gate[l].reshape(1, N_BRANCH * dm), d, pool_mix[l],
                         pool_scale[l].reshape(1, POOL_WIDTH), yc, attn_o, attn_lse, *ups,
                         norm_mlp[l].reshape(1, dm))
        if l + 1 < depth:
            x2d, w_mixers, w_gates, *ups = _ffn(x2d, h2, w1, w2, layer_jobs(l + 1))
        else:
            x2d, = _ffn(x2d, h2, w1, w2)
    return x2d.reshape(batch, seq, dm)
```

```python
import functools

import jax
import jax.numpy as jnp
from jax import lax
from jax.experimental import pallas as pl
from jax.experimental.pallas import tpu as pltpu

F32 = jnp.float32
BF16 = jnp.bfloat16

POOL_WINDOWS = (2, 4, 8, 16)
POOL_GROUP = 128
POOL_WIDTH = POOL_GROUP * len(POOL_WINDOWS)
CONV_WIDTH = 512
CONV_K = 3
HEAD_DIM = 64
ATTN_GROUPS = ((128, 1), (512, 4), (2048, 16))
HEADS_PER_GROUP = 4
GROUP_WIDTH = HEADS_PER_GROUP * HEAD_DIM
N_GROUPS = len(ATTN_GROUPS)
ATTN_WIDTH = N_GROUPS * GROUP_WIDTH
ATTN_BLOCK = 128
LANES = 128
BF16_ROWS = 16
N_BRANCH = 3
EPS = 1e-6
MASK_VALUE = -1e30
LOG2_E = 1.4426950408889634

OFF_POOL = 0
OFF_CONV = OFF_POOL + POOL_WIDTH
OFF_Q = OFF_CONV + 3 * CONV_WIDTH
OFF_K = OFF_Q + ATTN_WIDTH
OFF_V = OFF_K + ATTN_WIDTH
OFF_GATE = OFF_V + ATTN_WIDTH

POOL_HALO = 16
CONV_HALO = 8

VMEM_LIMIT_BYTES = 56 * 1024 * 1024
TM_IN = 1024
TM_MERGE = 1024
TM_FFN = 1024
FF_CHUNK = 2048
CAST_STEPS = 8
ITEM_BLOCKS = 2
CAST_GROUPS = 2


def _resident(shape):
    return pl.BlockSpec(shape, lambda *_: (0,) * len(shape), pipeline_mode=pl.Buffered(1))


def _cast_specs(w, layer, steps, splits=()):
    _, rows, cols = w.shape
    assert rows % steps == 0 and (rows // steps) % BF16_ROWS == 0
    rb = rows // steps
    bounds = (0,) + tuple(splits) + (cols,)
    widths = [b - a for a, b in zip(bounds, bounds[1:])]
    in_spec = pl.BlockSpec((None, rb, cols), lambda i: (layer, i, 0))
    out_specs = [pl.BlockSpec((rb, wd), lambda i: (i, 0)) for wd in widths]
    out_shapes = [jax.ShapeDtypeStruct((rows, wd), BF16) for wd in widths]
    return in_spec, out_specs, out_shapes


def _cast_rows(src_ref, dst_refs):
    start = 0
    for dst in dst_refs:
        dst[...] = src_ref[:, start:start + dst.shape[1]].astype(BF16)
        start += dst.shape[1]


def _cast_jobs_specs(jobs, steps):
    in_specs, out_specs, out_shapes, layout = [], [], [], []
    for w, layer, splits in jobs:
        in_spec, o_specs, o_shapes = _cast_specs(w, layer, steps, splits)
        in_specs.append(in_spec)
        out_specs += o_specs
        out_shapes += o_shapes
        layout.append(len(o_specs))
    return in_specs, out_specs, out_shapes, tuple(layout)


def _run_cast_jobs(srcs, dsts, layout):
    dsts = list(dsts)
    for src, n_dst in zip(srcs, layout):
        _cast_rows(src, [dsts.pop(0) for _ in range(n_dst)])


def _cast_kernel(*refs, layout):
    _run_cast_jobs(refs[:len(layout)], refs[len(layout):], layout)


def _cast_weights(jobs):
    in_specs, out_specs, out_shapes, layout = _cast_jobs_specs(jobs, CAST_STEPS)
    return pl.pallas_call(
        functools.partial(_cast_kernel, layout=layout),
        grid=(CAST_STEPS,),
        in_specs=in_specs,
        out_specs=out_specs,
        out_shape=out_shapes,
        compiler_params=pltpu.CompilerParams(dimension_semantics=("arbitrary",),
                                             vmem_limit_bytes=VMEM_LIMIT_BYTES),
        name="cast_weights",
    )(*(w for w, _, _ in jobs))


def _rms_norm(x, gain):
    ms = jnp.mean(x * x, axis=-1, keepdims=True)
    return x * lax.rsqrt(ms + EPS) * gain


def _dot(a, b):
    return jnp.dot(a, b, preferred_element_type=F32)


def _put_halves(ref, rows, val):
    for j in range(GROUP_WIDTH // LANES):
        ref[j, rows, :] = val[:, j * LANES:(j + 1) * LANES]


def _get_halves(ref, rows):
    return jnp.concatenate([ref[j, rows, :] for j in range(GROUP_WIDTH // LANES)], axis=-1)


def _in_proj_kernel(x_ref, gain_ref, w_ref, qgain_ref, kgain_ref, convw_ref, headmean_ref,
                    d_ref, yc_ref, q_ref, k_ref, v_ref, uhalo, chalo, zbuf, *, tm, tiles_per_seq):
    i = pl.program_id(0)

    @pl.when(i == 0)
    def _():
        uhalo[...] = jnp.zeros_like(uhalo)
        chalo[...] = jnp.zeros_like(chalo)

    tile_in_seq = i % tiles_per_seq
    carry = tile_in_seq != 0
    h = _rms_norm(x_ref[...], gain_ref[...]).astype(BF16)

    def emit_heads(z_all, slot, gain, o_ref):
        zs = [z_all[:, g * GROUP_WIDTH:(g + 1) * GROUP_WIDTH] for g in range(N_GROUPS)]
        if gain is not None:
            ms = _dot(jnp.concatenate([(z * z).astype(BF16) for z in zs], axis=0), headmean_ref[...])
            zs = [z * lax.rsqrt(ms[g * tm:(g + 1) * tm, :] + EPS) * gain[...] for g, z in enumerate(zs)]
        for g, (_, dilation) in enumerate(ATTN_GROUPS):
            if dilation == 1:
                o_ref[g] = zs[g].astype(BF16)
            else:
                buf = zbuf.at[(g - 1) * 3 + slot]
                _put_halves(buf, slice(None), zs[g])
                chunk = tm // dilation
                for r in range(dilation):
                    rows = pl.ds(r, chunk, stride=dilation)
                    o_ref[g, r * chunk:(r + 1) * chunk, :] = _get_halves(buf, rows).astype(BF16)

    zpc = _dot(h, w_ref[:, OFF_POOL:OFF_Q])

    u = zpc[:, OFF_POOL:OFF_POOL + POOL_WIDTH]
    ext = jnp.concatenate([jnp.where(carry, uhalo[...], 0.0), u], axis=0)
    uhalo[...] = u[tm - POOL_HALO:tm, :]
    pos = tile_in_seq * tm + lax.broadcasted_iota(jnp.int32, (tm, 1), 0)
    run = ext
    for g, w in enumerate(POOL_WINDOWS):
        run = run + pltpu.roll(run, w // 2, 0)
        cols = slice(g * POOL_GROUP, (g + 1) * POOL_GROUP)
        inv_count = 1.0 / jnp.minimum(pos + 1, w).astype(F32)
        d_ref[:, cols] = (run[POOL_HALO:, 0:POOL_GROUP] * inv_count - u[:, cols]).astype(BF16)
        run = run[:, POOL_GROUP:]

    zc = zpc[:, OFF_CONV:OFF_CONV + 3 * CONV_WIDTH]
    uc = zc[:, CONV_WIDTH:2 * CONV_WIDTH] * zc[:, 2 * CONV_WIDTH:3 * CONV_WIDTH]
    ext = jnp.concatenate([jnp.where(carry, chalo[...], 0.0), uc], axis=0)
    chalo[...] = uc[tm - CONV_HALO:tm, :]
    y = convw_ref[CONV_K - 1:CONV_K, :] * uc
    for j in range(CONV_K - 1):
        y = y + convw_ref[j:j + 1, :] * pltpu.roll(ext, CONV_K - 1 - j, 0)[CONV_HALO:, :]
    yc_ref[...] = (zc[:, 0:CONV_WIDTH] * y).astype(BF16)

    zqkv = _dot(h, w_ref[:, OFF_Q:OFF_GATE])
    for slot, (gain, o_ref) in enumerate(((qgain_ref, q_ref), (kgain_ref, k_ref), (None, v_ref))):
        emit_heads(zqkv[:, slot * ATTN_WIDTH:(slot + 1) * ATTN_WIDTH], slot, gain, o_ref)


def _in_proj(x2d, gain, w_mixers, qgain, kgain, convw, headmean, *, batch, seq):
    t, dm = x2d.shape
    assert w_mixers.shape == (dm, OFF_GATE)
    tm = TM_IN
    tiles_per_seq = seq // tm
    assert seq % tm == 0 and all(tm % (16 * d) == 0 for _, d in ATTN_GROUPS)
    row = lambda width: pl.BlockSpec((tm, width), lambda i: (i, 0))
    qkv_spec = pl.BlockSpec((None, N_GROUPS, tm, GROUP_WIDTH),
                            lambda i: (i // tiles_per_seq, 0, i % tiles_per_seq, 0))
    qkv_shape = jax.ShapeDtypeStruct((batch, N_GROUPS, seq, GROUP_WIDTH), BF16)
    n_regrouped = 3 * sum(1 for _, d in ATTN_GROUPS if d > 1)
    return pl.pallas_call(
        functools.partial(_in_proj_kernel, tm=tm, tiles_per_seq=tiles_per_seq),
        grid=(t // tm,),
        in_specs=[row(dm), _resident((1, dm)), _resident(w_mixers.shape),
                  _resident((1, GROUP_WIDTH)), _resident((1, GROUP_WIDTH)), _resident((CONV_K, CONV_WIDTH)),
                  _resident((GROUP_WIDTH, GROUP_WIDTH))],
        out_specs=[row(POOL_WIDTH), row(CONV_WIDTH), qkv_spec, qkv_spec, qkv_spec],
        out_shape=[jax.ShapeDtypeStruct((t, POOL_WIDTH), BF16), jax.ShapeDtypeStruct((t, CONV_WIDTH), BF16),
                   qkv_shape, qkv_shape, qkv_shape],
        scratch_shapes=[pltpu.VMEM((POOL_HALO, POOL_WIDTH), F32), pltpu.VMEM((CONV_HALO, CONV_WIDTH), F32),
                        pltpu.VMEM((n_regrouped, GROUP_WIDTH // LANES, tm, LANES), F32)],
        compiler_params=pltpu.CompilerParams(dimension_semantics=("arbitrary",),
                                             vmem_limit_bytes=VMEM_LIMIT_BYTES),
        name="in_proj",
    )(x2d, gain, w_mixers, qgain, kgain, convw, headmean)


def _in_head_masks():
    lane_head = lax.broadcasted_iota(jnp.int32, (1, GROUP_WIDTH), 1) // HEAD_DIM
    return [lane_head == hd for hd in range(HEADS_PER_GROUP)]


def _attn_scores(qb, kb):
    q_heads = jnp.concatenate([jnp.where(m, qb, jnp.zeros_like(qb)) for m in _in_head_masks()], axis=0)
    return lax.dot_general(q_heads, kb, (((1,), (1,)), ((), ())), preferred_element_type=F32)


def _attn_softmax(s_ref, first):
    blk = ATTN_BLOCK
    row = lax.broadcasted_iota(jnp.int32, (blk, blk), 0)
    col = lax.broadcasted_iota(jnp.int32, (blk, blk), 1)
    mask_value = MASK_VALUE * LOG2_E
    probs, row_max, row_den = [], [], []
    for hd in range(HEADS_PER_GROUP):
        rows = slice(hd * blk, (hd + 1) * blk)

        def masked():
            if first:
                return jnp.where(col <= row, s_ref[rows, 0:blk], mask_value)
            return jnp.concatenate([jnp.where(col >= row, s_ref[rows, 0:blk], mask_value),
                                    jnp.where(col <= row, s_ref[rows, blk:2 * blk], mask_value)], axis=1)

        m = jnp.max(masked(), axis=-1, keepdims=True)
        p = jnp.exp2(masked() - m)
        row_max.append(m)
        row_den.append(jnp.sum(p, axis=-1, keepdims=True))
        probs.append(p.astype(BF16))
    return jnp.concatenate(probs, axis=0), row_max, row_den


def _attn_values(probs, row_max, row_den, vb):
    blk = ATTN_BLOCK
    in_head = _in_head_masks()
    lane = lax.broadcasted_iota(jnp.int32, (1, LANES), 1)
    pv = _dot(probs, vb)
    out, stats = pv[0:blk, :], jnp.ones((blk, LANES), F32)
    for hd in range(HEADS_PER_GROUP):
        if hd:
            out = jnp.where(in_head[hd], pv[hd * blk:(hd + 1) * blk, :], out)
        stats = jnp.where(lane == hd, row_max[hd], stats)
        stats = jnp.where(lane == HEADS_PER_GROUP + hd, row_den[hd], stats)
    return out, stats


def _block_row_slices(n, r, dilation, tile):
    chunk = tile // dilation
    piece = min(ATTN_BLOCK, chunk)
    slices = []
    for j in range(ATTN_BLOCK // piece):
        pos = n * ATTN_BLOCK + j * piece
        start = (pos // chunk) * tile + r * chunk + pos % chunk
        if not isinstance(start, int):
            start = pl.multiple_of(start, piece)
        slices.append(pl.ds(start, piece))
    return slices


def _block_rows(ref, n, r, dilation, tile):
    parts = [ref[rows, :] for rows in _block_row_slices(n, r, dilation, tile)]
    return parts[0] if len(parts) == 1 else jnp.concatenate(parts, axis=0)


def _store_block_rows(ref, n, r, dilation, tile, val):
    start = 0
    for rows in _block_row_slices(n, r, dilation, tile):
        ref[rows, :] = val[start:start + rows.size, :]
        start += rows.size


def _attn_group(dilation, q_ref, k_ref, v_ref, o_ref, lse_ref, score_s, *, seq, tile):
    blk = ATTN_BLOCK
    n_blocks = seq // dilation // blk
    assert n_blocks % 2 == 0

    def keys_values(ref, n, r, first):
        own = _block_rows(ref, n, r, dilation, tile)
        if first:
            return own
        return jnp.concatenate([_block_rows(ref, n - 1, r, dilation, tile), own], axis=0)

    width = 2 if n_blocks == 2 else ITEM_BLOCKS

    def pair(n, first):
        return tuple((n + i, first and i == 0) for i in range(width))

    def park_scores(slot, n, r, first):
        for i, (nb, fb) in enumerate(pair(n, first)):
            s = _attn_scores(_block_rows(q_ref, nb, r, dilation, tile), keys_values(k_ref, nb, r, fb))
            score_s[slot, i, :, 0:s.shape[1]] = s

    def finish(slot, n, r, first):
        blocks = pair(n, first)
        soft = [_attn_softmax(score_s.at[slot, i], fb) for i, (_, fb) in enumerate(blocks)]
        new = [_attn_values(*sm, keys_values(v_ref, nb, r, fb)) for sm, (nb, fb) in zip(soft, blocks)]
        for (nb, _), (out, lse) in zip(blocks, new):
            _store_block_rows(o_ref, nb, r, dilation, tile, out.astype(o_ref.dtype))
            _store_block_rows(lse_ref, nb, r, dilation, tile, lse)

    def run_items(n_items, item):
        assert n_items >= 2 and n_items % 2 == 0
        park_scores(0, *item(0, True))
        park_scores(1, *item(1, False))
        finish(0, *item(0, True))

        def double_step(t, c):
            i = 2 * t + 1
            park_scores(0, *item(i + 1, False))
            finish(1, *item(i, False))
            park_scores(1, *item(i + 2, False))
            finish(0, *item(i + 1, False))
            return c

        lax.fori_loop(0, (n_items - 2) // 2, double_step, 0)
        finish(1, *item(n_items - 1, False))

    if n_blocks == 2:
        run_items(dilation, lambda i, lead: (0, i, True))
    elif dilation == 1:
        run_items(n_blocks // width, lambda i, lead: (width * i, 0, lead))
    else:
        def per_residue(r, carry):
            run_items(n_blocks // width, lambda i, lead: (width * i, r, lead))
            return carry

        lax.fori_loop(0, dilation, per_residue, 0)


def _attn_kernel(q_ref, k_ref, v_ref, wff1_src, wff2_src, o_ref, lse_ref, wff1_dst, wff2_dst, score_s, *,
                 seq, tile):
    g = pl.program_id(1)

    @pl.when(g < CAST_GROUPS)
    def _():
        _cast_rows(wff1_src, [wff1_dst])
        _cast_rows(wff2_src, [wff2_dst])

    for gi, (_, dilation) in enumerate(ATTN_GROUPS):
        @pl.when(g == gi)
        def _(dilation=dilation):
            _attn_group(dilation, q_ref, k_ref, v_ref, o_ref, lse_ref, score_s, seq=seq, tile=tile)


def _attention(q, k, v, w_ff1, w_ff2, *, layer, tile):
    batch, _, seq, _ = q.shape
    spec = lambda width: pl.BlockSpec((None, None, seq, width), lambda b, g: (b, g, 0, 0))
    cast_step = lambda b, g: b * CAST_GROUPS + jnp.minimum(g, CAST_GROUPS - 1)
    cast_in, cast_out, cast_shape = [], [], []
    for w in (w_ff1, w_ff2):
        _, rows, cols = w.shape
        rb = rows // (batch * CAST_GROUPS)
        assert rows % (batch * CAST_GROUPS) == 0 and rb % BF16_ROWS == 0
        cast_in.append(pl.BlockSpec((None, rb, cols), lambda b, g: (layer, cast_step(b, g), 0)))
        cast_out.append(pl.BlockSpec((rb, cols), lambda b, g: (cast_step(b, g), 0)))
        cast_shape.append(jax.ShapeDtypeStruct((rows, cols), BF16))
    return pl.pallas_call(
        functools.partial(_attn_kernel, seq=seq, tile=tile),
        grid=(batch, N_GROUPS),
        in_specs=[spec(GROUP_WIDTH)] * 3 + cast_in,
        out_specs=[spec(GROUP_WIDTH), spec(LANES)] + cast_out,
        out_shape=[jax.ShapeDtypeStruct((batch, N_GROUPS, seq, GROUP_WIDTH), BF16),
                   jax.ShapeDtypeStruct((batch, N_GROUPS, seq, LANES), F32)] + cast_shape,
        scratch_shapes=[pltpu.VMEM((2, 2, HEADS_PER_GROUP * ATTN_BLOCK, 2 * ATTN_BLOCK), F32)],
        compiler_params=pltpu.CompilerParams(dimension_semantics=("arbitrary", "arbitrary"),
                                             vmem_limit_bytes=VMEM_LIMIT_BYTES),
        name="attention",
    )(q, k, v, w_ff1, w_ff2)


def _mix_attention_groups(o_ref, lse_ref, tok_o, tok_l):
    tm = o_ref.shape[1]
    lane = lax.broadcasted_iota(jnp.int32, (1, LANES), 1)
    outs, lses, dens = [], [], []
    for g, (_, dilation) in enumerate(ATTN_GROUPS):
        o, lse = o_ref[g].astype(F32), lse_ref[g]
        if dilation > 1:
            slot = sum(1 for _, d in ATTN_GROUPS[:g] if d > 1)
            chunk = tm // dilation
            for r in range(dilation):
                rows, src = pl.ds(r, chunk, stride=dilation), slice(r * chunk, (r + 1) * chunk)
                _put_halves(tok_o.at[slot], rows, o[src, :])
                tok_l[slot, rows, :] = lse[src, :]
            o, lse = _get_halves(tok_o.at[slot], slice(None)), tok_l[slot]
        den = jnp.where(lane < HEADS_PER_GROUP, pltpu.roll(lse, LANES - HEADS_PER_GROUP, 1), 1.0)
        outs.append(o)
        dens.append(den)
        lses.append(lse + jnp.log2(den))
    lse_max = functools.reduce(jnp.maximum, lses)
    weights = [jnp.exp2(lse - lse_max) for lse in lses]
    inv_sum = 1.0 / functools.reduce(jnp.add, weights)
    in_head = _in_head_masks()
    mixed = None
    for o, w, den in zip(outs, weights, dens):
        w = w * inv_sum / den
        w_lanes = None
        for hd in range(HEADS_PER_GROUP):
            col = jnp.sum(jnp.where(lane == hd, w, 0.0), axis=-1, keepdims=True)
            w_lanes = jnp.broadcast_to(col, o.shape) if w_lanes is None else jnp.where(in_head[hd], col, w_lanes)
        mixed = w_lanes * o if mixed is None else mixed + w_lanes * o
    return mixed


def _merge_kernel(x_ref, gain_ref, w_ref, bg_ref, d_ref, wmix_ref, pscale_ref, yc_ref, o_ref, lse_ref,
                  wp_ref, wc_ref, wa_ref, wo_ref, gain2_ref, xo_ref, h2_ref, tok_o, tok_l):
    dm = x_ref.shape[1]
    x = x_ref[...]
    h = _rms_norm(x, gain_ref[...]).astype(BF16)
    y_attn = _mix_attention_groups(o_ref, lse_ref, tok_o, tok_l).astype(BF16)

    y_pool = jnp.concatenate(
        [_dot(d_ref[:, g * POOL_GROUP:(g + 1) * POOL_GROUP], wmix_ref[g]) for g in range(len(POOL_WINDOWS))],
        axis=-1)
    y_pool = (y_pool * pscale_ref[...]).astype(BF16)

    merged = None
    for j, (y, wj_ref) in enumerate(((y_pool, wp_ref), (yc_ref[...], wc_ref), (y_attn, wa_ref))):
        gate_cols = slice(j * dm, (j + 1) * dm)
        gate = jax.nn.sigmoid(_dot(h, w_ref[:, gate_cols]) + bg_ref[:, gate_cols])
        term = gate * _dot(y, wj_ref[...])
        merged = term if merged is None else merged + term

    x_new = x + _dot(merged.astype(BF16), wo_ref[...])
    xo_ref[...] = x_new
    h2_ref[...] = _rms_norm(x_new, gain2_ref[...]).astype(BF16)


def _merge(x2d, gain, w_gates, b_gate, d, w_mix, pool_scale, yc, attn_o, attn_lse, w_pool_up, w_conv_out,
           w_attn_up, w_o, gain2):
    t, dm = x2d.shape
    tm = TM_MERGE
    seq = attn_o.shape[2]
    assert t % tm == 0 and seq % tm == 0 and TM_MERGE == TM_IN
    tiles_per_seq = seq // tm
    row = lambda width: pl.BlockSpec((tm, width), lambda i: (i, 0))
    groups = lambda width: pl.BlockSpec((None, N_GROUPS, tm, width),
                                        lambda i: (i // tiles_per_seq, 0, i % tiles_per_seq, 0))
    res = lambda w: _resident(w.shape)
    n_dilated = sum(1 for _, dilation in ATTN_GROUPS if dilation > 1)
    return pl.pallas_call(
        _merge_kernel,
        grid=(t // tm,),
        in_specs=[row(dm), _resident((1, dm)), res(w_gates), res(b_gate), row(POOL_WIDTH), res(w_mix),
                  _resident((1, POOL_WIDTH)), row(CONV_WIDTH), groups(GROUP_WIDTH), groups(LANES),
                  res(w_pool_up), res(w_conv_out), res(w_attn_up), res(w_o), _resident((1, dm))],
        out_specs=[row(dm), row(dm)],
        out_shape=[jax.ShapeDtypeStruct((t, dm), F32), jax.ShapeDtypeStruct((t, dm), BF16)],
        scratch_shapes=[pltpu.VMEM((n_dilated, GROUP_WIDTH // LANES, tm, LANES), F32),
                        pltpu.VMEM((n_dilated, tm, LANES), F32)],
        compiler_params=pltpu.CompilerParams(dimension_semantics=("arbitrary",),
                                             vmem_limit_bytes=VMEM_LIMIT_BYTES),
        name="merge",
    )(x2d, gain, w_gates, b_gate, d, w_mix, pool_scale, yc, attn_o, attn_lse, w_pool_up, w_conv_out, w_attn_up,
      w_o, gain2)


def _ffn_kernel(x_ref, h2_ref, w1_ref, w2_ref, *rest, cast_layout):
    o_ref = rest[len(cast_layout)]
    _run_cast_jobs(rest[:len(cast_layout)], rest[len(cast_layout) + 1:], cast_layout)
    d_ff = w1_ref.shape[1]
    h2 = h2_ref[...]
    acc = None
    for c in range(d_ff // FF_CHUNK):
        cols = slice(c * FF_CHUNK, (c + 1) * FF_CHUNK)
        a = jnp.square(jnp.maximum(_dot(h2, w1_ref[:, cols]), 0.0)).astype(BF16)
        part = _dot(a, w2_ref[cols, :])
        acc = part if acc is None else acc + part
    o_ref[...] = x_ref[...] + acc


def _ffn(x2d, h2, w1, w2, cast_jobs=()):
    t, dm = x2d.shape
    tm = TM_FFN
    assert t % tm == 0 and w1.shape[1] % FF_CHUNK == 0
    steps = t // tm
    row = pl.BlockSpec((tm, dm), lambda i: (i, 0))
    cast_in, cast_out, cast_shape, cast_layout = _cast_jobs_specs(cast_jobs, steps)
    return pl.pallas_call(
        functools.partial(_ffn_kernel, cast_layout=cast_layout),
        grid=(steps,),
        in_specs=[row, row, _resident(w1.shape), _resident(w2.shape), *cast_in],
        out_specs=[row] + cast_out,
        out_shape=[jax.ShapeDtypeStruct((t, dm), F32)] + cast_shape,
        compiler_params=pltpu.CompilerParams(dimension_semantics=("arbitrary",),
                                             vmem_limit_bytes=VMEM_LIMIT_BYTES),
        name="ffn",
    )(x2d, h2, w1, w2, *(w for w, _, _ in cast_jobs))


def kernel(x, norm_mix, w_in, b_gate, pool_mix, pool_scale, conv_w, q_gain, k_gain, w_pool_up, w_conv_out,
           w_attn_up, w_o, norm_mlp, w_ff1, w_ff2):
    batch, seq, dm = x.shape
    depth = norm_mix.shape[0]
    assert w_in.shape[2] == OFF_GATE + N_BRANCH * dm
    head_id = jnp.arange(GROUP_WIDTH) // HEAD_DIM
    headmean = jnp.where(head_id[:, None] == head_id[None, :], 1.0 / HEAD_DIM, 0.0).astype(BF16)
    x2d = x.reshape(batch * seq, dm)
    pool_mix = pool_mix.astype(BF16)
    up_weights = (w_pool_up, w_conv_out, w_attn_up, w_o)
    layer_jobs = lambda l: [(w_in, l, (OFF_GATE,))] + [(w, l, ()) for w in up_weights]
    w_mixers, w_gates, *ups = _cast_weights(layer_jobs(0))
    for l in range(depth):
        gain = norm_mix[l].reshape(1, dm)
        qgain = (jnp.tile(q_gain[l], HEADS_PER_GROUP) * (LOG2_E * HEAD_DIM ** -0.5)).reshape(1, GROUP_WIDTH)
        kgain = jnp.tile(k_gain[l], HEADS_PER_GROUP).reshape(1, GROUP_WIDTH)
        d, yc, q, k, v = _in_proj(x2d, gain, w_mixers, qgain, kgain, conv_w[l], headmean, batch=batch, seq=seq)
        attn_o, attn_lse, w1, w2 = _attention(q, k, v, w_ff1, w_ff2, layer=l, tile=TM_IN)
        x2d, h2 = _merge(x2d, gain, w_gates, b_gate[l].reshape(1, N_BRANCH * dm), d, pool_mix[l],
                         pool_scale[l].reshape(1, POOL_WIDTH), yc, attn_o, attn_lse, *ups,
                         norm_mlp[l].reshape(1, dm))
        if l + 1 < depth:
            x2d, w_mixers, w_gates, *ups = _ffn(x2d, h2, w1, w2, layer_jobs(l + 1))
        else:
            x2d, = _ffn(x2d, h2, w1, w2)
    return x2d.reshape(batch, seq, dm)
```

```python
import functools

import jax
import jax.numpy as jnp
from jax import lax
from jax.experimental import pallas as pl
from jax.experimental.pallas import tpu as pltpu

F32 = jnp.float32
BF16 = jnp.bfloat16

POOL_WINDOWS = (2, 4, 8, 16)
POOL_GROUP = 128
POOL_WIDTH = POOL_GROUP * len(POOL_WINDOWS)
CONV_WIDTH = 512
CONV_K = 3
HEAD_DIM = 64
ATTN_GROUPS = ((128, 1), (512, 4), (2048, 16))
HEADS_PER_GROUP = 4
GROUP_WIDTH = HEADS_PER_GROUP * HEAD_DIM
N_GROUPS = len(ATTN_GROUPS)
ATTN_WIDTH = N_GROUPS * GROUP_WIDTH
ATTN_BLOCK = 128
LANES = 128
BF16_ROWS = 16
N_BRANCH = 3
EPS = 1e-6
MASK_VALUE = -1e30
LOG2_E = 1.4426950408889634

OFF_POOL = 0
OFF_CONV = OFF_POOL + POOL_WIDTH
OFF_Q = OFF_CONV + 3 * CONV_WIDTH
OFF_K = OFF_Q + ATTN_WIDTH
OFF_V = OFF_K + ATTN_WIDTH
OFF_GATE = OFF_V + ATTN_WIDTH

POOL_HALO = 16
CONV_HALO = 8

VMEM_LIMIT_BYTES = 56 * 1024 * 1024
TM_IN = 1024
TM_MERGE = 1024
TM_FFN = 1024
FF_CHUNK = 2048
CAST_STEPS = 8
ITEM_BLOCKS = 2
CAST_GROUPS = 2


def _resident(shape):
    return pl.BlockSpec(shape, lambda *_: (0,) * len(shape), pipeline_mode=pl.Buffered(1))


def _cast_specs(w, layer, steps, splits=()):
    _, rows, cols = w.shape
    assert rows % steps == 0 and (rows // steps) % BF16_ROWS == 0
    rb = rows // steps
    bounds = (0,) + tuple(splits) + (cols,)
    widths = [b - a for a, b in zip(bounds, bounds[1:])]
    in_spec = pl.BlockSpec((None, rb, cols), lambda i: (layer, i, 0))
    out_specs = [pl.BlockSpec((rb, wd), lambda i: (i, 0)) for wd in widths]
    out_shapes = [jax.ShapeDtypeStruct((rows, wd), BF16) for wd in widths]
    return in_spec, out_specs, out_shapes


def _cast_rows(src_ref, dst_refs):
    start = 0
    for dst in dst_refs:
        dst[...] = src_ref[:, start:start + dst.shape[1]].astype(BF16)
        start += dst.shape[1]


def _cast_jobs_specs(jobs, steps):
    in_specs, out_specs, out_shapes, layout = [], [], [], []
    for w, layer, splits in jobs:
        in_spec, o_specs, o_shapes = _cast_specs(w, layer, steps, splits)
        in_specs.append(in_spec)
        out_specs += o_specs
        out_shapes += o_shapes
        layout.append(len(o_specs))
    return in_specs, out_specs, out_shapes, tuple(layout)


def _run_cast_jobs(srcs, dsts, layout):
    dsts = list(dsts)
    for src, n_dst in zip(srcs, layout):
        _cast_rows(src, [dsts.pop(0) for _ in range(n_dst)])


def _cast_kernel(*refs, layout):
    _run_cast_jobs(refs[:len(layout)], refs[len(layout):], layout)


def _cast_weights(jobs):
    in_specs, out_specs, out_shapes, layout = _cast_jobs_specs(jobs, CAST_STEPS)
    return pl.pallas_call(
        functools.partial(_cast_kernel, layout=layout),
        grid=(CAST_STEPS,),
        in_specs=in_specs,
        out_specs=out_specs,
        out_shape=out_shapes,
        compiler_params=pltpu.CompilerParams(dimension_semantics=("arbitrary",),
                                             vmem_limit_bytes=VMEM_LIMIT_BYTES),
        name="cast_weights",
    )(*(w for w, _, _ in jobs))


def _rms_norm(x, gain):
    ms = jnp.mean(x * x, axis=-1, keepdims=True)
    return x * lax.rsqrt(ms + EPS) * gain


def _dot(a, b):
    return jnp.dot(a, b, preferred_element_type=F32)


def _regroup_rows(z, dilation):
    if dilation == 1:
        return z
    rows, width = z.shape
    return pltpu.einshape("crl->rcl", z.reshape(rows // dilation, dilation, width)).reshape(rows, width)


def _ungroup_rows(z, dilation):
    if dilation == 1:
        return z
    rows, width = z.shape
    return pltpu.einshape("rcl->crl", z.reshape(dilation, rows // dilation, width)).reshape(rows, width)


def _in_proj_kernel(x_ref, gain_ref, w_ref, qgain_ref, kgain_ref, convw_ref, headmean_ref,
                    d_ref, yc_ref, q_ref, k_ref, v_ref, uhalo, chalo, *, tm, tiles_per_seq):
    i = pl.program_id(0)

    @pl.when(i == 0)
    def _():
        uhalo[...] = jnp.zeros_like(uhalo)
        chalo[...] = jnp.zeros_like(chalo)

    tile_in_seq = i % tiles_per_seq
    carry = tile_in_seq != 0
    h = _rms_norm(x_ref[...], gain_ref[...]).astype(BF16)

    def emit_heads(z_all, slot, gain, o_ref):
        zs = [z_all[:, g * GROUP_WIDTH:(g + 1) * GROUP_WIDTH] for g in range(N_GROUPS)]
        if gain is not None:
            ms = _dot(jnp.concatenate([(z * z).astype(BF16) for z in zs], axis=0), headmean_ref[...])
            zs = [z * lax.rsqrt(ms[g * tm:(g + 1) * tm, :] + EPS) * gain[...] for g, z in enumerate(zs)]
        for g, (_, dilation) in enumerate(ATTN_GROUPS):
            o_ref[g] = _regroup_rows(zs[g].astype(BF16), dilation)

    zpc = _dot(h, w_ref[:, OFF_POOL:OFF_Q])

    u = zpc[:, OFF_POOL:OFF_POOL + POOL_WIDTH]
    ext = jnp.concatenate([jnp.where(carry, uhalo[...], 0.0), u], axis=0)
    uhalo[...] = u[tm - POOL_HALO:tm, :]
    pos = tile_in_seq * tm + lax.broadcasted_iota(jnp.int32, (tm, 1), 0)
    run = ext
    for g, w in enumerate(POOL_WINDOWS):
        run = run + pltpu.roll(run, w // 2, 0)
        cols = slice(g * POOL_GROUP, (g + 1) * POOL_GROUP)
        inv_count = 1.0 / jnp.minimum(pos + 1, w).astype(F32)
        d_ref[:, cols] = (run[POOL_HALO:, 0:POOL_GROUP] * inv_count - u[:, cols]).astype(BF16)
        run = run[:, POOL_GROUP:]

    zc = zpc[:, OFF_CONV:OFF_CONV + 3 * CONV_WIDTH]
    uc = zc[:, CONV_WIDTH:2 * CONV_WIDTH] * zc[:, 2 * CONV_WIDTH:3 * CONV_WIDTH]
    ext = jnp.concatenate([jnp.where(carry, chalo[...], 0.0), uc], axis=0)
    chalo[...] = uc[tm - CONV_HALO:tm, :]
    y = convw_ref[CONV_K - 1:CONV_K, :] * uc
    for j in range(CONV_K - 1):
        y = y + convw_ref[j:j + 1, :] * pltpu.roll(ext, CONV_K - 1 - j, 0)[CONV_HALO:, :]
    yc_ref[...] = (zc[:, 0:CONV_WIDTH] * y).astype(BF16)

    zqkv = _dot(h, w_ref[:, OFF_Q:OFF_GATE])
    for slot, (gain, o_ref) in enumerate(((qgain_ref, q_ref), (kgain_ref, k_ref), (None, v_ref))):
        emit_heads(zqkv[:, slot * ATTN_WIDTH:(slot + 1) * ATTN_WIDTH], slot, gain, o_ref)


def _in_proj(x2d, gain, w_mixers, qgain, kgain, convw, headmean, *, batch, seq):
    t, dm = x2d.shape
    assert w_mixers.shape == (dm, OFF_GATE)
    tm = TM_IN
    tiles_per_seq = seq // tm
    assert seq % tm == 0 and all(tm % (16 * d) == 0 for _, d in ATTN_GROUPS)
    row = lambda width: pl.BlockSpec((tm, width), lambda i: (i, 0))
    qkv_spec = pl.BlockSpec((None, N_GROUPS, tm, GROUP_WIDTH),
                            lambda i: (i // tiles_per_seq, 0, i % tiles_per_seq, 0))
    qkv_shape = jax.ShapeDtypeStruct((batch, N_GROUPS, seq, GROUP_WIDTH), BF16)
    return pl.pallas_call(
        functools.partial(_in_proj_kernel, tm=tm, tiles_per_seq=tiles_per_seq),
        grid=(t // tm,),
        in_specs=[row(dm), _resident((1, dm)), _resident(w_mixers.shape),
                  _resident((1, GROUP_WIDTH)), _resident((1, GROUP_WIDTH)), _resident((CONV_K, CONV_WIDTH)),
                  _resident((GROUP_WIDTH, GROUP_WIDTH))],
        out_specs=[row(POOL_WIDTH), row(CONV_WIDTH), qkv_spec, qkv_spec, qkv_spec],
        out_shape=[jax.ShapeDtypeStruct((t, POOL_WIDTH), BF16), jax.ShapeDtypeStruct((t, CONV_WIDTH), BF16),
                   qkv_shape, qkv_shape, qkv_shape],
        scratch_shapes=[pltpu.VMEM((POOL_HALO, POOL_WIDTH), F32), pltpu.VMEM((CONV_HALO, CONV_WIDTH), F32)],
        compiler_params=pltpu.CompilerParams(dimension_semantics=("arbitrary",),
                                             vmem_limit_bytes=VMEM_LIMIT_BYTES),
        name="in_proj",
    )(x2d, gain, w_mixers, qgain, kgain, convw, headmean)


def _in_head_masks():
    lane_head = lax.broadcasted_iota(jnp.int32, (1, GROUP_WIDTH), 1) // HEAD_DIM
    return [lane_head == hd for hd in range(HEADS_PER_GROUP)]


def _attn_scores(qb, kb):
    q_heads = jnp.concatenate([jnp.where(m, qb, jnp.zeros_like(qb)) for m in _in_head_masks()], axis=0)
    return lax.dot_general(q_heads, kb, (((1,), (1,)), ((), ())), preferred_element_type=F32)


def _attn_softmax(s_ref, first):
    blk = ATTN_BLOCK
    row = lax.broadcasted_iota(jnp.int32, (blk, blk), 0)
    col = lax.broadcasted_iota(jnp.int32, (blk, blk), 1)
    mask_value = MASK_VALUE * LOG2_E
    probs, row_max, row_den = [], [], []
    for hd in range(HEADS_PER_GROUP):
        rows = slice(hd * blk, (hd + 1) * blk)

        def masked():
            if first:
                return jnp.where(col <= row, s_ref[rows, 0:blk], mask_value)
            return jnp.concatenate([jnp.where(col >= row, s_ref[rows, 0:blk], mask_value),
                                    jnp.where(col <= row, s_ref[rows, blk:2 * blk], mask_value)], axis=1)

        m = jnp.max(masked(), axis=-1, keepdims=True)
        p = jnp.exp2(masked() - m)
        row_max.append(m)
        row_den.append(jnp.sum(p, axis=-1, keepdims=True))
        probs.append(p.astype(BF16))
    return jnp.concatenate(probs, axis=0), row_max, row_den


def _attn_values(probs, row_max, row_den, vb):
    blk = ATTN_BLOCK
    in_head = _in_head_masks()
    lane = lax.broadcasted_iota(jnp.int32, (1, LANES), 1)
    pv = _dot(probs, vb)
    out, stats = pv[0:blk, :], jnp.ones((blk, LANES), F32)
    for hd in range(HEADS_PER_GROUP):
        if hd:
            out = jnp.where(in_head[hd], pv[hd * blk:(hd + 1) * blk, :], out)
        stats = jnp.where(lane == hd, row_max[hd], stats)
        stats = jnp.where(lane == HEADS_PER_GROUP + hd, row_den[hd], stats)
    return out, stats


def _block_row_slices(n, r, dilation, tile):
    chunk = tile // dilation
    piece = min(ATTN_BLOCK, chunk)
    slices = []
    for j in range(ATTN_BLOCK // piece):
        pos = n * ATTN_BLOCK + j * piece
        start = (pos // chunk) * tile + r * chunk + pos % chunk
        if not isinstance(start, int):
            start = pl.multiple_of(start, piece)
        slices.append(pl.ds(start, piece))
    return slices


def _block_rows(ref, n, r, dilation, tile):
    parts = [ref[rows, :] for rows in _block_row_slices(n, r, dilation, tile)]
    return parts[0] if len(parts) == 1 else jnp.concatenate(parts, axis=0)


def _store_block_rows(ref, n, r, dilation, tile, val):
    start = 0
    for rows in _block_row_slices(n, r, dilation, tile):
        ref[rows, :] = val[start:start + rows.size, :]
        start += rows.size


def _attn_group(dilation, q_ref, k_ref, v_ref, o_ref, lse_ref, score_s, *, seq, tile):
    blk = ATTN_BLOCK
    n_blocks = seq // dilation // blk
    assert n_blocks % 2 == 0

    def keys_values(ref, n, r, first):
        own = _block_rows(ref, n, r, dilation, tile)
        if first:
            return own
        return jnp.concatenate([_block_rows(ref, n - 1, r, dilation, tile), own], axis=0)

    width = 2 if n_blocks == 2 else ITEM_BLOCKS

    def pair(n, first):
        return tuple((n + i, first and i == 0) for i in range(width))

    def park_scores(slot, n, r, first):
        for i, (nb, fb) in enumerate(pair(n, first)):
            s = _attn_scores(_block_rows(q_ref, nb, r, dilation, tile), keys_values(k_ref, nb, r, fb))
            score_s[slot, i, :, 0:s.shape[1]] = s

    def finish(slot, n, r, first):
        blocks = pair(n, first)
        soft = [_attn_softmax(score_s.at[slot, i], fb) for i, (_, fb) in enumerate(blocks)]
        new = [_attn_values(*sm, keys_values(v_ref, nb, r, fb)) for sm, (nb, fb) in zip(soft, blocks)]
        for (nb, _), (out, lse) in zip(blocks, new):
            _store_block_rows(o_ref, nb, r, dilation, tile, out.astype(o_ref.dtype))
            _store_block_rows(lse_ref, nb, r, dilation, tile, lse)

    def run_items(n_items, item):
        assert n_items >= 2 and n_items % 2 == 0
        park_scores(0, *item(0, True))
        park_scores(1, *item(1, False))
        finish(0, *item(0, True))

        def double_step(t, c):
            i = 2 * t + 1
            park_scores(0, *item(i + 1, False))
            finish(1, *item(i, False))
            park_scores(1, *item(i + 2, False))
            finish(0, *item(i + 1, False))
            return c

        lax.fori_loop(0, (n_items - 2) // 2, double_step, 0)
        finish(1, *item(n_items - 1, False))

    if n_blocks == 2:
        run_items(dilation, lambda i, lead: (0, i, True))
    elif dilation == 1:
        run_items(n_blocks // width, lambda i, lead: (width * i, 0, lead))
    else:
        def per_residue(r, carry):
            run_items(n_blocks // width, lambda i, lead: (width * i, r, lead))
            return carry

        lax.fori_loop(0, dilation, per_residue, 0)


def _attn_kernel(q_ref, k_ref, v_ref, wff1_src, wff2_src, o_ref, lse_ref, wff1_dst, wff2_dst, score_s, *,
                 seq, tile):
    g = pl.program_id(1)

    @pl.when(g < CAST_GROUPS)
    def _():
        _cast_rows(wff1_src, [wff1_dst])
        _cast_rows(wff2_src, [wff2_dst])

    for gi, (_, dilation) in enumerate(ATTN_GROUPS):
        @pl.when(g == gi)
        def _(dilation=dilation):
            _attn_group(dilation, q_ref, k_ref, v_ref, o_ref, lse_ref, score_s, seq=seq, tile=tile)


def _attention(q, k, v, w_ff1, w_ff2, *, layer, tile):
    batch, _, seq, _ = q.shape
    spec = lambda width: pl.BlockSpec((None, None, seq, width), lambda b, g: (b, g, 0, 0))
    cast_step = lambda b, g: b * CAST_GROUPS + jnp.minimum(g, CAST_GROUPS - 1)
    cast_in, cast_out, cast_shape = [], [], []
    for w in (w_ff1, w_ff2):
        _, rows, cols = w.shape
        rb = rows // (batch * CAST_GROUPS)
        assert rows % (batch * CAST_GROUPS) == 0 and rb % BF16_ROWS == 0
        cast_in.append(pl.BlockSpec((None, rb, cols), lambda b, g: (layer, cast_step(b, g), 0)))
        cast_out.append(pl.BlockSpec((rb, cols), lambda b, g: (cast_step(b, g), 0)))
        cast_shape.append(jax.ShapeDtypeStruct((rows, cols), BF16))
    return pl.pallas_call(
        functools.partial(_attn_kernel, seq=seq, tile=tile),
        grid=(batch, N_GROUPS),
        in_specs=[spec(GROUP_WIDTH)] * 3 + cast_in,
        out_specs=[spec(GROUP_WIDTH), spec(LANES)] + cast_out,
        out_shape=[jax.ShapeDtypeStruct((batch, N_GROUPS, seq, GROUP_WIDTH), BF16),
                   jax.ShapeDtypeStruct((batch, N_GROUPS, seq, LANES), F32)] + cast_shape,
        scratch_shapes=[pltpu.VMEM((2, 2, HEADS_PER_GROUP * ATTN_BLOCK, 2 * ATTN_BLOCK), F32)],
        compiler_params=pltpu.CompilerParams(dimension_semantics=("arbitrary", "arbitrary"),
                                             vmem_limit_bytes=VMEM_LIMIT_BYTES),
        name="attention",
    )(q, k, v, w_ff1, w_ff2)


def _mix_attention_groups(o_ref, lse_ref):
    lane = lax.broadcasted_iota(jnp.int32, (1, LANES), 1)
    outs, lses, dens = [], [], []
    for g, (_, dilation) in enumerate(ATTN_GROUPS):
        o = _ungroup_rows(o_ref[g], dilation).astype(F32)
        lse = _ungroup_rows(lse_ref[g], dilation)
        den = jnp.where(lane < HEADS_PER_GROUP, pltpu.roll(lse, LANES - HEADS_PER_GROUP, 1), 1.0)
        outs.append(o)
        dens.append(den)
        lses.append(lse + jnp.log2(den))
    lse_max = functools.reduce(jnp.maximum, lses)
    weights = [jnp.exp2(lse - lse_max) for lse in lses]
    inv_sum = 1.0 / functools.reduce(jnp.add, weights)
    in_head = _in_head_masks()
    mixed = None
    for o, w, den in zip(outs, weights, dens):
        w = w * inv_sum / den
        w_lanes = None
        for hd in range(HEADS_PER_GROUP):
            col = jnp.sum(jnp.where(lane == hd, w, 0.0), axis=-1, keepdims=True)
            w_lanes = jnp.broadcast_to(col, o.shape) if w_lanes is None else jnp.where(in_head[hd], col, w_lanes)
        mixed = w_lanes * o if mixed is None else mixed + w_lanes * o
    return mixed


def _merge_kernel(x_ref, gain_ref, w_ref, bg_ref, d_ref, wmix_ref, pscale_ref, yc_ref, o_ref, lse_ref,
                  wp_ref, wc_ref, wa_ref, wo_ref, gain2_ref, xo_ref, h2_ref):
    dm = x_ref.shape[1]
    x = x_ref[...]
    h = _rms_norm(x, gain_ref[...]).astype(BF16)
    y_attn = _mix_attention_groups(o_ref, lse_ref).astype(BF16)

    y_pool = jnp.concatenate(
        [_dot(d_ref[:, g * POOL_GROUP:(g + 1) * POOL_GROUP], wmix_ref[g]) for g in range(len(POOL_WINDOWS))],
        axis=-1)
    y_pool = (y_pool * pscale_ref[...]).astype(BF16)

    merged = None
    for j, (y, wj_ref) in enumerate(((y_pool, wp_ref), (yc_ref[...], wc_ref), (y_attn, wa_ref))):
        gate_cols = slice(j * dm, (j + 1) * dm)
        gate = jax.nn.sigmoid(_dot(h, w_ref[:, gate_cols]) + bg_ref[:, gate_cols])
        term = gate * _dot(y, wj_ref[...])
        merged = term if merged is None else merged + term

    x_new = x + _dot(merged.astype(BF16), wo_ref[...])
    xo_ref[...] = x_new
    h2_ref[...] = _rms_norm(x_new, gain2_ref[...]).astype(BF16)


def _merge(x2d, gain, w_gates, b_gate, d, w_mix, pool_scale, yc, attn_o, attn_lse, w_pool_up, w_conv_out,
           w_attn_up, w_o, gain2):
    t, dm = x2d.shape
    tm = TM_MERGE
    seq = attn_o.shape[2]
    assert t % tm == 0 and seq % tm == 0 and TM_MERGE == TM_IN
    tiles_per_seq = seq // tm
    row = lambda width: pl.BlockSpec((tm, width), lambda i: (i, 0))
    groups = lambda width: pl.BlockSpec((None, N_GROUPS, tm, width),
                                        lambda i: (i // tiles_per_seq, 0, i % tiles_per_seq, 0))
    res = lambda w: _resident(w.shape)
    return pl.pallas_call(
        _merge_kernel,
        grid=(t // tm,),
        in_specs=[row(dm), _resident((1, dm)), res(w_gates), res(b_gate), row(POOL_WIDTH), res(w_mix),
                  _resident((1, POOL_WIDTH)), row(CONV_WIDTH), groups(GROUP_WIDTH), groups(LANES),
                  res(w_pool_up), res(w_conv_out), res(w_attn_up), res(w_o), _resident((1, dm))],
        out_specs=[row(dm), row(dm)],
        out_shape=[jax.ShapeDtypeStruct((t, dm), F32), jax.ShapeDtypeStruct((t, dm), BF16)],
        compiler_params=pltpu.CompilerParams(dimension_semantics=("arbitrary",),
                                             vmem_limit_bytes=VMEM_LIMIT_BYTES),
        name="merge",
    )(x2d, gain, w_gates, b_gate, d, w_mix, pool_scale, yc, attn_o, attn_lse, w_pool_up, w_conv_out, w_attn_up,
      w_o, gain2)


def _ffn_kernel(x_ref, h2_ref, w1_ref, w2_ref, *rest, cast_layout):
    o_ref = rest[len(cast_layout)]
    _run_cast_jobs(rest[:len(cast_layout)], rest[len(cast_layout) + 1:], cast_layout)
    d_ff = w1_ref.shape[1]
    h2 = h2_ref[...]
    acc = None
    for c in range(d_ff // FF_CHUNK):
        cols = slice(c * FF_CHUNK, (c + 1) * FF_CHUNK)
        a = jnp.square(jnp.maximum(_dot(h2, w1_ref[:, cols]), 0.0)).astype(BF16)
        part = _dot(a, w2_ref[cols, :])
        acc = part if acc is None else acc + part
    o_ref[...] = x_ref[...] + acc


def _ffn(x2d, h2, w1, w2, cast_jobs=()):
    t, dm = x2d.shape
    tm = TM_FFN
    assert t % tm == 0 and w1.shape[1] % FF_CHUNK == 0
    steps = t // tm
    row = pl.BlockSpec((tm, dm), lambda i: (i, 0))
    cast_in, cast_out, cast_shape, cast_layout = _cast_jobs_specs(cast_jobs, steps)
    return pl.pallas_call(
        functools.partial(_ffn_kernel, cast_layout=cast_layout),
        grid=(steps,),
        in_specs=[row, row, _resident(w1.shape), _resident(w2.shape), *cast_in],
        out_specs=[row] + cast_out,
        out_shape=[jax.ShapeDtypeStruct((t, dm), F32)] + cast_shape,
        compiler_params=pltpu.CompilerParams(dimension_semantics=("arbitrary",),
                                             vmem_limit_bytes=VMEM_LIMIT_BYTES),
        name="ffn",
    )(x2d, h2, w1, w2, *(w for w, _, _ in cast_jobs))


def kernel(x, norm_mix, w_in, b_gate, pool_mix, pool_scale, conv_w, q_gain, k_gain, w_pool_up, w_conv_out,
           w_attn_up, w_o, norm_mlp, w_ff1, w_ff2):
    batch, seq, dm = x.shape
    depth = norm_mix.shape[0]
    assert w_in.shape[2] == OFF_GATE + N_BRANCH * dm
    head_id = jnp.arange(GROUP_WIDTH) // HEAD_DIM
    headmean = jnp.where(head_id[:, None] == head_id[None, :], 1.0 / HEAD_DIM, 0.0).astype(BF16)
    x2d = x.reshape(batch * seq, dm)
    pool_mix = pool_mix.astype(BF16)
    up_weights = (w_pool_up, w_conv_out, w_attn_up, w_o)
    layer_jobs = lambda l: [(w_in, l, (OFF_GATE,))] + [(w, l, ()) for w in up_weights]
    w_mixers, w_gates, *ups = _cast_weights(layer_jobs(0))
    for l in range(depth):
        gain = norm_mix[l].reshape(1, dm)
        qgain = (jnp.tile(q_gain[l], HEADS_PER_GROUP) * (LOG2_E * HEAD_DIM ** -0.5)).reshape(1, GROUP_WIDTH)
        kgain = jnp.tile(k_gain[l], HEADS_PER_GROUP).reshape(1, GROUP_WIDTH)
        d, yc, q, k, v = _in_proj(x2d, gain, w_mixers, qgain, kgain, conv_w[l], headmean, batch=batch, seq=seq)
        attn_o, attn_lse, w1, w2 = _attention(q, k, v, w_ff1, w_ff2, layer=l, tile=TM_IN)
        x2d, h2 = _merge(x2d, gain, w_gates, b_gate[l].reshape(1, N_BRANCH * dm), d, pool_mix[l],
                         pool_scale[l].reshape(1, POOL_WIDTH), yc, attn_o, attn_lse, *ups,
                         norm_mlp[l].reshape(1, dm))
        if l + 1 < depth:
            x2d, w_mixers, w_gates, *ups = _ffn(x2d, h2, w1, w2, layer_jobs(l + 1))
        else:
            x2d, = _ffn(x2d, h2, w1, w2)
    return x2d.reshape(batch, seq, dm)
```

```python
import functools

import jax
import jax.numpy as jnp
from jax import lax
from jax.experimental import pallas as pl
from jax.experimental.pallas import tpu as pltpu

F32 = jnp.float32
BF16 = jnp.bfloat16

POOL_WINDOWS = (2, 4, 8, 16)
POOL_GROUP = 128
POOL_WIDTH = POOL_GROUP * len(POOL_WINDOWS)
CONV_WIDTH = 512
CONV_K = 3
HEAD_DIM = 64
ATTN_GROUPS = ((128, 1), (512, 4), (2048, 16))
HEADS_PER_GROUP = 4
GROUP_WIDTH = HEADS_PER_GROUP * HEAD_DIM
N_GROUPS = len(ATTN_GROUPS)
ATTN_WIDTH = N_GROUPS * GROUP_WIDTH
ATTN_BLOCK = 128
LANES = 128
BF16_ROWS = 16
N_BRANCH = 3
EPS = 1e-6
MASK_VALUE = -1e30
LOG2_E = 1.4426950408889634

OFF_POOL = 0
OFF_CONV = OFF_POOL + POOL_WIDTH
OFF_Q = OFF_CONV + 3 * CONV_WIDTH
OFF_K = OFF_Q + ATTN_WIDTH
OFF_V = OFF_K + ATTN_WIDTH
OFF_GATE = OFF_V + ATTN_WIDTH

POOL_HALO = 16
CONV_HALO = 8

VMEM_LIMIT_BYTES = 56 * 1024 * 1024
TM_IN = 1024
TM_MERGE = 1024
TM_FFN = 1024
FF_CHUNK = 2048
TAIL_PARTS = 4
CAST_STEPS = 8
ITEM_BLOCKS = 2


def _resident(shape):
    return pl.BlockSpec(shape, lambda *_: (0,) * len(shape), pipeline_mode=pl.Buffered(1))


def _cast_specs(w, layer, steps, splits=()):
    _, rows, cols = w.shape
    assert rows % steps == 0 and (rows // steps) % BF16_ROWS == 0
    rb = rows // steps
    bounds = (0,) + tuple(splits) + (cols,)
    widths = [b - a for a, b in zip(bounds, bounds[1:])]
    in_spec = pl.BlockSpec((None, rb, cols), lambda i: (layer, i, 0))
    out_specs = [pl.BlockSpec((rb, wd), lambda i: (i, 0)) for wd in widths]
    out_shapes = [jax.ShapeDtypeStruct((rows, wd), BF16) for wd in widths]
    return in_spec, out_specs, out_shapes


def _cast_rows(src_ref, dst_refs):
    start = 0
    for dst in dst_refs:
        dst[...] = src_ref[:, start:start + dst.shape[1]].astype(BF16)
        start += dst.shape[1]


def _cast_jobs_specs(jobs, steps):
    in_specs, out_specs, out_shapes, layout = [], [], [], []
    for w, layer, splits in jobs:
        in_spec, o_specs, o_shapes = _cast_specs(w, layer, steps, splits)
        in_specs.append(in_spec)
        out_specs += o_specs
        out_shapes += o_shapes
        layout.append(len(o_specs))
    return in_specs, out_specs, out_shapes, tuple(layout)


def _run_cast_jobs(srcs, dsts, layout):
    dsts = list(dsts)
    for src, n_dst in zip(srcs, layout):
        _cast_rows(src, [dsts.pop(0) for _ in range(n_dst)])


def _cast_kernel(*refs, layout):
    _run_cast_jobs(refs[:len(layout)], refs[len(layout):], layout)


def _cast_weights(jobs):
    in_specs, out_specs, out_shapes, layout = _cast_jobs_specs(jobs, CAST_STEPS)
    return pl.pallas_call(
        functools.partial(_cast_kernel, layout=layout),
        grid=(CAST_STEPS,),
        in_specs=in_specs,
        out_specs=out_specs,
        out_shape=out_shapes,
        compiler_params=pltpu.CompilerParams(dimension_semantics=("arbitrary",),
                                             vmem_limit_bytes=VMEM_LIMIT_BYTES),
        name="cast_weights",
    )(*(w for w, _, _ in jobs))


def _rms_norm(x, gain):
    ms = jnp.mean(x * x, axis=-1, keepdims=True)
    return x * lax.rsqrt(ms + EPS) * gain


def _dot(a, b):
    return jnp.dot(a, b, preferred_element_type=F32)


def _regroup_rows(z, dilation):
    if dilation == 1:
        return z
    rows, width = z.shape
    return pltpu.einshape("crl->rcl", z.reshape(rows // dilation, dilation, width)).reshape(rows, width)


def _ungroup_rows(z, dilation):
    if dilation == 1:
        return z
    rows, width = z.shape
    return pltpu.einshape("rcl->crl", z.reshape(dilation, rows // dilation, width)).reshape(rows, width)


def _in_proj_kernel(x_ref, gain_ref, w_ref, qgain_ref, kgain_ref, convw_ref, headmean_ref, wff1_src, wff2_src,
                    d_ref, yc_ref, q_ref, k_ref, v_ref, wff1_dst, wff2_dst, uhalo, chalo, *, tm, tiles_per_seq):
    i = pl.program_id(0)
    _cast_rows(wff1_src, [wff1_dst])
    _cast_rows(wff2_src, [wff2_dst])

    @pl.when(i == 0)
    def _():
        uhalo[...] = jnp.zeros_like(uhalo)
        chalo[...] = jnp.zeros_like(chalo)

    tile_in_seq = i % tiles_per_seq
    carry = tile_in_seq != 0
    h = _rms_norm(x_ref[...], gain_ref[...]).astype(BF16)

    def emit_heads(z_all, slot, gain, o_ref):
        zs = [z_all[:, g * GROUP_WIDTH:(g + 1) * GROUP_WIDTH] for g in range(N_GROUPS)]
        if gain is not None:
            ms = _dot(jnp.concatenate([(z * z).astype(BF16) for z in zs], axis=0), headmean_ref[...])
            zs = [z * lax.rsqrt(ms[g * tm:(g + 1) * tm, :] + EPS) * gain[...] for g, z in enumerate(zs)]
        for g, (_, dilation) in enumerate(ATTN_GROUPS):
            o_ref[g] = _regroup_rows(zs[g].astype(BF16), dilation)

    zpc = _dot(h, w_ref[:, OFF_POOL:OFF_Q])

    u = zpc[:, OFF_POOL:OFF_POOL + POOL_WIDTH]
    ext = jnp.concatenate([jnp.where(carry, uhalo[...], 0.0), u], axis=0)
    uhalo[...] = u[tm - POOL_HALO:tm, :]
    pos = tile_in_seq * tm + lax.broadcasted_iota(jnp.int32, (tm, 1), 0)
    run = ext
    for g, w in enumerate(POOL_WINDOWS):
        run = run + pltpu.roll(run, w // 2, 0)
        cols = slice(g * POOL_GROUP, (g + 1) * POOL_GROUP)
        inv_count = 1.0 / jnp.minimum(pos + 1, w).astype(F32)
        d_ref[:, cols] = (run[POOL_HALO:, 0:POOL_GROUP] * inv_count - u[:, cols]).astype(BF16)
        run = run[:, POOL_GROUP:]

    zc = zpc[:, OFF_CONV:OFF_CONV + 3 * CONV_WIDTH]
    uc = zc[:, CONV_WIDTH:2 * CONV_WIDTH] * zc[:, 2 * CONV_WIDTH:3 * CONV_WIDTH]
    ext = jnp.concatenate([jnp.where(carry, chalo[...], 0.0), uc], axis=0)
    chalo[...] = uc[tm - CONV_HALO:tm, :]
    y = convw_ref[CONV_K - 1:CONV_K, :] * uc
    for j in range(CONV_K - 1):
        y = y + convw_ref[j:j + 1, :] * pltpu.roll(ext, CONV_K - 1 - j, 0)[CONV_HALO:, :]
    yc_ref[...] = (zc[:, 0:CONV_WIDTH] * y).astype(BF16)

    zqkv = _dot(h, w_ref[:, OFF_Q:OFF_GATE])
    for slot, (gain, o_ref) in enumerate(((qgain_ref, q_ref), (kgain_ref, k_ref), (None, v_ref))):
        emit_heads(zqkv[:, slot * ATTN_WIDTH:(slot + 1) * ATTN_WIDTH], slot, gain, o_ref)


def _in_proj(x2d, gain, w_mixers, qgain, kgain, convw, headmean, w_ff1, w_ff2, *, layer, batch, seq):
    t, dm = x2d.shape
    assert w_mixers.shape == (dm, OFF_GATE)
    tm = TM_IN
    tiles_per_seq = seq // tm
    assert seq % tm == 0 and all(tm % (16 * d) == 0 for _, d in ATTN_GROUPS)
    row = lambda width: pl.BlockSpec((tm, width), lambda i: (i, 0))
    qkv_spec = pl.BlockSpec((None, N_GROUPS, tm, GROUP_WIDTH),
                            lambda i: (i // tiles_per_seq, 0, i % tiles_per_seq, 0))
    qkv_shape = jax.ShapeDtypeStruct((batch, N_GROUPS, seq, GROUP_WIDTH), BF16)
    cast_in, cast_out, cast_shape, _ = _cast_jobs_specs([(w_ff1, layer, ()), (w_ff2, layer, ())], t // tm)
    return pl.pallas_call(
        functools.partial(_in_proj_kernel, tm=tm, tiles_per_seq=tiles_per_seq),
        grid=(t // tm,),
        in_specs=[row(dm), _resident((1, dm)), _resident(w_mixers.shape),
                  _resident((1, GROUP_WIDTH)), _resident((1, GROUP_WIDTH)), _resident((CONV_K, CONV_WIDTH)),
                  _resident((GROUP_WIDTH, GROUP_WIDTH)), *cast_in],
        out_specs=[row(POOL_WIDTH), row(CONV_WIDTH), qkv_spec, qkv_spec, qkv_spec] + cast_out,
        out_shape=[jax.ShapeDtypeStruct((t, POOL_WIDTH), BF16), jax.ShapeDtypeStruct((t, CONV_WIDTH), BF16),
                   qkv_shape, qkv_shape, qkv_shape] + cast_shape,
        scratch_shapes=[pltpu.VMEM((POOL_HALO, POOL_WIDTH), F32), pltpu.VMEM((CONV_HALO, CONV_WIDTH), F32)],
        compiler_params=pltpu.CompilerParams(dimension_semantics=("arbitrary",),
                                             vmem_limit_bytes=VMEM_LIMIT_BYTES),
        name="in_proj",
    )(x2d, gain, w_mixers, qgain, kgain, convw, headmean, w_ff1, w_ff2)


def _in_head_masks():
    lane_head = lax.broadcasted_iota(jnp.int32, (1, GROUP_WIDTH), 1) // HEAD_DIM
    return [lane_head == hd for hd in range(HEADS_PER_GROUP)]


def _attn_scores(qb, kb):
    q_heads = jnp.concatenate([jnp.where(m, qb, jnp.zeros_like(qb)) for m in _in_head_masks()], axis=0)
    return lax.dot_general(q_heads, kb, (((1,), (1,)), ((), ())), preferred_element_type=F32)


def _attn_softmax(s_ref, first):
    blk = ATTN_BLOCK
    row = lax.broadcasted_iota(jnp.int32, (blk, blk), 0)
    col = lax.broadcasted_iota(jnp.int32, (blk, blk), 1)
    mask_value = MASK_VALUE * LOG2_E
    probs, row_max, row_den = [], [], []
    for hd in range(HEADS_PER_GROUP):
        rows = slice(hd * blk, (hd + 1) * blk)

        def masked():
            if first:
                return jnp.where(col <= row, s_ref[rows, 0:blk], mask_value)
            return jnp.concatenate([jnp.where(col >= row, s_ref[rows, 0:blk], mask_value),
                                    jnp.where(col <= row, s_ref[rows, blk:2 * blk], mask_value)], axis=1)

        m = jnp.max(masked(), axis=-1, keepdims=True)
        p = jnp.exp2(masked() - m)
        row_max.append(m)
        row_den.append(jnp.sum(p, axis=-1, keepdims=True))
        probs.append(p.astype(BF16))
    return jnp.concatenate(probs, axis=0), row_max, row_den


def _attn_values(probs, row_max, row_den, vb):
    blk = ATTN_BLOCK
    in_head = _in_head_masks()
    lane = lax.broadcasted_iota(jnp.int32, (1, LANES), 1)
    pv = _dot(probs, vb)
    out, stats = pv[0:blk, :], jnp.ones((blk, LANES), F32)
    for hd in range(HEADS_PER_GROUP):
        if hd:
            out = jnp.where(in_head[hd], pv[hd * blk:(hd + 1) * blk, :], out)
        stats = jnp.where(lane == hd, row_max[hd], stats)
        stats = jnp.where(lane == HEADS_PER_GROUP + hd, row_den[hd], stats)
    return out, stats


def _block_row_slices(n, r, dilation, tile):
    chunk = tile // dilation
    piece = min(ATTN_BLOCK, chunk)
    slices = []
    for j in range(ATTN_BLOCK // piece):
        pos = n * ATTN_BLOCK + j * piece
        start = (pos // chunk) * tile + r * chunk + pos % chunk
        if not isinstance(start, int):
            start = pl.multiple_of(start, piece)
        slices.append(pl.ds(start, piece))
    return slices


def _block_rows(ref, n, r, dilation, tile):
    parts = [ref[rows, :] for rows in _block_row_slices(n, r, dilation, tile)]
    return parts[0] if len(parts) == 1 else jnp.concatenate(parts, axis=0)


def _store_block_rows(ref, n, r, dilation, tile, val):
    start = 0
    for rows in _block_row_slices(n, r, dilation, tile):
        ref[rows, :] = val[start:start + rows.size, :]
        start += rows.size


def _attn_group(dilation, q_ref, k_ref, v_ref, o_ref, lse_ref, score_s, *, seq, tile):
    blk = ATTN_BLOCK
    n_blocks = seq // dilation // blk
    assert n_blocks % 2 == 0

    def keys_values(ref, n, r, first):
        own = _block_rows(ref, n, r, dilation, tile)
        if first:
            return own
        return jnp.concatenate([_block_rows(ref, n - 1, r, dilation, tile), own], axis=0)

    width = 2 if n_blocks == 2 else ITEM_BLOCKS

    def pair(n, first):
        return tuple((n + i, first and i == 0) for i in range(width))

    def park_scores(slot, n, r, first):
        for i, (nb, fb) in enumerate(pair(n, first)):
            s = _attn_scores(_block_rows(q_ref, nb, r, dilation, tile), keys_values(k_ref, nb, r, fb))
            score_s[slot, i, :, 0:s.shape[1]] = s

    def finish(slot, n, r, first):
        blocks = pair(n, first)
        soft = [_attn_softmax(score_s.at[slot, i], fb) for i, (_, fb) in enumerate(blocks)]
        new = [_attn_values(*sm, keys_values(v_ref, nb, r, fb)) for sm, (nb, fb) in zip(soft, blocks)]
        for (nb, _), (out, lse) in zip(blocks, new):
            _store_block_rows(o_ref, nb, r, dilation, tile, out.astype(o_ref.dtype))
            _store_block_rows(lse_ref, nb, r, dilation, tile, lse)

    def run_items(n_items, item):
        assert n_items >= 2 and n_items % 2 == 0
        park_scores(0, *item(0, True))
        park_scores(1, *item(1, False))
        finish(0, *item(0, True))

        def double_step(t, c):
            i = 2 * t + 1
            park_scores(0, *item(i + 1, False))
            finish(1, *item(i, False))
            park_scores(1, *item(i + 2, False))
            finish(0, *item(i + 1, False))
            return c

        lax.fori_loop(0, (n_items - 2) // 2, double_step, 0)
        finish(1, *item(n_items - 1, False))

    if n_blocks == 2:
        run_items(dilation, lambda i, lead: (0, i, True))
    elif dilation == 1:
        run_items(n_blocks // width, lambda i, lead: (width * i, 0, lead))
    else:
        def per_residue(r, carry):
            run_items(n_blocks // width, lambda i, lead: (width * i, r, lead))
            return carry

        lax.fori_loop(0, dilation, per_residue, 0)


def _attn_kernel(q_ref, k_ref, v_ref, o_ref, lse_ref, score_s, *, seq, tile):
    g = pl.program_id(1)
    for gi, (_, dilation) in enumerate(ATTN_GROUPS):
        @pl.when(g == gi)
        def _(dilation=dilation):
            _attn_group(dilation, q_ref, k_ref, v_ref, o_ref, lse_ref, score_s, seq=seq, tile=tile)


def _attention(q, k, v, *, tile):
    batch, _, seq, _ = q.shape
    spec = lambda width: pl.BlockSpec((None, None, seq, width), lambda b, g: (b, g, 0, 0))
    return pl.pallas_call(
        functools.partial(_attn_kernel, seq=seq, tile=tile),
        grid=(batch, N_GROUPS),
        in_specs=[spec(GROUP_WIDTH)] * 3,
        out_specs=[spec(GROUP_WIDTH), spec(LANES)],
        out_shape=[jax.ShapeDtypeStruct((batch, N_GROUPS, seq, GROUP_WIDTH), BF16),
                   jax.ShapeDtypeStruct((batch, N_GROUPS, seq, LANES), F32)],
        scratch_shapes=[pltpu.VMEM((2, 2, HEADS_PER_GROUP * ATTN_BLOCK, 2 * ATTN_BLOCK), F32)],
        compiler_params=pltpu.CompilerParams(dimension_semantics=("arbitrary", "arbitrary"),
                                             vmem_limit_bytes=VMEM_LIMIT_BYTES),
        name="attention",
    )(q, k, v)


def _mix_attention_groups(o_ref, lse_ref):
    lane = lax.broadcasted_iota(jnp.int32, (1, LANES), 1)
    outs, lses, dens = [], [], []
    for g, (_, dilation) in enumerate(ATTN_GROUPS):
        o = _ungroup_rows(o_ref[g], dilation).astype(F32)
        lse = _ungroup_rows(lse_ref[g], dilation)
        den = jnp.where(lane < HEADS_PER_GROUP, pltpu.roll(lse, LANES - HEADS_PER_GROUP, 1), 1.0)
        outs.append(o)
        dens.append(den)
        lses.append(lse + jnp.log2(den))
    lse_max = functools.reduce(jnp.maximum, lses)
    weights = [jnp.exp2(lse - lse_max) for lse in lses]
    inv_sum = 1.0 / functools.reduce(jnp.add, weights)
    in_head = _in_head_masks()
    mixed = None
    for o, w, den in zip(outs, weights, dens):
        w = w * inv_sum / den
        w_lanes = None
        for hd in range(HEADS_PER_GROUP):
            col = jnp.sum(jnp.where(lane == hd, w, 0.0), axis=-1, keepdims=True)
            w_lanes = jnp.broadcast_to(col, o.shape) if w_lanes is None else jnp.where(in_head[hd], col, w_lanes)
        mixed = w_lanes * o if mixed is None else mixed + w_lanes * o
    return mixed


def _merge_kernel(x_ref, gain_ref, w_ref, bg_ref, d_ref, wmix_ref, pscale_ref, yc_ref, o_ref, lse_ref,
                  wp_ref, wc_ref, wa_ref, wo_ref, gain2_ref, xo_ref, h2_ref):
    dm = x_ref.shape[1]
    x = x_ref[...]
    h = _rms_norm(x, gain_ref[...]).astype(BF16)
    y_attn = _mix_attention_groups(o_ref, lse_ref).astype(BF16)

    y_pool = jnp.concatenate(
        [_dot(d_ref[:, g * POOL_GROUP:(g + 1) * POOL_GROUP], wmix_ref[g]) for g in range(len(POOL_WINDOWS))],
        axis=-1)
    y_pool = (y_pool * pscale_ref[...]).astype(BF16)

    merged = None
    for j, (y, wj_ref) in enumerate(((y_pool, wp_ref), (yc_ref[...], wc_ref), (y_attn, wa_ref))):
        gate_cols = slice(j * dm, (j + 1) * dm)
        gate = jax.nn.sigmoid(_dot(h, w_ref[:, gate_cols]) + bg_ref[:, gate_cols])
        term = gate * _dot(y, wj_ref[...])
        merged = term if merged is None else merged + term

    merged = merged.astype(BF16)
    part = x.shape[0] // TAIL_PARTS
    for p in range(TAIL_PARTS):
        rows = slice(p * part, (p + 1) * part)
        x_new = x[rows, :] + _dot(merged[rows, :], wo_ref[...])
        xo_ref[rows, :] = x_new
        h2_ref[rows, :] = _rms_norm(x_new, gain2_ref[...]).astype(BF16)


def _merge(x2d, gain, w_gates, b_gate, d, w_mix, pool_scale, yc, attn_o, attn_lse, w_pool_up, w_conv_out,
           w_attn_up, w_o, gain2):
    t, dm = x2d.shape
    tm = TM_MERGE
    seq = attn_o.shape[2]
    assert t % tm == 0 and seq % tm == 0 and TM_MERGE == TM_IN
    tiles_per_seq = seq // tm
    row = lambda width: pl.BlockSpec((tm, width), lambda i: (i, 0))
    groups = lambda width: pl.BlockSpec((None, N_GROUPS, tm, width),
                                        lambda i: (i // tiles_per_seq, 0, i % tiles_per_seq, 0))
    res = lambda w: _resident(w.shape)
    return pl.pallas_call(
        _merge_kernel,
        grid=(t // tm,),
        in_specs=[row(dm), _resident((1, dm)), res(w_gates), res(b_gate), row(POOL_WIDTH), res(w_mix),
                  _resident((1, POOL_WIDTH)), row(CONV_WIDTH), groups(GROUP_WIDTH), groups(LANES),
                  res(w_pool_up), res(w_conv_out), res(w_attn_up), res(w_o), _resident((1, dm))],
        out_specs=[row(dm), row(dm)],
        out_shape=[jax.ShapeDtypeStruct((t, dm), F32), jax.ShapeDtypeStruct((t, dm), BF16)],
        compiler_params=pltpu.CompilerParams(dimension_semantics=("arbitrary",),
                                             vmem_limit_bytes=VMEM_LIMIT_BYTES),
        name="merge",
    )(x2d, gain, w_gates, b_gate, d, w_mix, pool_scale, yc, attn_o, attn_lse, w_pool_up, w_conv_out, w_attn_up,
      w_o, gain2)


def _ffn_kernel(x_ref, h2_ref, w1_ref, w2_ref, *rest, cast_layout):
    o_ref = rest[len(cast_layout)]
    _run_cast_jobs(rest[:len(cast_layout)], rest[len(cast_layout) + 1:], cast_layout)
    d_ff = w1_ref.shape[1]
    h2 = h2_ref[...]
    acc = None
    for c in range(d_ff // FF_CHUNK):
        cols = slice(c * FF_CHUNK, (c + 1) * FF_CHUNK)
        a = jnp.square(jnp.maximum(_dot(h2, w1_ref[:, cols]), 0.0)).astype(BF16)
        part = _dot(a, w2_ref[cols, :])
        acc = part if acc is None else acc + part
    o_ref[...] = x_ref[...] + acc


def _ffn(x2d, h2, w1, w2, cast_jobs=()):
    t, dm = x2d.shape
    tm = TM_FFN
    assert t % tm == 0 and w1.shape[1] % FF_CHUNK == 0
    steps = t // tm
    row = pl.BlockSpec((tm, dm), lambda i: (i, 0))
    cast_in, cast_out, cast_shape, cast_layout = _cast_jobs_specs(cast_jobs, steps)
    return pl.pallas_call(
        functools.partial(_ffn_kernel, cast_layout=cast_layout),
        grid=(steps,),
        in_specs=[row, row, _resident(w1.shape), _resident(w2.shape), *cast_in],
        out_specs=[row] + cast_out,
        out_shape=[jax.ShapeDtypeStruct((t, dm), F32)] + cast_shape,
        compiler_params=pltpu.CompilerParams(dimension_semantics=("arbitrary",),
                                             vmem_limit_bytes=VMEM_LIMIT_BYTES),
        name="ffn",
    )(x2d, h2, w1, w2, *(w for w, _, _ in cast_jobs))


def kernel(x, norm_mix, w_in, b_gate, pool_mix, pool_scale, conv_w, q_gain, k_gain, w_pool_up, w_conv_out,
           w_attn_up, w_o, norm_mlp, w_ff1, w_ff2):
    batch, seq, dm = x.shape
    depth = norm_mix.shape[0]
    assert w_in.shape[2] == OFF_GATE + N_BRANCH * dm
    head_id = jnp.arange(GROUP_WIDTH) // HEAD_DIM
    headmean = jnp.where(head_id[:, None] == head_id[None, :], 1.0 / HEAD_DIM, 0.0).astype(BF16)
    x2d = x.reshape(batch * seq, dm)
    pool_mix = pool_mix.astype(BF16)
    up_weights = (w_pool_up, w_conv_out, w_attn_up, w_o)
    layer_jobs = lambda l: [(w_in, l, (OFF_GATE,))] + [(w, l, ()) for w in up_weights]
    w_mixers, w_gates, *ups = _cast_weights(layer_jobs(0))
    for l in range(depth):
        gain = norm_mix[l].reshape(1, dm)
        qgain = (jnp.tile(q_gain[l], HEADS_PER_GROUP) * (LOG2_E * HEAD_DIM ** -0.5)).reshape(1, GROUP_WIDTH)
        kgain = jnp.tile(k_gain[l], HEADS_PER_GROUP).reshape(1, GROUP_WIDTH)
        d, yc, q, k, v, w1, w2 = _in_proj(x2d, gain, w_mixers, qgain, kgain, conv_w[l], headmean, w_ff1, w_ff2,
                                          layer=l, batch=batch, seq=seq)
        attn_o, attn_lse = _attention(q, k, v, tile=TM_IN)
        x2d, h2 = _merge(x2d, gain, w_gates, b_gate[l].reshape(1, N_BRANCH * dm), d, pool_mix[l],
                         pool_scale[l].reshape(1, POOL_WIDTH), yc, attn_o, attn_lse, *ups,
                         norm_mlp[l].reshape(1, dm))
        if l + 1 < depth:
            x2d, w_mixers, w_gates, *ups = _ffn(x2d, h2, w1, w2, layer_jobs(l + 1))
        else:
            x2d, = _ffn(x2d, h2, w1, w2)
    return x2d.reshape(batch, seq, dm)
```

```python
import functools

import jax
import jax.numpy as jnp
from jax import lax
from jax.experimental import pallas as pl
from jax.experimental.pallas import tpu as pltpu

F32 = jnp.float32
BF16 = jnp.bfloat16

POOL_WINDOWS = (2, 4, 8, 16)
POOL_GROUP = 128
POOL_WIDTH = POOL_GROUP * len(POOL_WINDOWS)
CONV_WIDTH = 512
CONV_K = 3
HEAD_DIM = 64
ATTN_GROUPS = ((128, 1), (512, 4), (2048, 16))
HEADS_PER_GROUP = 4
GROUP_WIDTH = HEADS_PER_GROUP * HEAD_DIM
N_GROUPS = len(ATTN_GROUPS)
ATTN_WIDTH = N_GROUPS * GROUP_WIDTH
ATTN_BLOCK = 128
LANES = 128
BF16_ROWS = 16
N_BRANCH = 3
EPS = 1e-6
MASK_VALUE = -1e30
LOG2_E = 1.4426950408889634

OFF_POOL = 0
OFF_CONV = OFF_POOL + POOL_WIDTH
OFF_Q = OFF_CONV + 3 * CONV_WIDTH
OFF_K = OFF_Q + ATTN_WIDTH
OFF_V = OFF_K + ATTN_WIDTH
OFF_GATE = OFF_V + ATTN_WIDTH

POOL_HALO = 16
CONV_HALO = 8

VMEM_LIMIT_BYTES = 56 * 1024 * 1024
TM_IN = 1024
TM_MERGE = 1024
TM_FFN = 1024
FF_CHUNK = 2048
CAST_STEPS = 8
ITEM_BLOCKS = 2
CAST_GROUPS = 2


def _resident(shape):
    return pl.BlockSpec(shape, lambda *_: (0,) * len(shape), pipeline_mode=pl.Buffered(1))


def _cast_specs(w, layer, steps, bounds):
    _, rows, cols = w.shape
    assert rows % steps == 0 and (rows // steps) % BF16_ROWS == 0
    rb = rows // steps
    widths = [b - a for a, b in zip(bounds, bounds[1:])]
    fetched = bounds[-1] if bounds[0] == 0 else cols
    in_spec = pl.BlockSpec((None, rb, fetched), lambda i: (layer, i, 0))
    out_specs = [pl.BlockSpec((rb, wd), lambda i: (i, 0)) for wd in widths]
    out_shapes = [jax.ShapeDtypeStruct((rows, wd), BF16) for wd in widths]
    return in_spec, out_specs, out_shapes


def _cast_rows(src_ref, dst_refs, start=0):
    for dst in dst_refs:
        dst[...] = src_ref[:, start:start + dst.shape[1]].astype(BF16)
        start += dst.shape[1]


def _cast_jobs_specs(jobs, steps):
    in_specs, out_specs, out_shapes, layout = [], [], [], []
    for w, layer, bounds in jobs:
        in_spec, o_specs, o_shapes = _cast_specs(w, layer, steps, bounds)
        in_specs.append(in_spec)
        out_specs += o_specs
        out_shapes += o_shapes
        layout.append((len(o_specs), bounds[0]))
    return in_specs, out_specs, out_shapes, tuple(layout)


def _run_cast_jobs(srcs, dsts, layout):
    dsts = list(dsts)
    for src, (n_dst, first_col) in zip(srcs, layout):
        _cast_rows(src, [dsts.pop(0) for _ in range(n_dst)], first_col)


def _cast_kernel(*refs, layout):
    _run_cast_jobs(refs[:len(layout)], refs[len(layout):], layout)


def _cast_weights(jobs):
    in_specs, out_specs, out_shapes, layout = _cast_jobs_specs(jobs, CAST_STEPS)
    return pl.pallas_call(
        functools.partial(_cast_kernel, layout=layout),
        grid=(CAST_STEPS,),
        in_specs=in_specs,
        out_specs=out_specs,
        out_shape=out_shapes,
        compiler_params=pltpu.CompilerParams(dimension_semantics=("arbitrary",),
                                             vmem_limit_bytes=VMEM_LIMIT_BYTES),
        name="cast_weights",
    )(*(w for w, _, _ in jobs))


def _rms_norm(x, gain):
    ms = jnp.mean(x * x, axis=-1, keepdims=True)
    return x * lax.rsqrt(ms + EPS) * gain


def _dot(a, b):
    return jnp.dot(a, b, preferred_element_type=F32)


def _regroup_rows(z, dilation):
    if dilation == 1:
        return z
    rows, width = z.shape
    return pltpu.einshape("crl->rcl", z.reshape(rows // dilation, dilation, width)).reshape(rows, width)


def _ungroup_rows(z, dilation):
    if dilation == 1:
        return z
    rows, width = z.shape
    return pltpu.einshape("rcl->crl", z.reshape(dilation, rows // dilation, width)).reshape(rows, width)


def _in_proj_kernel(x_ref, gain_ref, w_ref, qgain_ref, kgain_ref, convw_ref, headmean_ref, *rest,
                    tm, tiles_per_seq, cast_layout):
    n_src = len(cast_layout)
    d_ref, yc_ref, q_ref, k_ref, v_ref = rest[n_src:n_src + 5]
    uhalo, chalo = rest[-2:]
    _run_cast_jobs(rest[:n_src], rest[n_src + 5:-2], cast_layout)
    i = pl.program_id(0)

    @pl.when(i == 0)
    def _():
        uhalo[...] = jnp.zeros_like(uhalo)
        chalo[...] = jnp.zeros_like(chalo)

    tile_in_seq = i % tiles_per_seq
    carry = tile_in_seq != 0
    h = _rms_norm(x_ref[...], gain_ref[...]).astype(BF16)

    def emit_heads(z_all, slot, gain, o_ref):
        zs = [z_all[:, g * GROUP_WIDTH:(g + 1) * GROUP_WIDTH] for g in range(N_GROUPS)]
        if gain is not None:
            ms = _dot(jnp.concatenate([(z * z).astype(BF16) for z in zs], axis=0), headmean_ref[...])
            zs = [z * lax.rsqrt(ms[g * tm:(g + 1) * tm, :] + EPS) * gain[...] for g, z in enumerate(zs)]
        for g, (_, dilation) in enumerate(ATTN_GROUPS):
            o_ref[g] = _regroup_rows(zs[g].astype(BF16), dilation)

    zpc = _dot(h, w_ref[:, OFF_POOL:OFF_Q])

    u = zpc[:, OFF_POOL:OFF_POOL + POOL_WIDTH]
    ext = jnp.concatenate([jnp.where(carry, uhalo[...], 0.0), u], axis=0)
    uhalo[...] = u[tm - POOL_HALO:tm, :]
    pos = tile_in_seq * tm + lax.broadcasted_iota(jnp.int32, (tm, 1), 0)
    run = ext
    for g, w in enumerate(POOL_WINDOWS):
        run = run + pltpu.roll(run, w // 2, 0)
        cols = slice(g * POOL_GROUP, (g + 1) * POOL_GROUP)
        inv_count = 1.0 / jnp.minimum(pos + 1, w).astype(F32)
        d_ref[:, cols] = (run[POOL_HALO:, 0:POOL_GROUP] * inv_count - u[:, cols]).astype(BF16)
        run = run[:, POOL_GROUP:]

    zc = zpc[:, OFF_CONV:OFF_CONV + 3 * CONV_WIDTH]
    uc = zc[:, CONV_WIDTH:2 * CONV_WIDTH] * zc[:, 2 * CONV_WIDTH:3 * CONV_WIDTH]
    ext = jnp.concatenate([jnp.where(carry, chalo[...], 0.0), uc], axis=0)
    chalo[...] = uc[tm - CONV_HALO:tm, :]
    y = convw_ref[CONV_K - 1:CONV_K, :] * uc
    for j in range(CONV_K - 1):
        y = y + convw_ref[j:j + 1, :] * pltpu.roll(ext, CONV_K - 1 - j, 0)[CONV_HALO:, :]
    yc_ref[...] = (zc[:, 0:CONV_WIDTH] * y).astype(BF16)

    zqkv = _dot(h, w_ref[:, OFF_Q:OFF_GATE])
    for slot, (gain, o_ref) in enumerate(((qgain_ref, q_ref), (kgain_ref, k_ref), (None, v_ref))):
        emit_heads(zqkv[:, slot * ATTN_WIDTH:(slot + 1) * ATTN_WIDTH], slot, gain, o_ref)


def _in_proj(x2d, gain, w_mixers, qgain, kgain, convw, headmean, cast_jobs=(), *, batch, seq):
    t, dm = x2d.shape
    assert w_mixers.shape == (dm, OFF_GATE)
    tm = TM_IN
    tiles_per_seq = seq // tm
    assert seq % tm == 0 and all(tm % (16 * d) == 0 for _, d in ATTN_GROUPS)
    row = lambda width: pl.BlockSpec((tm, width), lambda i: (i, 0))
    qkv_spec = pl.BlockSpec((None, N_GROUPS, tm, GROUP_WIDTH),
                            lambda i: (i // tiles_per_seq, 0, i % tiles_per_seq, 0))
    qkv_shape = jax.ShapeDtypeStruct((batch, N_GROUPS, seq, GROUP_WIDTH), BF16)
    cast_in, cast_out, cast_shape, cast_layout = _cast_jobs_specs(cast_jobs, t // tm)
    return pl.pallas_call(
        functools.partial(_in_proj_kernel, tm=tm, tiles_per_seq=tiles_per_seq, cast_layout=cast_layout),
        grid=(t // tm,),
        in_specs=[row(dm), _resident((1, dm)), _resident(w_mixers.shape),
                  _resident((1, GROUP_WIDTH)), _resident((1, GROUP_WIDTH)), _resident((CONV_K, CONV_WIDTH)),
                  _resident((GROUP_WIDTH, GROUP_WIDTH)), *cast_in],
        out_specs=[row(POOL_WIDTH), row(CONV_WIDTH), qkv_spec, qkv_spec, qkv_spec] + cast_out,
        out_shape=[jax.ShapeDtypeStruct((t, POOL_WIDTH), BF16), jax.ShapeDtypeStruct((t, CONV_WIDTH), BF16),
                   qkv_shape, qkv_shape, qkv_shape] + cast_shape,
        scratch_shapes=[pltpu.VMEM((POOL_HALO, POOL_WIDTH), F32), pltpu.VMEM((CONV_HALO, CONV_WIDTH), F32)],
        compiler_params=pltpu.CompilerParams(dimension_semantics=("arbitrary",),
                                             vmem_limit_bytes=VMEM_LIMIT_BYTES),
        name="in_proj",
    )(x2d, gain, w_mixers, qgain, kgain, convw, headmean, *(w for w, _, _ in cast_jobs))


def _in_head_masks():
    lane_head = lax.broadcasted_iota(jnp.int32, (1, GROUP_WIDTH), 1) // HEAD_DIM
    return [lane_head == hd for hd in range(HEADS_PER_GROUP)]


def _attn_scores(qb, kb):
    q_heads = jnp.concatenate([jnp.where(m, qb, jnp.zeros_like(qb)) for m in _in_head_masks()], axis=0)
    return lax.dot_general(q_heads, kb, (((1,), (1,)), ((), ())), preferred_element_type=F32)


def _attn_softmax(s_ref, first):
    blk = ATTN_BLOCK
    row = lax.broadcasted_iota(jnp.int32, (blk, blk), 0)
    col = lax.broadcasted_iota(jnp.int32, (blk, blk), 1)
    mask_value = MASK_VALUE * LOG2_E
    probs, row_max, row_den = [], [], []
    for hd in range(HEADS_PER_GROUP):
        rows = slice(hd * blk, (hd + 1) * blk)

        def masked():
            if first:
                return jnp.where(col <= row, s_ref[rows, 0:blk], mask_value)
            return jnp.concatenate([jnp.where(col >= row, s_ref[rows, 0:blk], mask_value),
                                    jnp.where(col <= row, s_ref[rows, blk:2 * blk], mask_value)], axis=1)

        m = jnp.max(masked(), axis=-1, keepdims=True)
        p = jnp.exp2(masked() - m)
        row_max.append(m)
        row_den.append(jnp.sum(p, axis=-1, keepdims=True))
        probs.append(p.astype(BF16))
    return jnp.concatenate(probs, axis=0), row_max, row_den


def _attn_values(probs, row_max, row_den, vb):
    blk = ATTN_BLOCK
    in_head = _in_head_masks()
    lane = lax.broadcasted_iota(jnp.int32, (1, LANES), 1)
    pv = _dot(probs, vb)
    out, stats = pv[0:blk, :], jnp.ones((blk, LANES), F32)
    for hd in range(HEADS_PER_GROUP):
        if hd:
            out = jnp.where(in_head[hd], pv[hd * blk:(hd + 1) * blk, :], out)
        stats = jnp.where(lane == hd, row_max[hd], stats)
        stats = jnp.where(lane == HEADS_PER_GROUP + hd, row_den[hd], stats)
    return out, stats


def _block_row_slices(n, r, dilation, tile):
    chunk = tile // dilation
    piece = min(ATTN_BLOCK, chunk)
    slices = []
    for j in range(ATTN_BLOCK // piece):
        pos = n * ATTN_BLOCK + j * piece
        start = (pos // chunk) * tile + r * chunk + pos % chunk
        if not isinstance(start, int):
            start = pl.multiple_of(start, piece)
        slices.append(pl.ds(start, piece))
    return slices


def _block_rows(ref, n, r, dilation, tile):
    parts = [ref[rows, :] for rows in _block_row_slices(n, r, dilation, tile)]
    return parts[0] if len(parts) == 1 else jnp.concatenate(parts, axis=0)


def _store_block_rows(ref, n, r, dilation, tile, val):
    start = 0
    for rows in _block_row_slices(n, r, dilation, tile):
        ref[rows, :] = val[start:start + rows.size, :]
        start += rows.size


def _attn_group(dilation, q_ref, k_ref, v_ref, o_ref, lse_ref, score_s, *, seq, tile):
    blk = ATTN_BLOCK
    n_blocks = seq // dilation // blk
    assert n_blocks % 2 == 0

    def keys_values(ref, n, r, first):
        own = _block_rows(ref, n, r, dilation, tile)
        if first:
            return own
        return jnp.concatenate([_block_rows(ref, n - 1, r, dilation, tile), own], axis=0)

    width = 2 if n_blocks == 2 else ITEM_BLOCKS

    def pair(n, first):
        return tuple((n + i, first and i == 0) for i in range(width))

    def park_scores(slot, n, r, first):
        for i, (nb, fb) in enumerate(pair(n, first)):
            s = _attn_scores(_block_rows(q_ref, nb, r, dilation, tile), keys_values(k_ref, nb, r, fb))
            score_s[slot, i, :, 0:s.shape[1]] = s

    def finish(slot, n, r, first):
        blocks = pair(n, first)
        soft = [_attn_softmax(score_s.at[slot, i], fb) for i, (_, fb) in enumerate(blocks)]
        new = [_attn_values(*sm, keys_values(v_ref, nb, r, fb)) for sm, (nb, fb) in zip(soft, blocks)]
        for (nb, _), (out, lse) in zip(blocks, new):
            _store_block_rows(o_ref, nb, r, dilation, tile, out.astype(o_ref.dtype))
            _store_block_rows(lse_ref, nb, r, dilation, tile, lse)

    def run_items(n_items, item):
        assert n_items >= 2 and n_items % 2 == 0
        park_scores(0, *item(0, True))
        park_scores(1, *item(1, False))
        finish(0, *item(0, True))

        def double_step(t, c):
            i = 2 * t + 1
            park_scores(0, *item(i + 1, False))
            finish(1, *item(i, False))
            park_scores(1, *item(i + 2, False))
            finish(0, *item(i + 1, False))
            return c

        lax.fori_loop(0, (n_items - 2) // 2, double_step, 0)
        finish(1, *item(n_items - 1, False))

    if n_blocks == 2:
        run_items(dilation, lambda i, lead: (0, i, True))
    elif dilation == 1:
        run_items(n_blocks // width, lambda i, lead: (width * i, 0, lead))
    else:
        def per_residue(r, carry):
            run_items(n_blocks // width, lambda i, lead: (width * i, r, lead))
            return carry

        lax.fori_loop(0, dilation, per_residue, 0)


def _attn_kernel(q_ref, k_ref, v_ref, wff1_src, wff2_src, o_ref, lse_ref, wff1_dst, wff2_dst, score_s, *,
                 seq, tile):
    g = pl.program_id(1)

    @pl.when(g < CAST_GROUPS)
    def _():
        _cast_rows(wff1_src, [wff1_dst])
        _cast_rows(wff2_src, [wff2_dst])

    for gi, (_, dilation) in enumerate(ATTN_GROUPS):
        @pl.when(g == gi)
        def _(dilation=dilation):
            _attn_group(dilation, q_ref, k_ref, v_ref, o_ref, lse_ref, score_s, seq=seq, tile=tile)


def _attention(q, k, v, w_ff1, w_ff2, *, layer, tile):
    batch, _, seq, _ = q.shape
    spec = lambda width: pl.BlockSpec((None, None, seq, width), lambda b, g: (b, g, 0, 0))
    cast_step = lambda b, g: b * CAST_GROUPS + jnp.minimum(g, CAST_GROUPS - 1)
    cast_in, cast_out, cast_shape = [], [], []
    for w in (w_ff1, w_ff2):
        _, rows, cols = w.shape
        rb = rows // (batch * CAST_GROUPS)
        assert rows % (batch * CAST_GROUPS) == 0 and rb % BF16_ROWS == 0
        cast_in.append(pl.BlockSpec((None, rb, cols), lambda b, g: (layer, cast_step(b, g), 0)))
        cast_out.append(pl.BlockSpec((rb, cols), lambda b, g: (cast_step(b, g), 0)))
        cast_shape.append(jax.ShapeDtypeStruct((rows, cols), BF16))
    return pl.pallas_call(
        functools.partial(_attn_kernel, seq=seq, tile=tile),
        grid=(batch, N_GROUPS),
        in_specs=[spec(GROUP_WIDTH)] * 3 + cast_in,
        out_specs=[spec(GROUP_WIDTH), spec(LANES)] + cast_out,
        out_shape=[jax.ShapeDtypeStruct((batch, N_GROUPS, seq, GROUP_WIDTH), BF16),
                   jax.ShapeDtypeStruct((batch, N_GROUPS, seq, LANES), F32)] + cast_shape,
        scratch_shapes=[pltpu.VMEM((2, 2, HEADS_PER_GROUP * ATTN_BLOCK, 2 * ATTN_BLOCK), F32)],
        compiler_params=pltpu.CompilerParams(dimension_semantics=("arbitrary", "arbitrary"),
                                             vmem_limit_bytes=VMEM_LIMIT_BYTES),
        name="attention",
    )(q, k, v, w_ff1, w_ff2)


def _mix_attention_groups(o_ref, lse_ref):
    lane = lax.broadcasted_iota(jnp.int32, (1, LANES), 1)
    outs, lses, dens = [], [], []
    for g, (_, dilation) in enumerate(ATTN_GROUPS):
        o = _ungroup_rows(o_ref[g], dilation).astype(F32)
        lse = _ungroup_rows(lse_ref[g], dilation)
        den = jnp.where(lane < HEADS_PER_GROUP, pltpu.roll(lse, LANES - HEADS_PER_GROUP, 1), 1.0)
        outs.append(o)
        dens.append(den)
        lses.append(lse + jnp.log2(den))
    lse_max = functools.reduce(jnp.maximum, lses)
    weights = [jnp.exp2(lse - lse_max) for lse in lses]
    inv_sum = 1.0 / functools.reduce(jnp.add, weights)
    in_head = _in_head_masks()
    mixed = None
    for o, w, den in zip(outs, weights, dens):
        w = w * inv_sum / den
        w_lanes = None
        for hd in range(HEADS_PER_GROUP):
            col = jnp.sum(jnp.where(lane == hd, w, 0.0), axis=-1, keepdims=True)
            w_lanes = jnp.broadcast_to(col, o.shape) if w_lanes is None else jnp.where(in_head[hd], col, w_lanes)
        mixed = w_lanes * o if mixed is None else mixed + w_lanes * o
    return mixed


def _merge_kernel(x_ref, gain_ref, w_ref, bg_ref, d_ref, wmix_ref, pscale_ref, yc_ref, o_ref, lse_ref,
                  wp_ref, wc_ref, wa_ref, wo_ref, gain2_ref, xo_ref, h2_ref):
    dm = x_ref.shape[1]
    x = x_ref[...]
    h = _rms_norm(x, gain_ref[...]).astype(BF16)
    y_attn = _mix_attention_groups(o_ref, lse_ref).astype(BF16)

    y_pool = jnp.concatenate(
        [_dot(d_ref[:, g * POOL_GROUP:(g + 1) * POOL_GROUP], wmix_ref[g]) for g in range(len(POOL_WINDOWS))],
        axis=-1)
    y_pool = (y_pool * pscale_ref[...]).astype(BF16)

    merged = None
    for j, (y, wj_ref) in enumerate(((y_pool, wp_ref), (yc_ref[...], wc_ref), (y_attn, wa_ref))):
        gate_cols = slice(j * dm, (j + 1) * dm)
        gate = jax.nn.sigmoid(_dot(h, w_ref[:, gate_cols]) + bg_ref[:, gate_cols])
        term = gate * _dot(y, wj_ref[...])
        merged = term if merged is None else merged + term

    x_new = x + _dot(merged.astype(BF16), wo_ref[...])
    xo_ref[...] = x_new
    h2_ref[...] = _rms_norm(x_new, gain2_ref[...]).astype(BF16)


def _merge(x2d, gain, w_gates, b_gate, d, w_mix, pool_scale, yc, attn_o, attn_lse, w_pool_up, w_conv_out,
           w_attn_up, w_o, gain2):
    t, dm = x2d.shape
    tm = TM_MERGE
    seq = attn_o.shape[2]
    assert t % tm == 0 and seq % tm == 0 and TM_MERGE == TM_IN
    tiles_per_seq = seq // tm
    row = lambda width: pl.BlockSpec((tm, width), lambda i: (i, 0))
    groups = lambda width: pl.BlockSpec((None, N_GROUPS, tm, width),
                                        lambda i: (i // tiles_per_seq, 0, i % tiles_per_seq, 0))
    res = lambda w: _resident(w.shape)
    return pl.pallas_call(
        _merge_kernel,
        grid=(t // tm,),
        in_specs=[row(dm), _resident((1, dm)), res(w_gates), res(b_gate), row(POOL_WIDTH), res(w_mix),
                  _resident((1, POOL_WIDTH)), row(CONV_WIDTH), groups(GROUP_WIDTH), groups(LANES),
                  res(w_pool_up), res(w_conv_out), res(w_attn_up), res(w_o), _resident((1, dm))],
        out_specs=[row(dm), row(dm)],
        out_shape=[jax.ShapeDtypeStruct((t, dm), F32), jax.ShapeDtypeStruct((t, dm), BF16)],
        compiler_params=pltpu.CompilerParams(dimension_semantics=("arbitrary",),
                                             vmem_limit_bytes=VMEM_LIMIT_BYTES),
        name="merge",
    )(x2d, gain, w_gates, b_gate, d, w_mix, pool_scale, yc, attn_o, attn_lse, w_pool_up, w_conv_out, w_attn_up,
      w_o, gain2)


def _ffn_kernel(x_ref, h2_ref, w1_ref, w2_ref, *rest, cast_layout):
    o_ref = rest[len(cast_layout)]
    _run_cast_jobs(rest[:len(cast_layout)], rest[len(cast_layout) + 1:], cast_layout)
    d_ff = w1_ref.shape[1]
    h2 = h2_ref[...]
    acc = None
    for c in range(d_ff // FF_CHUNK):
        cols = slice(c * FF_CHUNK, (c + 1) * FF_CHUNK)
        a = jnp.square(jnp.maximum(_dot(h2, w1_ref[:, cols]), 0.0)).astype(BF16)
        part = _dot(a, w2_ref[cols, :])
        acc = part if acc is None else acc + part
    o_ref[...] = x_ref[...] + acc


def _ffn(x2d, h2, w1, w2, cast_jobs=()):
    t, dm = x2d.shape
    tm = TM_FFN
    assert t % tm == 0 and w1.shape[1] % FF_CHUNK == 0
    steps = t // tm
    row = pl.BlockSpec((tm, dm), lambda i: (i, 0))
    cast_in, cast_out, cast_shape, cast_layout = _cast_jobs_specs(cast_jobs, steps)
    return pl.pallas_call(
        functools.partial(_ffn_kernel, cast_layout=cast_layout),
        grid=(steps,),
        in_specs=[row, row, _resident(w1.shape), _resident(w2.shape), *cast_in],
        out_specs=[row] + cast_out,
        out_shape=[jax.ShapeDtypeStruct((t, dm), F32)] + cast_shape,
        compiler_params=pltpu.CompilerParams(dimension_semantics=("arbitrary",),
                                             vmem_limit_bytes=VMEM_LIMIT_BYTES),
        name="ffn",
    )(x2d, h2, w1, w2, *(w for w, _, _ in cast_jobs))


def kernel(x, norm_mix, w_in, b_gate, pool_mix, pool_scale, conv_w, q_gain, k_gain, w_pool_up, w_conv_out,
           w_attn_up, w_o, norm_mlp, w_ff1, w_ff2):
    batch, seq, dm = x.shape
    depth = norm_mix.shape[0]
    assert w_in.shape[2] == OFF_GATE + N_BRANCH * dm
    head_id = jnp.arange(GROUP_WIDTH) // HEAD_DIM
    headmean = jnp.where(head_id[:, None] == head_id[None, :], 1.0 / HEAD_DIM, 0.0).astype(BF16)
    x2d = x.reshape(batch * seq, dm)
    pool_mix = pool_mix.astype(BF16)
    up_weights = (w_pool_up, w_conv_out, w_attn_up, w_o)
    in_cols = w_in.shape[2]
    late_jobs = lambda l: [(w_in, l, (OFF_GATE, in_cols))] + [(w, l, (0, w.shape[2])) for w in up_weights]
    layer_jobs = lambda l: [(w_in, l, (0, OFF_GATE, in_cols))] + late_jobs(l)[1:]
    w_mixers, = _cast_weights([(w_in, 0, (0, OFF_GATE))])
    for l in range(depth):
        gain = norm_mix[l].reshape(1, dm)
        qgain = (jnp.tile(q_gain[l], HEADS_PER_GROUP) * (LOG2_E * HEAD_DIM ** -0.5)).reshape(1, GROUP_WIDTH)
        kgain = jnp.tile(k_gain[l], HEADS_PER_GROUP).reshape(1, GROUP_WIDTH)
        d, yc, q, k, v, *late = _in_proj(x2d, gain, w_mixers, qgain, kgain, conv_w[l], headmean,
                                         late_jobs(0) if l == 0 else (), batch=batch, seq=seq)
        if l == 0:
            w_gates, *ups = late
        attn_o, attn_lse, w1, w2 = _attention(q, k, v, w_ff1, w_ff2, layer=l, tile=TM_IN)
        x2d, h2 = _merge(x2d, gain, w_gates, b_gate[l].reshape(1, N_BRANCH * dm), d, pool_mix[l],
                         pool_scale[l].reshape(1, POOL_WIDTH), yc, attn_o, attn_lse, *ups,
                         norm_mlp[l].reshape(1, dm))
        if l + 1 < depth:
            x2d, w_mixers, w_gates, *ups = _ffn(x2d, h2, w1, w2, layer_jobs(l + 1))
        else:
            x2d, = _ffn(x2d, h2, w1, w2)
    return x2d.reshape(batch, seq, dm)
```

```python
import functools

import jax
import jax.numpy as jnp
from jax import lax
from jax.experimental import pallas as pl
from jax.experimental.pallas import tpu as pltpu

F32 = jnp.float32
BF16 = jnp.bfloat16

POOL_WINDOWS = (2, 4, 8, 16)
POOL_GROUP = 128
POOL_WIDTH = POOL_GROUP * len(POOL_WINDOWS)
CONV_WIDTH = 512
CONV_K = 3
HEAD_DIM = 64
ATTN_GROUPS = ((128, 1), (512, 4), (2048, 16))
HEADS_PER_GROUP = 4
GROUP_WIDTH = HEADS_PER_GROUP * HEAD_DIM
N_GROUPS = len(ATTN_GROUPS)
ATTN_WIDTH = N_GROUPS * GROUP_WIDTH
ATTN_BLOCK = 128
LANES = 128
BF16_ROWS = 16
N_BRANCH = 3
EPS = 1e-6
MASK_VALUE = -1e30
LOG2_E = 1.4426950408889634

OFF_POOL = 0
OFF_CONV = OFF_POOL + POOL_WIDTH
OFF_Q = OFF_CONV + 3 * CONV_WIDTH
OFF_K = OFF_Q + ATTN_WIDTH
OFF_V = OFF_K + ATTN_WIDTH
OFF_GATE = OFF_V + ATTN_WIDTH

POOL_HALO = 16
CONV_HALO = 8

VMEM_LIMIT_BYTES = 56 * 1024 * 1024
TM_IN = 1024
TM_MERGE = 1024
TM_FFN = 1024
FF_CHUNK = 2048
CAST_STEPS = 8
ITEM_BLOCKS = 2
CAST_GROUPS = 2


def _resident(shape):
    return pl.BlockSpec(shape, lambda *_: (0,) * len(shape), pipeline_mode=pl.Buffered(1))


def _cast_specs(w, layer, steps, bounds):
    _, rows, cols = w.shape
    assert rows % steps == 0 and (rows // steps) % BF16_ROWS == 0
    rb = rows // steps
    widths = [b - a for a, b in zip(bounds, bounds[1:])]
    fetched = bounds[-1] if bounds[0] == 0 else cols
    in_spec = pl.BlockSpec((None, rb, fetched), lambda i: (layer, i, 0))
    out_specs = [pl.BlockSpec((rb, wd), lambda i: (i, 0)) for wd in widths]
    out_shapes = [jax.ShapeDtypeStruct((rows, wd), BF16) for wd in widths]
    return in_spec, out_specs, out_shapes


def _cast_rows(src_ref, dst_refs, start=0):
    for dst in dst_refs:
        dst[...] = src_ref[:, start:start + dst.shape[1]].astype(BF16)
        start += dst.shape[1]


def _cast_jobs_specs(jobs, steps):
    in_specs, out_specs, out_shapes, layout = [], [], [], []
    for w, layer, bounds in jobs:
        in_spec, o_specs, o_shapes = _cast_specs(w, layer, steps, bounds)
        in_specs.append(in_spec)
        out_specs += o_specs
        out_shapes += o_shapes
        layout.append((len(o_specs), bounds[0]))
    return in_specs, out_specs, out_shapes, tuple(layout)


def _run_cast_jobs(srcs, dsts, layout):
    dsts = list(dsts)
    for src, (n_dst, first_col) in zip(srcs, layout):
        _cast_rows(src, [dsts.pop(0) for _ in range(n_dst)], first_col)


def _cast_kernel(*refs, layout):
    _run_cast_jobs(refs[:len(layout)], refs[len(layout):], layout)


def _cast_weights(jobs):
    in_specs, out_specs, out_shapes, layout = _cast_jobs_specs(jobs, CAST_STEPS)
    return pl.pallas_call(
        functools.partial(_cast_kernel, layout=layout),
        grid=(CAST_STEPS,),
        in_specs=in_specs,
        out_specs=out_specs,
        out_shape=out_shapes,
        compiler_params=pltpu.CompilerParams(dimension_semantics=("arbitrary",),
                                             vmem_limit_bytes=VMEM_LIMIT_BYTES),
        name="cast_weights",
    )(*(w for w, _, _ in jobs))


def _rms_norm(x, gain):
    ms = jnp.mean(x * x, axis=-1, keepdims=True)
    return x * lax.rsqrt(ms + EPS) * gain


def _dot(a, b):
    return jnp.dot(a, b, preferred_element_type=F32)


def _regroup_rows(z, dilation):
    if dilation == 1:
        return z
    rows, width = z.shape
    return pltpu.einshape("crl->rcl", z.reshape(rows // dilation, dilation, width)).reshape(rows, width)


def _ungroup_rows(z, dilation):
    if dilation == 1:
        return z
    rows, width = z.shape
    return pltpu.einshape("rcl->crl", z.reshape(dilation, rows // dilation, width)).reshape(rows, width)


def _in_proj_kernel(x_ref, gain_ref, w_ref, qgain_ref, kgain_ref, convw_ref, headmean_ref, *rest,
                    tm, tiles_per_seq, cast_layout):
    n_src = len(cast_layout)
    d_ref, yc_ref, q_ref, k_ref, v_ref = rest[n_src:n_src + 5]
    uhalo, chalo = rest[-2:]
    _run_cast_jobs(rest[:n_src], rest[n_src + 5:-2], cast_layout)
    i = pl.program_id(0)

    @pl.when(i == 0)
    def _():
        uhalo[...] = jnp.zeros_like(uhalo)
        chalo[...] = jnp.zeros_like(chalo)

    tile_in_seq = i % tiles_per_seq
    carry = tile_in_seq != 0
    h = _rms_norm(x_ref[...], gain_ref[...]).astype(BF16)

    def emit_heads(z_all, slot, gain, o_ref):
        zs = [z_all[:, g * GROUP_WIDTH:(g + 1) * GROUP_WIDTH] for g in range(N_GROUPS)]
        if gain is not None:
            ms = _dot(jnp.concatenate([(z * z).astype(BF16) for z in zs], axis=0), headmean_ref[...])
            zs = [z * lax.rsqrt(ms[g * tm:(g + 1) * tm, :] + EPS) * gain[...] for g, z in enumerate(zs)]
        for g, (_, dilation) in enumerate(ATTN_GROUPS):
            o_ref[g] = _regroup_rows(zs[g].astype(BF16), dilation)

    zpc = _dot(h, w_ref[:, OFF_POOL:OFF_Q])

    u = zpc[:, OFF_POOL:OFF_POOL + POOL_WIDTH]
    ext = jnp.concatenate([jnp.where(carry, uhalo[...], 0.0), u], axis=0)
    uhalo[...] = u[tm - POOL_HALO:tm, :]
    pos = tile_in_seq * tm + lax.broadcasted_iota(jnp.int32, (tm, 1), 0)
    run = ext
    for g, w in enumerate(POOL_WINDOWS):
        run = run + pltpu.roll(run, w // 2, 0)
        cols = slice(g * POOL_GROUP, (g + 1) * POOL_GROUP)
        inv_count = 1.0 / jnp.minimum(pos + 1, w).astype(F32)
        d_ref[:, cols] = (run[POOL_HALO:, 0:POOL_GROUP] * inv_count - u[:, cols]).astype(BF16)
        run = run[:, POOL_GROUP:]

    zc = zpc[:, OFF_CONV:OFF_CONV + 3 * CONV_WIDTH]
    uc = zc[:, CONV_WIDTH:2 * CONV_WIDTH] * zc[:, 2 * CONV_WIDTH:3 * CONV_WIDTH]
    ext = jnp.concatenate([jnp.where(carry, chalo[...], 0.0), uc], axis=0)
    chalo[...] = uc[tm - CONV_HALO:tm, :]
    y = convw_ref[CONV_K - 1:CONV_K, :] * uc
    for j in range(CONV_K - 1):
        y = y + convw_ref[j:j + 1, :] * pltpu.roll(ext, CONV_K - 1 - j, 0)[CONV_HALO:, :]
    yc_ref[...] = (zc[:, 0:CONV_WIDTH] * y).astype(BF16)

    zqkv = _dot(h, w_ref[:, OFF_Q:OFF_GATE])
    for slot, (gain, o_ref) in enumerate(((qgain_ref, q_ref), (kgain_ref, k_ref), (None, v_ref))):
        emit_heads(zqkv[:, slot * ATTN_WIDTH:(slot + 1) * ATTN_WIDTH], slot, gain, o_ref)


def _in_proj(x2d, gain, w_mixers, qgain, kgain, convw, headmean, cast_jobs=(), *, batch, seq):
    t, dm = x2d.shape
    assert w_mixers.shape == (dm, OFF_GATE)
    tm = TM_IN
    tiles_per_seq = seq // tm
    assert seq % tm == 0 and all(tm % (16 * d) == 0 for _, d in ATTN_GROUPS)
    row = lambda width: pl.BlockSpec((tm, width), lambda i: (i, 0))
    qkv_spec = pl.BlockSpec((None, N_GROUPS, tm, GROUP_WIDTH),
                            lambda i: (i // tiles_per_seq, 0, i % tiles_per_seq, 0))
    qkv_shape = jax.ShapeDtypeStruct((batch, N_GROUPS, seq, GROUP_WIDTH), BF16)
    cast_in, cast_out, cast_shape, cast_layout = _cast_jobs_specs(cast_jobs, t // tm)
    return pl.pallas_call(
        functools.partial(_in_proj_kernel, tm=tm, tiles_per_seq=tiles_per_seq, cast_layout=cast_layout),
        grid=(t // tm,),
        in_specs=[row(dm), _resident((1, dm)), _resident(w_mixers.shape),
                  _resident((1, GROUP_WIDTH)), _resident((1, GROUP_WIDTH)), _resident((CONV_K, CONV_WIDTH)),
                  _resident((GROUP_WIDTH, GROUP_WIDTH)), *cast_in],
        out_specs=[row(POOL_WIDTH), row(CONV_WIDTH), qkv_spec, qkv_spec, qkv_spec] + cast_out,
        out_shape=[jax.ShapeDtypeStruct((t, POOL_WIDTH), BF16), jax.ShapeDtypeStruct((t, CONV_WIDTH), BF16),
                   qkv_shape, qkv_shape, qkv_shape] + cast_shape,
        scratch_shapes=[pltpu.VMEM((POOL_HALO, POOL_WIDTH), F32), pltpu.VMEM((CONV_HALO, CONV_WIDTH), F32)],
        compiler_params=pltpu.CompilerParams(dimension_semantics=("arbitrary",),
                                             vmem_limit_bytes=VMEM_LIMIT_BYTES),
        name="in_proj",
    )(x2d, gain, w_mixers, qgain, kgain, convw, headmean, *(w for w, _, _ in cast_jobs))


def _in_head_masks():
    lane_head = lax.broadcasted_iota(jnp.int32, (1, GROUP_WIDTH), 1) // HEAD_DIM
    return [lane_head == hd for hd in range(HEADS_PER_GROUP)]


def _attn_scores(qb, kb):
    q_heads = jnp.concatenate([jnp.where(m, qb, jnp.zeros_like(qb)) for m in _in_head_masks()], axis=0)
    return lax.dot_general(q_heads, kb, (((1,), (1,)), ((), ())), preferred_element_type=F32)


def _attn_softmax(s_ref, first):
    blk = ATTN_BLOCK
    row = lax.broadcasted_iota(jnp.int32, (blk, blk), 0)
    col = lax.broadcasted_iota(jnp.int32, (blk, blk), 1)
    mask_value = MASK_VALUE * LOG2_E
    probs, row_max, row_den = [], [], []
    for hd in range(HEADS_PER_GROUP):
        rows = slice(hd * blk, (hd + 1) * blk)

        def masked():
            if first:
                return jnp.where(col <= row, s_ref[rows, 0:blk], mask_value)
            return jnp.concatenate([jnp.where(col >= row, s_ref[rows, 0:blk], mask_value),
                                    jnp.where(col <= row, s_ref[rows, blk:2 * blk], mask_value)], axis=1)

        m = jnp.max(masked(), axis=-1, keepdims=True)
        p = jnp.exp2(masked() - m)
        row_max.append(m)
        row_den.append(jnp.sum(p, axis=-1, keepdims=True))
        probs.append(p.astype(BF16))
    return jnp.concatenate(probs, axis=0), row_max, row_den


def _attn_values(probs, row_max, row_den, vb):
    blk = ATTN_BLOCK
    in_head = _in_head_masks()
    lane = lax.broadcasted_iota(jnp.int32, (1, LANES), 1)
    pv = _dot(probs, vb)
    out, stats = pv[0:blk, :], jnp.ones((blk, LANES), F32)
    for hd in range(HEADS_PER_GROUP):
        if hd:
            out = jnp.where(in_head[hd], pv[hd * blk:(hd + 1) * blk, :], out)
        stats = jnp.where(lane == hd, row_max[hd], stats)
        stats = jnp.where(lane == HEADS_PER_GROUP + hd, row_den[hd], stats)
    return out, stats


def _block_row_slices(n, r, dilation, tile):
    chunk = tile // dilation
    piece = min(ATTN_BLOCK, chunk)
    slices = []
    for j in range(ATTN_BLOCK // piece):
        pos = n * ATTN_BLOCK + j * piece
        start = (pos // chunk) * tile + r * chunk + pos % chunk
        if not isinstance(start, int):
            start = pl.multiple_of(start, piece)
        slices.append(pl.ds(start, piece))
    return slices


def _block_rows(ref, n, r, dilation, tile):
    parts = [ref[rows, :] for rows in _block_row_slices(n, r, dilation, tile)]
    return parts[0] if len(parts) == 1 else jnp.concatenate(parts, axis=0)


def _store_block_rows(ref, n, r, dilation, tile, val):
    start = 0
    for rows in _block_row_slices(n, r, dilation, tile):
        ref[rows, :] = val[start:start + rows.size, :]
        start += rows.size


def _attn_group(dilation, q_ref, k_ref, v_ref, o_ref, lse_ref, score_s, *, seq, tile):
    blk = ATTN_BLOCK
    n_blocks = seq // dilation // blk
    assert n_blocks % 2 == 0

    def keys_values(ref, n, r, first):
        own = _block_rows(ref, n, r, dilation, tile)
        if first:
            return own
        return jnp.concatenate([_block_rows(ref, n - 1, r, dilation, tile), own], axis=0)

    width = 2 if n_blocks == 2 else ITEM_BLOCKS

    def pair(n, first):
        return tuple((n + i, first and i == 0) for i in range(width))

    def park_scores(slot, n, r, first):
        for i, (nb, fb) in enumerate(pair(n, first)):
            s = _attn_scores(_block_rows(q_ref, nb, r, dilation, tile), keys_values(k_ref, nb, r, fb))
            score_s[slot, i, :, 0:s.shape[1]] = s

    def finish(slot, n, r, first):
        blocks = pair(n, first)
        soft = [_attn_softmax(score_s.at[slot, i], fb) for i, (_, fb) in enumerate(blocks)]
        new = [_attn_values(*sm, keys_values(v_ref, nb, r, fb)) for sm, (nb, fb) in zip(soft, blocks)]
        for (nb, _), (out, lse) in zip(blocks, new):
            _store_block_rows(o_ref, nb, r, dilation, tile, out.astype(o_ref.dtype))
            _store_block_rows(lse_ref, nb, r, dilation, tile, lse)

    def run_items(n_items, item):
        assert n_items >= 2 and n_items % 2 == 0
        park_scores(0, *item(0, True))
        park_scores(1, *item(1, False))
        finish(0, *item(0, True))

        def double_step(t, c):
            i = 2 * t + 1
            park_scores(0, *item(i + 1, False))
            finish(1, *item(i, False))
            park_scores(1, *item(i + 2, False))
            finish(0, *item(i + 1, False))
            return c

        lax.fori_loop(0, (n_items - 2) // 2, double_step, 0)
        finish(1, *item(n_items - 1, False))

    if n_blocks == 2:
        run_items(dilation, lambda i, lead: (0, i, True))
    elif dilation == 1:
        run_items(n_blocks // width, lambda i, lead: (width * i, 0, lead))
    else:
        def per_residue(r, carry):
            run_items(n_blocks // width, lambda i, lead: (width * i, r, lead))
            return carry

        lax.fori_loop(0, dilation, per_residue, 0)


def _attn_kernel(q_ref, k_ref, v_ref, wff1_src, wff2_src, o_ref, lse_ref, wff1_dst, wff2_dst, score_s, *,
                 seq, tile):
    g = pl.program_id(1)

    @pl.when(g < CAST_GROUPS)
    def _():
        _cast_rows(wff1_src, [wff1_dst])
        _cast_rows(wff2_src, [wff2_dst])

    for gi, (_, dilation) in enumerate(ATTN_GROUPS):
        @pl.when(g == gi)
        def _(dilation=dilation):
            _attn_group(dilation, q_ref, k_ref, v_ref, o_ref, lse_ref, score_s, seq=seq, tile=tile)


def _attention(q, k, v, w_ff1, w_ff2, *, layer, tile):
    batch, _, seq, _ = q.shape
    spec = lambda width: pl.BlockSpec((None, None, seq, width), lambda b, g: (b, g, 0, 0))
    cast_step = lambda b, g: b * CAST_GROUPS + jnp.minimum(g, CAST_GROUPS - 1)
    cast_in, cast_out, cast_shape = [], [], []
    for w in (w_ff1, w_ff2):
        _, rows, cols = w.shape
        rb = rows // (batch * CAST_GROUPS)
        assert rows % (batch * CAST_GROUPS) == 0 and rb % BF16_ROWS == 0
        cast_in.append(pl.BlockSpec((None, rb, cols), lambda b, g: (layer, cast_step(b, g), 0)))
        cast_out.append(pl.BlockSpec((rb, cols), lambda b, g: (cast_step(b, g), 0)))
        cast_shape.append(jax.ShapeDtypeStruct((rows, cols), BF16))
    return pl.pallas_call(
        functools.partial(_attn_kernel, seq=seq, tile=tile),
        grid=(batch, N_GROUPS),
        in_specs=[spec(GROUP_WIDTH)] * 3 + cast_in,
        out_specs=[spec(GROUP_WIDTH), spec(LANES)] + cast_out,
        out_shape=[jax.ShapeDtypeStruct((batch, N_GROUPS, seq, GROUP_WIDTH), BF16),
                   jax.ShapeDtypeStruct((batch, N_GROUPS, seq, LANES), F32)] + cast_shape,
        scratch_shapes=[pltpu.VMEM((2, 2, HEADS_PER_GROUP * ATTN_BLOCK, 2 * ATTN_BLOCK), F32)],
        compiler_params=pltpu.CompilerParams(dimension_semantics=("arbitrary", "arbitrary"),
                                             vmem_limit_bytes=VMEM_LIMIT_BYTES),
        name="attention",
    )(q, k, v, w_ff1, w_ff2)


def _mix_attention_groups(o_ref, lse_ref):
    lane = lax.broadcasted_iota(jnp.int32, (1, LANES), 1)
    outs, lses, dens = [], [], []
    for g, (_, dilation) in enumerate(ATTN_GROUPS):
        o = _ungroup_rows(o_ref[g], dilation).astype(F32)
        lse = _ungroup_rows(lse_ref[g], dilation)
        den = jnp.where(lane < HEADS_PER_GROUP, pltpu.roll(lse, LANES - HEADS_PER_GROUP, 1), 1.0)
        outs.append(o)
        dens.append(den)
        lses.append(lse + jnp.log2(den))
    lse_max = functools.reduce(jnp.maximum, lses)
    weights = [jnp.exp2(lse - lse_max) for lse in lses]
    inv_sum = 1.0 / functools.reduce(jnp.add, weights)
    in_head = _in_head_masks()
    mixed = None
    for o, w, den in zip(outs, weights, dens):
        w = w * inv_sum / den
        w_lanes = None
        for hd in range(HEADS_PER_GROUP):
            col = jnp.sum(jnp.where(lane == hd, w, 0.0), axis=-1, keepdims=True)
            w_lanes = jnp.broadcast_to(col, o.shape) if w_lanes is None else jnp.where(in_head[hd], col, w_lanes)
        mixed = w_lanes * o if mixed is None else mixed + w_lanes * o
    return mixed


def _merge_kernel(x_ref, gain_ref, w_ref, bg_ref, d_ref, wmix_ref, pscale_ref, yc_ref, o_ref, lse_ref,
                  wp_ref, wc_ref, wa_ref, wo_ref, gain2_ref, xo_ref, h2_ref):
    dm = x_ref.shape[1]
    x = x_ref[...]
    h = _rms_norm(x, gain_ref[...]).astype(BF16)
    y_attn = _mix_attention_groups(o_ref, lse_ref).astype(BF16)

    y_pool = jnp.concatenate(
        [_dot(d_ref[:, g * POOL_GROUP:(g + 1) * POOL_GROUP], wmix_ref[g]) for g in range(len(POOL_WINDOWS))],
        axis=-1)
    y_pool = (y_pool * pscale_ref[...]).astype(BF16)

    merged = None
    for j, (y, wj_ref) in enumerate(((y_pool, wp_ref), (yc_ref[...], wc_ref), (y_attn, wa_ref))):
        gate_cols = slice(j * dm, (j + 1) * dm)
        gate = 0.5 * jnp.tanh(0.5 * (_dot(h, w_ref[:, gate_cols]) + bg_ref[:, gate_cols])) + 0.5
        term = gate * _dot(y, wj_ref[...])
        merged = term if merged is None else merged + term

    x_new = x + _dot(merged.astype(BF16), wo_ref[...])
    xo_ref[...] = x_new
    h2_ref[...] = _rms_norm(x_new, gain2_ref[...]).astype(BF16)


def _merge(x2d, gain, w_gates, b_gate, d, w_mix, pool_scale, yc, attn_o, attn_lse, w_pool_up, w_conv_out,
           w_attn_up, w_o, gain2):
    t, dm = x2d.shape
    tm = TM_MERGE
    seq = attn_o.shape[2]
    assert t % tm == 0 and seq % tm == 0 and TM_MERGE == TM_IN
    tiles_per_seq = seq // tm
    row = lambda width: pl.BlockSpec((tm, width), lambda i: (i, 0))
    groups = lambda width: pl.BlockSpec((None, N_GROUPS, tm, width),
                                        lambda i: (i // tiles_per_seq, 0, i % tiles_per_seq, 0))
    res = lambda w: _resident(w.shape)
    return pl.pallas_call(
        _merge_kernel,
        grid=(t // tm,),
        in_specs=[row(dm), _resident((1, dm)), res(w_gates), res(b_gate), row(POOL_WIDTH), res(w_mix),
                  _resident((1, POOL_WIDTH)), row(CONV_WIDTH), groups(GROUP_WIDTH), groups(LANES),
                  res(w_pool_up), res(w_conv_out), res(w_attn_up), res(w_o), _resident((1, dm))],
        out_specs=[row(dm), row(dm)],
        out_shape=[jax.ShapeDtypeStruct((t, dm), F32), jax.ShapeDtypeStruct((t, dm), BF16)],
        compiler_params=pltpu.CompilerParams(dimension_semantics=("arbitrary",),
                                             vmem_limit_bytes=VMEM_LIMIT_BYTES),
        name="merge",
    )(x2d, gain, w_gates, b_gate, d, w_mix, pool_scale, yc, attn_o, attn_lse, w_pool_up, w_conv_out, w_attn_up,
      w_o, gain2)


def _ffn_kernel(x_ref, h2_ref, w1_ref, w2_ref, *rest, cast_layout):
    o_ref = rest[len(cast_layout)]
    _run_cast_jobs(rest[:len(cast_layout)], rest[len(cast_layout) + 1:], cast_layout)
    d_ff = w1_ref.shape[1]
    h2 = h2_ref[...]
    acc = None
    for c in range(d_ff // FF_CHUNK):
        cols = slice(c * FF_CHUNK, (c + 1) * FF_CHUNK)
        a = jnp.square(jnp.maximum(_dot(h2, w1_ref[:, cols]), 0.0)).astype(BF16)
        part = _dot(a, w2_ref[cols, :])
        acc = part if acc is None else acc + part
    o_ref[...] = x_ref[...] + acc


def _ffn(x2d, h2, w1, w2, cast_jobs=()):
    t, dm = x2d.shape
    tm = TM_FFN
    assert t % tm == 0 and w1.shape[1] % FF_CHUNK == 0
    steps = t // tm
    row = pl.BlockSpec((tm, dm), lambda i: (i, 0))
    cast_in, cast_out, cast_shape, cast_layout = _cast_jobs_specs(cast_jobs, steps)
    return pl.pallas_call(
        functools.partial(_ffn_kernel, cast_layout=cast_layout),
        grid=(steps,),
        in_specs=[row, row, _resident(w1.shape), _resident(w2.shape), *cast_in],
        out_specs=[row] + cast_out,
        out_shape=[jax.ShapeDtypeStruct((t, dm), F32)] + cast_shape,
        compiler_params=pltpu.CompilerParams(dimension_semantics=("arbitrary",),
                                             vmem_limit_bytes=VMEM_LIMIT_BYTES),
        name="ffn",
    )(x2d, h2, w1, w2, *(w for w, _, _ in cast_jobs))


def kernel(x, norm_mix, w_in, b_gate, pool_mix, pool_scale, conv_w, q_gain, k_gain, w_pool_up, w_conv_out,
           w_attn_up, w_o, norm_mlp, w_ff1, w_ff2):
    batch, seq, dm = x.shape
    depth = norm_mix.shape[0]
    assert w_in.shape[2] == OFF_GATE + N_BRANCH * dm
    head_id = jnp.arange(GROUP_WIDTH) // HEAD_DIM
    headmean = jnp.where(head_id[:, None] == head_id[None, :], 1.0 / HEAD_DIM, 0.0).astype(BF16)
    x2d = x.reshape(batch * seq, dm)
    pool_mix = pool_mix.astype(BF16)
    up_weights = (w_pool_up, w_conv_out, w_attn_up, w_o)
    in_cols = w_in.shape[2]
    late_jobs = lambda l: [(w_in, l, (OFF_GATE, in_cols))] + [(w, l, (0, w.shape[2])) for w in up_weights]
    layer_jobs = lambda l: [(w_in, l, (0, OFF_GATE, in_cols))] + late_jobs(l)[1:]
    w_mixers, = _cast_weights([(w_in, 0, (0, OFF_GATE))])
    for l in range(depth):
        gain = norm_mix[l].reshape(1, dm)
        qgain = (jnp.tile(q_gain[l], HEADS_PER_GROUP) * (LOG2_E * HEAD_DIM ** -0.5)).reshape(1, GROUP_WIDTH)
        kgain = jnp.tile(k_gain[l], HEADS_PER_GROUP).reshape(1, GROUP_WIDTH)
        d, yc, q, k, v, *late = _in_proj(x2d, gain, w_mixers, qgain, kgain, conv_w[l], headmean,
                                         late_jobs(0) if l == 0 else (), batch=batch, seq=seq)
        if l == 0:
            w_gates, *ups = late
        attn_o, attn_lse, w1, w2 = _attention(q, k, v, w_ff1, w_ff2, layer=l, tile=TM_IN)
        x2d, h2 = _merge(x2d, gain, w_gates, b_gate[l].reshape(1, N_BRANCH * dm), d, pool_mix[l],
                         pool_scale[l].reshape(1, POOL_WIDTH), yc, attn_o, attn_lse, *ups,
                         norm_mlp[l].reshape(1, dm))
        if l + 1 < depth:
            x2d, w_mixers, w_gates, *ups = _ffn(x2d, h2, w1, w2, layer_jobs(l + 1))
        else:
            x2d, = _ffn(x2d, h2, w1, w2)
    return x2d.reshape(batch, seq, dm)
```

```python
import functools

import jax
import jax.numpy as jnp
from jax import lax
from jax.experimental import pallas as pl
from jax.experimental.pallas import tpu as pltpu

F32 = jnp.float32
BF16 = jnp.bfloat16

POOL_WINDOWS = (2, 4, 8, 16)
POOL_GROUP = 128
POOL_WIDTH = POOL_GROUP * len(POOL_WINDOWS)
CONV_WIDTH = 512
CONV_K = 3
HEAD_DIM = 64
ATTN_GROUPS = ((128, 1), (512, 4), (2048, 16))
HEADS_PER_GROUP = 4
GROUP_WIDTH = HEADS_PER_GROUP * HEAD_DIM
N_GROUPS = len(ATTN_GROUPS)
ATTN_WIDTH = N_GROUPS * GROUP_WIDTH
ATTN_BLOCK = 128
LANES = 128
BF16_ROWS = 16
N_BRANCH = 3
EPS = 1e-6
MASK_VALUE = -1e30
LOG2_E = 1.4426950408889634

OFF_POOL = 0
OFF_CONV = OFF_POOL + POOL_WIDTH
OFF_Q = OFF_CONV + 3 * CONV_WIDTH
OFF_K = OFF_Q + ATTN_WIDTH
OFF_V = OFF_K + ATTN_WIDTH
OFF_GATE = OFF_V + ATTN_WIDTH

POOL_HALO = 16
CONV_HALO = 8

VMEM_LIMIT_BYTES = 56 * 1024 * 1024
TM_IN = 1024
TM_MERGE = 1024
TM_FFN = 1024
FF_CHUNK = 2048
CAST_STEPS = 8
ITEM_BLOCKS = 2
CAST_GROUPS = 2


def _resident(shape):
    return pl.BlockSpec(shape, lambda *_: (0,) * len(shape), pipeline_mode=pl.Buffered(1))


def _cast_specs(w, layer, steps, bounds):
    _, rows, cols = w.shape
    assert rows % steps == 0 and (rows // steps) % BF16_ROWS == 0
    rb = rows // steps
    widths = [b - a for a, b in zip(bounds, bounds[1:])]
    fetched = bounds[-1] if bounds[0] == 0 else cols
    in_spec = pl.BlockSpec((None, rb, fetched), lambda i: (layer, i, 0))
    out_specs = [pl.BlockSpec((rb, wd), lambda i: (i, 0)) for wd in widths]
    out_shapes = [jax.ShapeDtypeStruct((rows, wd), BF16) for wd in widths]
    return in_spec, out_specs, out_shapes


def _cast_rows(src_ref, dst_refs, start=0):
    for dst in dst_refs:
        dst[...] = src_ref[:, start:start + dst.shape[1]].astype(BF16)
        start += dst.shape[1]


def _cast_jobs_specs(jobs, steps):
    in_specs, out_specs, out_shapes, layout = [], [], [], []
    for w, layer, bounds in jobs:
        in_spec, o_specs, o_shapes = _cast_specs(w, layer, steps, bounds)
        in_specs.append(in_spec)
        out_specs += o_specs
        out_shapes += o_shapes
        layout.append((len(o_specs), bounds[0]))
    return in_specs, out_specs, out_shapes, tuple(layout)


def _run_cast_jobs(srcs, dsts, layout):
    dsts = list(dsts)
    for src, (n_dst, first_col) in zip(srcs, layout):
        _cast_rows(src, [dsts.pop(0) for _ in range(n_dst)], first_col)


def _cast_kernel(*refs, layout):
    _run_cast_jobs(refs[:len(layout)], refs[len(layout):], layout)


def _cast_weights(jobs):
    in_specs, out_specs, out_shapes, layout = _cast_jobs_specs(jobs, CAST_STEPS)
    return pl.pallas_call(
        functools.partial(_cast_kernel, layout=layout),
        grid=(CAST_STEPS,),
        in_specs=in_specs,
        out_specs=out_specs,
        out_shape=out_shapes,
        compiler_params=pltpu.CompilerParams(dimension_semantics=("arbitrary",),
                                             vmem_limit_bytes=VMEM_LIMIT_BYTES),
        name="cast_weights",
    )(*(w for w, _, _ in jobs))


def _rms_norm(x, gain):
    ms = jnp.mean(x * x, axis=-1, keepdims=True)
    return x * lax.rsqrt(ms + EPS) * gain


def _dot(a, b):
    return jnp.dot(a, b, preferred_element_type=F32)


def _regroup_rows(z, dilation):
    if dilation == 1:
        return z
    rows, width = z.shape
    return pltpu.einshape("crl->rcl", z.reshape(rows // dilation, dilation, width)).reshape(rows, width)


def _ungroup_rows(z, dilation):
    if dilation == 1:
        return z
    rows, width = z.shape
    return pltpu.einshape("rcl->crl", z.reshape(dilation, rows // dilation, width)).reshape(rows, width)


def _in_proj_kernel(x_ref, gain_ref, w_ref, qgain_ref, kgain_ref, convw_ref, headmean_ref, *rest,
                    tm, tiles_per_seq, cast_layout):
    n_src = len(cast_layout)
    d_ref, yc_ref, q_ref, k_ref, v_ref = rest[n_src:n_src + 5]
    uhalo, chalo = rest[-2:]
    _run_cast_jobs(rest[:n_src], rest[n_src + 5:-2], cast_layout)
    i = pl.program_id(0)

    @pl.when(i == 0)
    def _():
        uhalo[...] = jnp.zeros_like(uhalo)
        chalo[...] = jnp.zeros_like(chalo)

    tile_in_seq = i % tiles_per_seq
    carry = tile_in_seq != 0
    h = _rms_norm(x_ref[...], gain_ref[...]).astype(BF16)

    def emit_heads(z_all, slot, gain, o_ref):
        zs = [z_all[:, g * GROUP_WIDTH:(g + 1) * GROUP_WIDTH] for g in range(N_GROUPS)]
        if gain is not None:
            ms = _dot(jnp.concatenate([(z * z).astype(BF16) for z in zs], axis=0), headmean_ref[...])
            zs = [z * lax.rsqrt(ms[g * tm:(g + 1) * tm, :] + EPS) * gain[...] for g, z in enumerate(zs)]
        for g, (_, dilation) in enumerate(ATTN_GROUPS):
            o_ref[g] = _regroup_rows(zs[g].astype(BF16), dilation)

    zpc = _dot(h, w_ref[:, OFF_POOL:OFF_Q])

    u = zpc[:, OFF_POOL:OFF_POOL + POOL_WIDTH]
    ext = jnp.concatenate([jnp.where(carry, uhalo[...], 0.0), u], axis=0)
    uhalo[...] = u[tm - POOL_HALO:tm, :]
    pos = tile_in_seq * tm + lax.broadcasted_iota(jnp.int32, (tm, 1), 0)
    run = ext
    for g, w in enumerate(POOL_WINDOWS):
        run = run + pltpu.roll(run, w // 2, 0)
        cols = slice(g * POOL_GROUP, (g + 1) * POOL_GROUP)
        inv_count = 1.0 / jnp.minimum(pos + 1, w).astype(F32)
        d_ref[:, cols] = (run[POOL_HALO:, 0:POOL_GROUP] * inv_count - u[:, cols]).astype(BF16)
        run = run[:, POOL_GROUP:]

    zc = zpc[:, OFF_CONV:OFF_CONV + 3 * CONV_WIDTH]
    uc = zc[:, CONV_WIDTH:2 * CONV_WIDTH] * zc[:, 2 * CONV_WIDTH:3 * CONV_WIDTH]
    ext = jnp.concatenate([jnp.where(carry, chalo[...], 0.0), uc], axis=0)
    chalo[...] = uc[tm - CONV_HALO:tm, :]
    y = convw_ref[CONV_K - 1:CONV_K, :] * uc
    for j in range(CONV_K - 1):
        y = y + convw_ref[j:j + 1, :] * pltpu.roll(ext, CONV_K - 1 - j, 0)[CONV_HALO:, :]
    yc_ref[...] = (zc[:, 0:CONV_WIDTH] * y).astype(BF16)

    zqkv = _dot(h, w_ref[:, OFF_Q:OFF_GATE])
    for slot, (gain, o_ref) in enumerate(((qgain_ref, q_ref), (kgain_ref, k_ref), (None, v_ref))):
        emit_heads(zqkv[:, slot * ATTN_WIDTH:(slot + 1) * ATTN_WIDTH], slot, gain, o_ref)


def _in_proj(x2d, gain, w_mixers, qgain, kgain, convw, headmean, cast_jobs=(), *, batch, seq):
    t, dm = x2d.shape
    assert w_mixers.shape == (dm, OFF_GATE)
    tm = TM_IN
    tiles_per_seq = seq // tm
    assert seq % tm == 0 and all(tm % (16 * d) == 0 for _, d in ATTN_GROUPS)
    row = lambda width: pl.BlockSpec((tm, width), lambda i: (i, 0))
    qkv_spec = pl.BlockSpec((None, N_GROUPS, tm, GROUP_WIDTH),
                            lambda i: (i // tiles_per_seq, 0, i % tiles_per_seq, 0))
    qkv_shape = jax.ShapeDtypeStruct((batch, N_GROUPS, seq, GROUP_WIDTH), BF16)
    cast_in, cast_out, cast_shape, cast_layout = _cast_jobs_specs(cast_jobs, t // tm)
    return pl.pallas_call(
        functools.partial(_in_proj_kernel, tm=tm, tiles_per_seq=tiles_per_seq, cast_layout=cast_layout),
        grid=(t // tm,),
        in_specs=[row(dm), _resident((1, dm)), _resident(w_mixers.shape),
                  _resident((1, GROUP_WIDTH)), _resident((1, GROUP_WIDTH)), _resident((CONV_K, CONV_WIDTH)),
                  _resident((GROUP_WIDTH, GROUP_WIDTH)), *cast_in],
        out_specs=[row(POOL_WIDTH), row(CONV_WIDTH), qkv_spec, qkv_spec, qkv_spec] + cast_out,
        out_shape=[jax.ShapeDtypeStruct((t, POOL_WIDTH), BF16), jax.ShapeDtypeStruct((t, CONV_WIDTH), BF16),
                   qkv_shape, qkv_shape, qkv_shape] + cast_shape,
        scratch_shapes=[pltpu.VMEM((POOL_HALO, POOL_WIDTH), F32), pltpu.VMEM((CONV_HALO, CONV_WIDTH), F32)],
        compiler_params=pltpu.CompilerParams(dimension_semantics=("arbitrary",),
                                             vmem_limit_bytes=VMEM_LIMIT_BYTES),
        name="in_proj",
    )(x2d, gain, w_mixers, qgain, kgain, convw, headmean, *(w for w, _, _ in cast_jobs))


def _in_head_masks():
    lane_head = lax.broadcasted_iota(jnp.int32, (1, GROUP_WIDTH), 1) // HEAD_DIM
    return [lane_head == hd for hd in range(HEADS_PER_GROUP)]


def _attn_scores(qb, kb):
    q_heads = jnp.concatenate([jnp.where(m, qb, jnp.zeros_like(qb)) for m in _in_head_masks()], axis=0)
    return lax.dot_general(q_heads, kb, (((1,), (1,)), ((), ())), preferred_element_type=F32)


def _attn_softmax(s_ref, first):
    blk = ATTN_BLOCK
    row = lax.broadcasted_iota(jnp.int32, (blk, blk), 0)
    col = lax.broadcasted_iota(jnp.int32, (blk, blk), 1)
    mask_value = MASK_VALUE * LOG2_E
    probs, row_max, row_den = [], [], []
    for hd in range(HEADS_PER_GROUP):
        rows = slice(hd * blk, (hd + 1) * blk)

        def masked():
            if first:
                return jnp.where(col <= row, s_ref[rows, 0:blk], mask_value)
            return jnp.concatenate([jnp.where(col >= row, s_ref[rows, 0:blk], mask_value),
                                    jnp.where(col <= row, s_ref[rows, blk:2 * blk], mask_value)], axis=1)

        m = jnp.max(masked(), axis=-1, keepdims=True)
        p = jnp.exp2(masked() - m)
        row_max.append(m)
        row_den.append(jnp.sum(p, axis=-1, keepdims=True))
        probs.append(p.astype(BF16))
    return jnp.concatenate(probs, axis=0), row_max, row_den


def _attn_values(probs, row_max, row_den, vb):
    blk = ATTN_BLOCK
    in_head = _in_head_masks()
    lane = lax.broadcasted_iota(jnp.int32, (1, LANES), 1)
    pv = _dot(probs, vb)
    out, stats = pv[0:blk, :], jnp.ones((blk, LANES), F32)
    for hd in range(HEADS_PER_GROUP):
        if hd:
            out = jnp.where(in_head[hd], pv[hd * blk:(hd + 1) * blk, :], out)
        stats = jnp.where(lane == hd, row_max[hd], stats)
        stats = jnp.where(lane == HEADS_PER_GROUP + hd, row_den[hd], stats)
    return out, stats


def _block_row_slices(n, r, dilation, tile):
    chunk = tile // dilation
    piece = min(ATTN_BLOCK, chunk)
    slices = []
    for j in range(ATTN_BLOCK // piece):
        pos = n * ATTN_BLOCK + j * piece
        start = (pos // chunk) * tile + r * chunk + pos % chunk
        if not isinstance(start, int):
            start = pl.multiple_of(start, piece)
        slices.append(pl.ds(start, piece))
    return slices


def _block_rows(ref, n, r, dilation, tile):
    parts = [ref[rows, :] for rows in _block_row_slices(n, r, dilation, tile)]
    return parts[0] if len(parts) == 1 else jnp.concatenate(parts, axis=0)


def _store_block_rows(ref, n, r, dilation, tile, val):
    start = 0
    for rows in _block_row_slices(n, r, dilation, tile):
        ref[rows, :] = val[start:start + rows.size, :]
        start += rows.size


def _attn_group(dilation, q_ref, k_ref, v_ref, o_ref, lse_ref, score_s, *, seq, tile):
    blk = ATTN_BLOCK
    n_blocks = seq // dilation // blk
    assert n_blocks % 2 == 0

    def keys_values(ref, n, r, first):
        own = _block_rows(ref, n, r, dilation, tile)
        if first:
            return own
        return jnp.concatenate([_block_rows(ref, n - 1, r, dilation, tile), own], axis=0)

    width = 2 if n_blocks == 2 else ITEM_BLOCKS

    def pair(n, first):
        return tuple((n + i, first and i == 0) for i in range(width))

    def park_scores(slot, n, r, first):
        for i, (nb, fb) in enumerate(pair(n, first)):
            s = _attn_scores(_block_rows(q_ref, nb, r, dilation, tile), keys_values(k_ref, nb, r, fb))
            score_s[slot, i, :, 0:s.shape[1]] = s

    def finish(slot, n, r, first):
        blocks = pair(n, first)
        soft = [_attn_softmax(score_s.at[slot, i], fb) for i, (_, fb) in enumerate(blocks)]
        new = [_attn_values(*sm, keys_values(v_ref, nb, r, fb)) for sm, (nb, fb) in zip(soft, blocks)]
        for (nb, _), (out, lse) in zip(blocks, new):
            _store_block_rows(o_ref, nb, r, dilation, tile, out.astype(o_ref.dtype))
            _store_block_rows(lse_ref, nb, r, dilation, tile, lse)

    def run_items(n_items, item):
        assert n_items >= 2 and n_items % 2 == 0
        park_scores(0, *item(0, True))
        park_scores(1, *item(1, False))
        finish(0, *item(0, True))

        def double_step(t, c):
            i = 2 * t + 1
            park_scores(0, *item(i + 1, False))
            finish(1, *item(i, False))
            park_scores(1, *item(i + 2, False))
            finish(0, *item(i + 1, False))
            return c

        lax.fori_loop(0, (n_items - 2) // 2, double_step, 0)
        finish(1, *item(n_items - 1, False))

    if n_blocks == 2:
        run_items(dilation, lambda i, lead: (0, i, True))
    elif dilation == 1:
        run_items(n_blocks // width, lambda i, lead: (width * i, 0, lead))
    else:
        def per_residue(r, carry):
            run_items(n_blocks // width, lambda i, lead: (width * i, r, lead))
            return carry

        lax.fori_loop(0, dilation, per_residue, 0)


def _attn_kernel(q_ref, k_ref, v_ref, wff1_src, wff2_src, o_ref, lse_ref, wff1_dst, wff2_dst, score_s, *,
                 seq, tile):
    g = pl.program_id(1)

    @pl.when(g < CAST_GROUPS)
    def _():
        _cast_rows(wff1_src, [wff1_dst])
        _cast_rows(wff2_src, [wff2_dst])

    for gi, (_, dilation) in enumerate(ATTN_GROUPS):
        @pl.when(g == gi)
        def _(dilation=dilation):
            _attn_group(dilation, q_ref, k_ref, v_ref, o_ref, lse_ref, score_s, seq=seq, tile=tile)


def _attention(q, k, v, w_ff1, w_ff2, *, layer, tile):
    batch, _, seq, _ = q.shape
    spec = lambda width: pl.BlockSpec((None, None, seq, width), lambda b, g: (b, g, 0, 0))
    cast_step = lambda b, g: b * CAST_GROUPS + jnp.minimum(g, CAST_GROUPS - 1)
    cast_in, cast_out, cast_shape = [], [], []
    for w in (w_ff1, w_ff2):
        _, rows, cols = w.shape
        rb = rows // (batch * CAST_GROUPS)
        assert rows % (batch * CAST_GROUPS) == 0 and rb % BF16_ROWS == 0
        cast_in.append(pl.BlockSpec((None, rb, cols), lambda b, g: (layer, cast_step(b, g), 0)))
        cast_out.append(pl.BlockSpec((rb, cols), lambda b, g: (cast_step(b, g), 0)))
        cast_shape.append(jax.ShapeDtypeStruct((rows, cols), BF16))
    return pl.pallas_call(
        functools.partial(_attn_kernel, seq=seq, tile=tile),
        grid=(batch, N_GROUPS),
        in_specs=[spec(GROUP_WIDTH)] * 3 + cast_in,
        out_specs=[spec(GROUP_WIDTH), spec(LANES)] + cast_out,
        out_shape=[jax.ShapeDtypeStruct((batch, N_GROUPS, seq, GROUP_WIDTH), BF16),
                   jax.ShapeDtypeStruct((batch, N_GROUPS, seq, LANES), F32)] + cast_shape,
        scratch_shapes=[pltpu.VMEM((2, 2, HEADS_PER_GROUP * ATTN_BLOCK, 2 * ATTN_BLOCK), F32)],
        compiler_params=pltpu.CompilerParams(dimension_semantics=("arbitrary", "arbitrary"),
                                             vmem_limit_bytes=VMEM_LIMIT_BYTES),
        name="attention",
    )(q, k, v, w_ff1, w_ff2)


def _mix_attention_groups(o_ref, lse_ref):
    lane = lax.broadcasted_iota(jnp.int32, (1, LANES), 1)
    outs, lses, dens = [], [], []
    for g, (_, dilation) in enumerate(ATTN_GROUPS):
        o = _ungroup_rows(o_ref[g], dilation).astype(F32)
        lse = _ungroup_rows(lse_ref[g], dilation)
        den = jnp.where(lane < HEADS_PER_GROUP, pltpu.roll(lse, LANES - HEADS_PER_GROUP, 1), 1.0)
        outs.append(o)
        dens.append(den)
        lses.append(lse + jnp.log2(den))
    lse_max = functools.reduce(jnp.maximum, lses)
    weights = [jnp.exp2(lse - lse_max) for lse in lses]
    inv_sum = 1.0 / functools.reduce(jnp.add, weights)
    in_head = _in_head_masks()
    mixed = None
    for o, w, den in zip(outs, weights, dens):
        w = w * inv_sum / den
        w_lanes = None
        for hd in range(HEADS_PER_GROUP):
            col = jnp.sum(jnp.where(lane == hd, w, 0.0), axis=-1, keepdims=True)
            w_lanes = jnp.broadcast_to(col, o.shape) if w_lanes is None else jnp.where(in_head[hd], col, w_lanes)
        mixed = w_lanes * o if mixed is None else mixed + w_lanes * o
    return mixed


def _merge_kernel(x_ref, gain_ref, w_ref, bg_ref, d_ref, wmix_ref, pscale_ref, yc_ref, o_ref, lse_ref,
                  wp_ref, wc_ref, wa_ref, wo_ref, xo_ref):
    dm = x_ref.shape[1]
    x = x_ref[...]
    h = _rms_norm(x, gain_ref[...]).astype(BF16)
    y_attn = _mix_attention_groups(o_ref, lse_ref).astype(BF16)

    y_pool = jnp.concatenate(
        [_dot(d_ref[:, g * POOL_GROUP:(g + 1) * POOL_GROUP], wmix_ref[g]) for g in range(len(POOL_WINDOWS))],
        axis=-1)
    y_pool = (y_pool * pscale_ref[...]).astype(BF16)

    merged = None
    for j, (y, wj_ref) in enumerate(((y_pool, wp_ref), (yc_ref[...], wc_ref), (y_attn, wa_ref))):
        gate_cols = slice(j * dm, (j + 1) * dm)
        gate = 0.5 * jnp.tanh(0.5 * (_dot(h, w_ref[:, gate_cols]) + bg_ref[:, gate_cols])) + 0.5
        term = gate * _dot(y, wj_ref[...])
        merged = term if merged is None else merged + term

    xo_ref[...] = x + _dot(merged.astype(BF16), wo_ref[...])


def _merge(x2d, gain, w_gates, b_gate, d, w_mix, pool_scale, yc, attn_o, attn_lse, w_pool_up, w_conv_out,
           w_attn_up, w_o):
    t, dm = x2d.shape
    tm = TM_MERGE
    seq = attn_o.shape[2]
    assert t % tm == 0 and seq % tm == 0 and TM_MERGE == TM_IN
    tiles_per_seq = seq // tm
    row = lambda width: pl.BlockSpec((tm, width), lambda i: (i, 0))
    groups = lambda width: pl.BlockSpec((None, N_GROUPS, tm, width),
                                        lambda i: (i // tiles_per_seq, 0, i % tiles_per_seq, 0))
    res = lambda w: _resident(w.shape)
    return pl.pallas_call(
        _merge_kernel,
        grid=(t // tm,),
        in_specs=[row(dm), _resident((1, dm)), res(w_gates), res(b_gate), row(POOL_WIDTH), res(w_mix),
                  _resident((1, POOL_WIDTH)), row(CONV_WIDTH), groups(GROUP_WIDTH), groups(LANES),
                  res(w_pool_up), res(w_conv_out), res(w_attn_up), res(w_o)],
        out_specs=row(dm),
        out_shape=jax.ShapeDtypeStruct((t, dm), F32),
        compiler_params=pltpu.CompilerParams(dimension_semantics=("arbitrary",),
                                             vmem_limit_bytes=VMEM_LIMIT_BYTES),
        name="merge",
    )(x2d, gain, w_gates, b_gate, d, w_mix, pool_scale, yc, attn_o, attn_lse, w_pool_up, w_conv_out, w_attn_up,
      w_o)


def _ffn_kernel(x_ref, gain2_ref, w1_ref, w2_ref, *rest, cast_layout):
    o_ref = rest[len(cast_layout)]
    _run_cast_jobs(rest[:len(cast_layout)], rest[len(cast_layout) + 1:], cast_layout)
    d_ff = w1_ref.shape[1]
    h2 = _rms_norm(x_ref[...], gain2_ref[...]).astype(BF16)
    acc = None
    for c in range(d_ff // FF_CHUNK):
        cols = slice(c * FF_CHUNK, (c + 1) * FF_CHUNK)
        a = jnp.square(jnp.maximum(_dot(h2, w1_ref[:, cols]), 0.0)).astype(BF16)
        part = _dot(a, w2_ref[cols, :])
        acc = part if acc is None else acc + part
    o_ref[...] = x_ref[...] + acc


def _ffn(x2d, gain2, w1, w2, cast_jobs=()):
    t, dm = x2d.shape
    tm = TM_FFN
    assert t % tm == 0 and w1.shape[1] % FF_CHUNK == 0
    steps = t // tm
    row = pl.BlockSpec((tm, dm), lambda i: (i, 0))
    cast_in, cast_out, cast_shape, cast_layout = _cast_jobs_specs(cast_jobs, steps)
    return pl.pallas_call(
        functools.partial(_ffn_kernel, cast_layout=cast_layout),
        grid=(steps,),
        in_specs=[row, _resident((1, dm)), _resident(w1.shape), _resident(w2.shape), *cast_in],
        out_specs=[row] + cast_out,
        out_shape=[jax.ShapeDtypeStruct((t, dm), F32)] + cast_shape,
        compiler_params=pltpu.CompilerParams(dimension_semantics=("arbitrary",),
                                             vmem_limit_bytes=VMEM_LIMIT_BYTES),
        name="ffn",
    )(x2d, gain2, w1, w2, *(w for w, _, _ in cast_jobs))


def kernel(x, norm_mix, w_in, b_gate, pool_mix, pool_scale, conv_w, q_gain, k_gain, w_pool_up, w_conv_out,
           w_attn_up, w_o, norm_mlp, w_ff1, w_ff2):
    batch, seq, dm = x.shape
    depth = norm_mix.shape[0]
    assert w_in.shape[2] == OFF_GATE + N_BRANCH * dm
    head_id = jnp.arange(GROUP_WIDTH) // HEAD_DIM
    headmean = jnp.where(head_id[:, None] == head_id[None, :], 1.0 / HEAD_DIM, 0.0).astype(BF16)
    x2d = x.reshape(batch * seq, dm)
    pool_mix = pool_mix.astype(BF16)
    up_weights = (w_pool_up, w_conv_out, w_attn_up, w_o)
    in_cols = w_in.shape[2]
    late_jobs = lambda l: [(w_in, l, (OFF_GATE, in_cols))] + [(w, l, (0, w.shape[2])) for w in up_weights]
    layer_jobs = lambda l: [(w_in, l, (0, OFF_GATE, in_cols))] + late_jobs(l)[1:]
    w_mixers, = _cast_weights([(w_in, 0, (0, OFF_GATE))])
    for l in range(depth):
        gain = norm_mix[l].reshape(1, dm)
        qgain = (jnp.tile(q_gain[l], HEADS_PER_GROUP) * (LOG2_E * HEAD_DIM ** -0.5)).reshape(1, GROUP_WIDTH)
        kgain = jnp.tile(k_gain[l], HEADS_PER_GROUP).reshape(1, GROUP_WIDTH)
        d, yc, q, k, v, *late = _in_proj(x2d, gain, w_mixers, qgain, kgain, conv_w[l], headmean,
                                         late_jobs(0) if l == 0 else (), batch=batch, seq=seq)
        if l == 0:
            w_gates, *ups = late
        attn_o, attn_lse, w1, w2 = _attention(q, k, v, w_ff1, w_ff2, layer=l, tile=TM_IN)
        x2d = _merge(x2d, gain, w_gates, b_gate[l].reshape(1, N_BRANCH * dm), d, pool_mix[l],
                     pool_scale[l].reshape(1, POOL_WIDTH), yc, attn_o, attn_lse, *ups)
        gain2 = norm_mlp[l].reshape(1, dm)
        if l + 1 < depth:
            x2d, w_mixers, w_gates, *ups = _ffn(x2d, gain2, w1, w2, layer_jobs(l + 1))
        else:
            x2d, = _ffn(x2d, gain2, w1, w2)
    return x2d.reshape(batch, seq, dm)
```
